```python
import jax
import jax.numpy as jnp
from jax import lax
import numpy as np

D_MODEL = 1024
BATCH = 8
SEQ = 4096
DEPTH = 2

CHUNK = 64
NORM_EPS = 1e-6
MASK_VALUE = -1e30
MIN_FORGET = 1e-30

A_HEADS = 4
A_DK = 64
A_DV = 64
A_WIDTH = A_HEADS * A_DV

B_HEADS = 4
B_DK = 48
B_DV = 96
B_WIDTH = B_HEADS * B_DV
ROPE_BASE = 10000.0

C_HEADS = 6
C_DH = 64
C_WIDTH = C_HEADS * C_DH
N_PREV_CHUNKS = 8
MAX_REL = 128

D_MIX = A_WIDTH + B_WIDTH + C_WIDTH
IN_SIZES = (A_HEADS * A_DK, A_HEADS * A_DK, A_WIDTH, A_WIDTH,
            B_HEADS * B_DK, B_HEADS * B_DK, B_WIDTH, B_WIDTH,
            C_WIDTH, C_WIDTH, C_WIDTH)
D_IN = sum(IN_SIZES)

N_GROUPS = 4
EXPERTS_PER_GROUP = 8
N_EXPERTS = N_GROUPS * EXPERTS_PER_GROUP
TOP_K = 2
D_EXPERT = 512
MOE_BLOCK = 128

kernel_name = 'hybrid_hgrn2_retention_chunkattn_hiermoe'


def rmsnorm(x, w):
    xf = x.astype(jnp.float32)
    y = xf * lax.rsqrt(jnp.mean(xf * xf, axis=-1, keepdims=True) + NORM_EPS)
    return (y * w.astype(jnp.float32)).astype(x.dtype)


def hgrn2_mixer(q, f_logit, i, g, lb, norm_w):
    f32 = jnp.float32
    Bsz, S = q.shape[0], q.shape[1]
    nc = S // CHUNK
    lbh = lb.astype(f32).reshape(A_HEADS, A_DK)
    zf = f_logit.astype(f32)
    f = lbh + (1.0 - lbh) * jax.nn.sigmoid(zf)
    log_f = jnp.log(jnp.maximum(f, MIN_FORGET))
    k = (1.0 - lbh) * jax.nn.sigmoid(-zf)
    qf = jax.nn.silu(q.astype(f32)) * (A_DK ** -0.5)
    v = i.astype(f32)

    def to_chunks(t):
        return t.reshape(Bsz, nc, CHUNK, A_HEADS, t.shape[-1]).transpose(1, 0, 3, 2, 4)

    qc, kc, vc, lfc = to_chunks(qf), to_chunks(k), to_chunks(v), to_chunks(log_f)
    G = jnp.cumsum(lfc, axis=3)
    causal = jnp.tril(jnp.ones((CHUNK, CHUNK), dtype=bool))[None, None, :, :, None]

    def step(state, inp):
        qb, kb, vb, Gb = inp
        G_last = Gb[:, :, -1]
        inter = jnp.einsum('bhik,bhkv->bhiv', qb * jnp.exp(Gb), state)
        diff = Gb[:, :, :, None, :] - Gb[:, :, None, :, :]
        decay = jnp.where(causal, jnp.exp(jnp.minimum(diff, 0.0)), 0.0)
        scores = jnp.einsum('bhik,bhjk,bhijk->bhij', qb, kb, decay)
        intra = jnp.einsum('bhij,bhjv->bhiv', scores, vb)
        new_state = (jnp.exp(G_last)[..., None] * state
                     + jnp.einsum('bhjk,bhjv->bhkv', kb * jnp.exp(G_last[:, :, None] - Gb), vb))
        return new_state, inter + intra

    s0 = jnp.zeros((Bsz, A_HEADS, A_DK, A_DV), f32)
    _, o = lax.scan(step, s0, (qc, kc, vc, G))
    o = o.transpose(1, 0, 3, 2, 4).reshape(Bsz, S, A_WIDTH)
    o = rmsnorm(o, norm_w) * jax.nn.silu(g.astype(f32))
    return o.astype(g.dtype)


def rotary(t, pos):
    half = t.shape[-1] // 2
    inv_freq = ROPE_BASE ** (-jnp.arange(half, dtype=jnp.float32) / half)
    ang = pos.astype(jnp.float32)[:, None] * inv_freq[None, :]
    cos = jnp.cos(ang)[None, :, None, :]
    sin = jnp.sin(ang)[None, :, None, :]
    t1, t2 = t[..., :half], t[..., half:]
    return jnp.concatenate([t1 * cos - t2 * sin, t1 * sin + t2 * cos], axis=-1)


def retention_mixer(q, k, v, g, norm_w):
    f32 = jnp.float32
    Bsz, S = q.shape[0], q.shape[1]
    nc = S // CHUNK
    pos = jnp.arange(S)
    qf = rotary(q.astype(f32), pos)
    kf = rotary(k.astype(f32), pos) * (B_DK ** -0.5)
    vf = v.astype(f32)
    log_gamma = jnp.log1p(-jnp.exp2(-5.0 - jnp.arange(B_HEADS, dtype=f32)))

    def to_chunks(t):
        return t.reshape(Bsz, nc, CHUNK, B_HEADS, t.shape[-1]).transpose(0, 3, 1, 2, 4)

    qc, kc, vc = to_chunks(qf), to_chunks(kf), to_chunks(vf)
    idx = jnp.arange(CHUNK, dtype=f32)
    rel = idx[:, None] - idx[None, :]
    decay = jnp.where(rel >= 0, jnp.exp(log_gamma[:, None, None] * jnp.maximum(rel, 0.0)), 0.0)
    scores = jnp.einsum('bhnid,bhnjd->bhnij', qc, kc) * decay[None, :, None]
    intra = jnp.einsum('bhnij,bhnjv->bhniv', scores, vc)
    k_decay = jnp.exp(log_gamma[:, None] * (CHUNK - 1.0 - idx)[None, :])
    q_decay = jnp.exp(log_gamma[:, None] * (idx + 1.0)[None, :])
    chunk_gamma = jnp.exp(log_gamma * CHUNK)
    U = jnp.einsum('bhnjd,hj,bhnjv->nbhdv', kc, k_decay, vc)

    def step(state, u):
        return chunk_gamma[None, :, None, None] * state + u, state

    s0 = jnp.zeros((Bsz, B_HEADS, B_DK, B_DV), f32)
    _, s_prev = lax.scan(step, s0, U)
    inter = jnp.einsum('bhnid,hi,nbhdv->bhniv', qc, q_decay, s_prev)
    o = (intra + inter).transpose(0, 2, 3, 1, 4).reshape(Bsz, S, B_HEADS, B_DV)
    mu = jnp.mean(o, axis=-1, keepdims=True)
    var = jnp.mean(jnp.square(o - mu), axis=-1, keepdims=True)
    o = ((o - mu) * lax.rsqrt(var + NORM_EPS)).reshape(Bsz, S, B_WIDTH)
    o = o * norm_w.astype(f32) * jax.nn.silu(g.astype(f32))
    return o.astype(g.dtype)


def chunk_attention_mixer(q, k, v, rel_bias):
    f32 = jnp.float32
    Bsz, S = q.shape[0], q.shape[1]
    nc = S // CHUNK
    W = N_PREV_CHUNKS + 1
    qc = q.reshape(Bsz, nc, CHUNK, C_HEADS, C_DH)
    pad = ((0, 0), (N_PREV_CHUNKS, 0), (0, 0), (0, 0), (0, 0))
    kp = jnp.pad(k.reshape(Bsz, nc, CHUNK, C_HEADS, C_DH), pad)
    vp = jnp.pad(v.reshape(Bsz, nc, CHUNK, C_HEADS, C_DH), pad)
    kb = jnp.concatenate([kp[:, w:w + nc] for w in range(W)], axis=2)
    vb = jnp.concatenate([vp[:, w:w + nc] for w in range(W)], axis=2)
    qi = jnp.arange(CHUNK)[:, None]
    km = jnp.arange(W * CHUNK)[None, :]
    dist = N_PREV_CHUNKS * CHUNK + qi - km
    bias = rel_bias[:, jnp.clip(dist, -MAX_REL, MAX_REL) + MAX_REL]
    valid = (jnp.arange(nc)[:, None] - N_PREV_CHUNKS + km // CHUNK) >= 0
    s = (jnp.einsum('bnihd,bnjhd->bhnij', qc, kb).astype(f32) * (C_DH ** -0.5)
         + bias[None, :, None].astype(f32))
    s = jnp.where(valid[None, None, :, None, :], s, MASK_VALUE)
    p = jax.nn.softmax(s, axis=-1).astype(v.dtype)
    o = jnp.einsum('bhnij,bnjhd->bnihd', p, vb)
    return o.reshape(Bsz, S, C_WIDTH)


def hier_moe(h, w_group, w_expert, w_gate, w_up, w_down):
    f32 = jnp.float32
    Bsz, S, D = h.shape
    T = Bsz * S
    x = h.reshape(T, D)
    g_logits = (x @ w_group).astype(f32)
    grp = jnp.argmax(g_logits, axis=-1)
    p_grp = jnp.take_along_axis(jax.nn.softmax(g_logits, axis=-1), grp[:, None], axis=-1)
    e_logits = (x @ w_expert).astype(f32).reshape(T, N_GROUPS, EXPERTS_PER_GROUP)
    e_logits = jnp.take_along_axis(e_logits, grp[:, None, None], axis=1)[:, 0]
    top_v, top_i = lax.top_k(e_logits, TOP_K)
    gates = jax.nn.softmax(top_v, axis=-1) * p_grp
    eid = grp[:, None] * EXPERTS_PER_GROUP + top_i

    A = T * TOP_K
    flat_e = eid.reshape(A)
    flat_t = jnp.arange(A, dtype=jnp.int32) // TOP_K
    flat_w = gates.reshape(A)
    order = jnp.argsort(flat_e)
    se = flat_e[order]
    counts = jnp.bincount(flat_e, length=N_EXPERTS)
    padded = (counts + MOE_BLOCK - 1) // MOE_BLOCK * MOE_BLOCK
    pad_end = jnp.cumsum(padded)
    pad_start = pad_end - padded
    start = jnp.cumsum(counts) - counts
    dest = pad_start[se] + jnp.arange(A) - start[se]
    P = (-(-A // MOE_BLOCK) + N_EXPERTS) * MOE_BLOCK
    NB = P // MOE_BLOCK
    slot_tok = jnp.full((P,), T, dtype=jnp.int32).at[dest].set(flat_t[order])
    slot_w = jnp.zeros((P,), f32).at[dest].set(flat_w[order])
    blk_exp = jnp.minimum(jnp.searchsorted(pad_end, jnp.arange(NB) * MOE_BLOCK, side='right'),
                          N_EXPERTS - 1)
    xpad = jnp.concatenate([x, jnp.zeros((1, D), x.dtype)], axis=0)
    xs = xpad[slot_tok].reshape(NB, MOE_BLOCK, D)

    def expert_block(args):
        xb, e = args
        hb = jax.nn.silu(xb @ w_gate[e]) * (xb @ w_up[e])
        return hb @ w_down[e]

    ys = lax.map(expert_block, (xs, blk_exp)).reshape(P, D)
    ys = ys * slot_w[:, None].astype(ys.dtype)
    out = jnp.zeros((T + 1, D), ys.dtype).at[slot_tok].add(ys)[:T]
    return out.reshape(Bsz, S, D)


def setup_inputs(seed: int = 0) -> dict:
    key = jax.random.key(seed)
    ks = jax.random.split(key, 16)
    f32 = jnp.float32

    def nrm(k, shape, scale):
        return jax.random.normal(k, shape, f32) * scale

    return {
        'x': nrm(ks[0], (BATCH, SEQ, D_MODEL), 1.0),
        'w_in': nrm(ks[1], (DEPTH, D_MODEL, D_IN), D_MODEL ** -0.5),
        'w_out': nrm(ks[2], (DEPTH, D_MIX, D_MODEL), D_MIX ** -0.5),
        'norm_mix': 1.0 + nrm(ks[3], (DEPTH, D_MODEL), 0.02),
        'norm_ffn': 1.0 + nrm(ks[4], (DEPTH, D_MODEL), 0.02),
        'norm_final': 1.0 + nrm(ks[5], (D_MODEL,), 0.02),
        'hgrn_lb': nrm(ks[6], (DEPTH, A_HEADS * A_DK), 1.0),
        'hgrn_norm': 1.0 + nrm(ks[7], (DEPTH, A_WIDTH), 0.02),
        'ret_norm': 1.0 + nrm(ks[8], (DEPTH, B_WIDTH), 0.02),
        'rel_bias': nrm(ks[9], (C_HEADS, 2 * MAX_REL + 1), 0.2),
        'router_group': nrm(ks[10], (DEPTH, D_MODEL, N_GROUPS), D_MODEL ** -0.5),
        'router_expert': nrm(ks[11], (DEPTH, D_MODEL, N_EXPERTS), D_MODEL ** -0.5),
        'expert_w_gate': nrm(ks[12], (DEPTH, N_EXPERTS, D_MODEL, D_EXPERT), D_MODEL ** -0.5),
        'expert_w_up': nrm(ks[13], (DEPTH, N_EXPERTS, D_MODEL, D_EXPERT), D_MODEL ** -0.5),
        'expert_w_down': nrm(ks[14], (DEPTH, N_EXPERTS, D_EXPERT, D_MODEL), D_EXPERT ** -0.5),
    }


def reference(x, w_in, w_out, norm_mix, norm_ffn, norm_final, hgrn_lb, hgrn_norm, ret_norm,
              rel_bias, router_group, router_expert, expert_w_gate, expert_w_up, expert_w_down):
    f32 = jnp.float32
    Bsz, S = x.shape[0], x.shape[1]
    p = jax.nn.softmax(hgrn_lb.astype(f32), axis=0)
    lower_bounds = jnp.clip(jnp.cumsum(p, axis=0) - p[0], 0.0, 1.0)
    split_at = np.cumsum(IN_SIZES)[:-1].tolist()

    def heads(t, n):
        return t.reshape(Bsz, S, n, -1)

    for l in range(DEPTH):
        h = rmsnorm(x, norm_mix[l])
        z = h @ w_in[l]
        aq, af, ai, ag, bq, bk, bv, bg, cq, ck, cv = jnp.split(z, split_at, axis=-1)
        ya = hgrn2_mixer(heads(aq, A_HEADS), heads(af, A_HEADS), heads(ai, A_HEADS), ag,
                         lower_bounds[l], hgrn_norm[l])
        yb = retention_mixer(heads(bq, B_HEADS), heads(bk, B_HEADS), heads(bv, B_HEADS), bg, ret_norm[l])
        yc = chunk_attention_mixer(heads(cq, C_HEADS), heads(ck, C_HEADS), heads(cv, C_HEADS), rel_bias)
        x = x + jnp.concatenate([ya, yb, yc], axis=-1) @ w_out[l]
        h = rmsnorm(x, norm_ffn[l])
        x = x + hier_moe(h, router_group[l], router_expert[l], expert_w_gate[l],
                         expert_w_up[l], expert_w_down[l])
    return rmsnorm(x, norm_final)
```

```python
import functools

import numpy as np
import jax
import jax.numpy as jnp
from jax import lax
from jax.experimental import pallas as pl
from jax.experimental.pallas import tpu as pltpu

F32 = jnp.float32
BF16 = jnp.bfloat16

D_MODEL = 1024
CHUNK = 64
NORM_EPS = 1e-6
MASK_VALUE = -1e30
MIN_FORGET = 1e-30

A_HEADS, A_DK, A_DV = 4, 64, 64
A_WIDTH = A_HEADS * A_DV
B_HEADS, B_DK, B_DV = 4, 48, 96
B_HALF = B_DK // 2
B_QSLOT = 32
B_VSLOT = 128
B_QW = 2 * B_HEADS * B_QSLOT
B_VW = B_HEADS * B_VSLOT
ROPE_BASE = 10000.0
C_HEADS, C_DH = 6, 64
C_WIDTH = C_HEADS * C_DH
N_PREV_CHUNKS = 8
MAX_REL = 128
C_WIN = (N_PREV_CHUNKS + 1) * CHUNK

ZA_W = 4 * A_WIDTH
ZB_W = 2 * B_QW + 2 * B_VW
ZC_W = 3 * C_WIDTH
Y_W = A_WIDTH + B_VW + C_WIDTH

N_GROUPS, EXPERTS_PER_GROUP = 4, 8
N_EXPERTS = N_GROUPS * EXPERTS_PER_GROUP
D_EXPERT = 512
ROUTE_W = 128

SUB = 16
VMEM_LIMIT = 56 * 1024 * 1024

TM = 512
TS_A = 256
TS_B = 256
TQ_C = 512
MOE_BM = 256
TM_COMBINE = 256


def _cparams(*sem):
    return pltpu.CompilerParams(dimension_semantics=sem, vmem_limit_bytes=VMEM_LIMIT)


def _sigmoid(x):
    return 1.0 / (1.0 + jnp.exp(-x))


def _silu(x):
    return x * _sigmoid(x)


def _split3(x):
    hi = x.astype(BF16)
    r1 = x - hi.astype(F32)
    mid = r1.astype(BF16)
    lo = (r1 - mid.astype(F32)).astype(BF16)
    return hi, mid, lo


def _dot(a, b):
    return jnp.dot(a, b, preferred_element_type=F32)


def _dot_nt(a, b):
    return lax.dot_general(a, b, (((1,), (1,)), ((), ())), preferred_element_type=F32)


def _dot_tn(a, b):
    return lax.dot_general(a, b, (((0,), (0,)), ((), ())), preferred_element_type=F32)


def _rms(x, w):
    ms = jnp.mean(x * x, axis=-1, keepdims=True)
    return x * lax.rsqrt(ms + NORM_EPS) * w


def _project(h, w_ref, za_ref, zb_ref, zc_ref):
    hb = h.astype(BF16)
    za_ref[...] = _dot(hb, w_ref[:, 0:ZA_W]).astype(BF16)
    zb_ref[...] = _dot(hb, w_ref[:, ZA_W:ZA_W + ZB_W]).astype(BF16)
    zc_ref[...] = _dot(hb, w_ref[:, ZA_W + ZB_W:]).astype(BF16)


def _inproj_kernel(x_ref, nw_ref, w_ref, za_ref, zb_ref, zc_ref):
    _project(_rms(x_ref[...], nw_ref[...]), w_ref, za_ref, zb_ref, zc_ref)


def _inproj(x2d, nw, w):
    T = x2d.shape[0]
    row = lambda i: (i, 0)
    fixed = lambda i: (0, 0)
    return pl.pallas_call(
        _inproj_kernel,
        grid=(T // TM,),
        in_specs=[pl.BlockSpec((TM, D_MODEL), row),
                  pl.BlockSpec((1, D_MODEL), fixed),
                  pl.BlockSpec((D_MODEL, ZA_W + ZB_W + ZC_W), fixed)],
        out_specs=[pl.BlockSpec((TM, ZA_W), row),
                   pl.BlockSpec((TM, ZB_W), row),
                   pl.BlockSpec((TM, ZC_W), row)],
        out_shape=[jax.ShapeDtypeStruct((T, ZA_W), BF16),
                   jax.ShapeDtypeStruct((T, ZB_W), BF16),
                   jax.ShapeDtypeStruct((T, ZC_W), BF16)],
        compiler_params=_cparams("parallel"),
        name="inproj",
    )(x2d, nw, w)


def _hgrn_kernel(z_ref, lb_ref, nw_ref, tri_ref, eones_ref, bd_ref, o_ref, st_ref):
    @pl.when(pl.program_id(1) == 0)
    def _():
        st_ref[...] = jnp.zeros_like(st_ref)

    W = A_WIDTH
    lb = lb_ref[...]
    lane_head = lax.broadcasted_iota(jnp.int32, (1, W), 1) // A_DK
    head_masks = [lane_head == h for h in range(A_HEADS)]
    tri = tri_ref[...]
    eones = eones_ref[...]
    bd = bd_ref[...]
    ii = lax.broadcasted_iota(jnp.int32, (SUB, SUB, 1), 0)
    jj = lax.broadcasted_iota(jnp.int32, (SUB, SUB, 1), 1)
    causal = jj <= ii
    col = lax.broadcasted_iota(jnp.int32, (1, CHUNK), 1)

    n_chunks = z_ref.shape[0] // CHUNK
    n_sub = CHUNK // SUB
    outs = []
    for c in range(n_chunks):
        r0 = c * CHUNK
        zq = z_ref[r0:r0 + CHUNK, 0:W].astype(F32)
        zf = z_ref[r0:r0 + CHUNK, W:2 * W].astype(F32)
        vv = z_ref[r0:r0 + CHUNK, 2 * W:3 * W]
        vf = vv.astype(F32)
        f = lb + (1.0 - lb) * _sigmoid(zf)
        lf = jnp.log(jnp.maximum(f, MIN_FORGET))
        kk = (1.0 - lb) * _sigmoid(-zf)
        qf = _silu(zq) * (A_DK ** -0.5)
        hi, mid, lo = _split3(lf)
        G = _dot(tri, hi) + _dot(tri, mid) + _dot(tri, lo)
        g_last = G[CHUNK - 1:CHUNK, :]

        st = st_ref[...]
        inter = _dot_nt((qf * jnp.exp(G)).astype(BF16), st.astype(BF16))
        khat = kk * jnp.exp(g_last - G)
        ut = _dot_tn(vv, khat.astype(BF16))
        st_ref[...] = jnp.exp(g_last) * st + ut * bd

        p_rows = []
        for s in range(1, n_sub):
            b = G[s * SUB - 1:s * SUB, :]
            qt = qf[s * SUB:(s + 1) * SUB, :] * jnp.exp(G[s * SUB:(s + 1) * SUB, :] - b)
            kt = kk * jnp.exp(jnp.minimum(b - G, 0.0))
            lhs = jnp.concatenate([jnp.where(m, qt, 0.0) for m in head_masks], axis=0)
            p = _dot_nt(lhs.astype(BF16), kt.astype(BF16))
            p_rows.append(jnp.where(col < s * SUB, p, 0.0))
        pall = jnp.concatenate(p_rows, axis=0)
        pv = _dot(pall.astype(BF16), vv)

        pieces = []
        for s in range(n_sub):
            sl = slice(s * SUB, (s + 1) * SUB)
            gb, qb, kb, vb = G[sl], qf[sl], kk[sl], vf[sl]
            dec = jnp.exp(jnp.minimum(gb[:, None, :] - gb[None, :, :], 0.0))
            a = (qb[:, None, :] * kb[None, :, :]) * dec
            r = _dot(a.reshape(SUB * SUB, W).astype(BF16), eones).reshape(SUB, SUB, W)
            diag = jnp.sum(jnp.where(causal, r, 0.0) * vb[None, :, :], axis=1)
            piece = inter[sl] + diag
            if s > 0:
                base = (s - 1) * CHUNK
                for h in range(A_HEADS):
                    blk = pv[base + h * SUB:base + (h + 1) * SUB, :]
                    piece = piece + jnp.where(head_masks[h], blk, 0.0)
            pieces.append(piece)
        outs.append(jnp.concatenate(pieces, axis=0))

    o = jnp.concatenate(outs, axis=0)
    g = z_ref[:, 3 * W:4 * W].astype(F32)
    o_ref[...] = (_rms(o, nw_ref[...]) * _silu(g)).astype(BF16)


def _hgrn(za, lb, nw, tri, eones, bd, B, S):
    nt = S // TS_A
    fixed = lambda b, t: (0, 0)
    return pl.pallas_call(
        _hgrn_kernel,
        grid=(B, nt),
        in_specs=[pl.BlockSpec((TS_A, ZA_W), lambda b, t: (b * nt + t, 0)),
                  pl.BlockSpec((1, A_WIDTH), fixed),
                  pl.BlockSpec((1, A_WIDTH), fixed),
                  pl.BlockSpec((CHUNK, CHUNK), fixed),
                  pl.BlockSpec((A_WIDTH, A_WIDTH), fixed),
                  pl.BlockSpec((A_WIDTH, A_WIDTH), fixed)],
        out_specs=pl.BlockSpec((TS_A, A_WIDTH), lambda b, t: (b * nt + t, 0)),
        out_shape=jax.ShapeDtypeStruct((B * S, A_WIDTH), BF16),
        scratch_shapes=[pltpu.VMEM((A_WIDTH, A_WIDTH), F32)],
        compiler_params=_cparams("parallel", "arbitrary"),
        name="hgrn2",
    )(za, lb, nw, tri, eones, bd)


def _ret_kernel(z_ref, cos_ref, sin_ref, dstack_ref, qdec_ref, kdec_ref, cg_ref, bd_ref, nw_ref,
                o_ref, st_ref):
    @pl.when(pl.program_id(1) == 0)
    def _():
        st_ref[...] = jnp.zeros_like(st_ref)

    H = B_QW // 2
    lane_head = (lax.broadcasted_iota(jnp.int32, (1, B_QW), 1) % H) // B_QSLOT
    head_masks = [lane_head == h for h in range(B_HEADS)]
    lane_real = lax.broadcasted_iota(jnp.int32, (1, B_VSLOT), 1) < B_DV
    dstack = dstack_ref[...]
    qdec, kdec = qdec_ref[...], kdec_ref[...]
    cg, bd = cg_ref[...], bd_ref[...]

    n_chunks = z_ref.shape[0] // CHUNK
    outs = []
    for c in range(n_chunks):
        rows = slice(c * CHUNK, (c + 1) * CHUNK)
        cos, sin = cos_ref[rows, :], sin_ref[rows, :]

        def rot(off):
            t1 = z_ref[rows, off:off + H].astype(F32)
            t2 = z_ref[rows, off + H:off + 2 * H].astype(F32)
            return jnp.concatenate([t1 * cos - t2 * sin, t1 * sin + t2 * cos], axis=1)

        qr = rot(0)
        kr = rot(B_QW) * (B_DK ** -0.5)
        vv = z_ref[rows, 2 * B_QW:2 * B_QW + B_VW]

        lhs = jnp.concatenate([jnp.where(m, qr, 0.0) for m in head_masks], axis=0)
        sc = _dot_nt(lhs.astype(BF16), kr.astype(BF16)) * dstack
        pv = _dot(sc.astype(BF16), vv)
        intra = jnp.concatenate(
            [pv[h * CHUNK:(h + 1) * CHUNK, h * B_VSLOT:(h + 1) * B_VSLOT] for h in range(B_HEADS)],
            axis=1)
        st = st_ref[...]
        inter = _dot_nt((qr * qdec).astype(BF16), st.astype(BF16))
        ut = _dot_tn(vv, (kr * kdec).astype(BF16))
        st_ref[...] = cg * st + ut * bd
        outs.append(intra + inter)

    o = jnp.concatenate(outs, axis=0)
    normed = []
    for h in range(B_HEADS):
        oh = o[:, h * B_VSLOT:(h + 1) * B_VSLOT]
        mu = jnp.sum(oh, axis=-1, keepdims=True) * (1.0 / B_DV)
        d = jnp.where(lane_real, oh - mu, 0.0)
        var = jnp.sum(d * d, axis=-1, keepdims=True) * (1.0 / B_DV)
        normed.append(d * lax.rsqrt(var + NORM_EPS))
    y = jnp.concatenate(normed, axis=1)
    g = z_ref[:, 2 * B_QW + B_VW:].astype(F32)
    o_ref[...] = (y * nw_ref[...] * _silu(g)).astype(BF16)


def _retention(zb, cos, sin, consts, nw, B, S):
    nt = S // TS_B
    fixed = lambda b, t: (0, 0)
    dstack, qdec, kdec, cg, bd = consts
    return pl.pallas_call(
        _ret_kernel,
        grid=(B, nt),
        in_specs=[pl.BlockSpec((TS_B, ZB_W), lambda b, t: (b * nt + t, 0)),
                  pl.BlockSpec((TS_B, B_QW // 2), lambda b, t: (t, 0)),
                  pl.BlockSpec((TS_B, B_QW // 2), lambda b, t: (t, 0)),
                  pl.BlockSpec(dstack.shape, fixed),
                  pl.BlockSpec(qdec.shape, fixed),
                  pl.BlockSpec(kdec.shape, fixed),
                  pl.BlockSpec(cg.shape, fixed),
                  pl.BlockSpec(bd.shape, fixed),
                  pl.BlockSpec((1, B_VW), fixed)],
        out_specs=pl.BlockSpec((TS_B, B_VW), lambda b, t: (b * nt + t, 0)),
        out_shape=jax.ShapeDtypeStruct((B * S, B_VW), BF16),
        scratch_shapes=[pltpu.VMEM((B_VW, B_QW), F32)],
        compiler_params=_cparams("parallel", "arbitrary"),
        name="retention",
    )(zb, cos, sin, dstack, qdec, kdec, cg, bd, nw)


def _attn_kernel(q_ref, kp_ref, kc_ref, vp_ref, vc_ref, bias_ref, o_ref, kwin_ref, vwin_ref):
    t = pl.program_id(1)
    kwin_ref[0:TQ_C, :] = kp_ref[...]
    kwin_ref[TQ_C:, :] = kc_ref[...]
    vwin_ref[0:TQ_C, :] = vp_ref[...]
    vwin_ref[TQ_C:, :] = vc_ref[...]
    jcol = lax.broadcasted_iota(jnp.int32, (1, C_WIN), 1)

    def chunk(c, carry):
        r0 = pl.multiple_of(c * CHUNK, CHUNK)
        valid = (jcol + (r0 + (t - 1) * TQ_C)) >= 0
        for h in range(C_HEADS):
            lanes = slice(h * C_DH, (h + 1) * C_DH)
            qh = q_ref[pl.ds(r0, CHUNK), lanes] * (C_DH ** -0.5)
            kh = kwin_ref[pl.ds(r0, C_WIN), lanes]
            vh = vwin_ref[pl.ds(r0, C_WIN), lanes]
            s = _dot_nt(qh, kh) + bias_ref[h]
            s = jnp.where(valid, s, MASK_VALUE)
            m = jnp.max(s, axis=-1, keepdims=True)
            p = jnp.exp(s - m)
            l = jnp.sum(p, axis=-1, keepdims=True)
            o = _dot(p.astype(BF16), vh) / l
            o_ref[pl.ds(r0, CHUNK), lanes] = o.astype(BF16)
        return carry

    lax.fori_loop(0, TQ_C // CHUNK, chunk, 0)


def _attention(zc, bias, B, S):
    nt = S // TQ_C
    cur = lambda col: (lambda b, t: (b * nt + t, col))
    prev = lambda col: (lambda b, t: (b * nt + jnp.maximum(t - 1, 0), col))
    blk = (TQ_C, C_WIDTH)
    return pl.pallas_call(
        _attn_kernel,
        grid=(B, nt),
        in_specs=[pl.BlockSpec(blk, cur(0)),
                  pl.BlockSpec(blk, prev(1)), pl.BlockSpec(blk, cur(1)),
                  pl.BlockSpec(blk, prev(2)), pl.BlockSpec(blk, cur(2)),
                  pl.BlockSpec(bias.shape, lambda b, t: (0, 0, 0))],
        out_specs=pl.BlockSpec(blk, cur(0)),
        out_shape=jax.ShapeDtypeStruct((B * S, C_WIDTH), BF16),
        scratch_shapes=[pltpu.VMEM((2 * TQ_C, C_WIDTH), BF16),
                        pltpu.VMEM((2 * TQ_C, C_WIDTH), BF16)],
        compiler_params=_cparams("parallel", "arbitrary"),
        name="chunk_attention",
    )(zc, zc, zc, zc, zc, bias)


def _route(logits):
    lane = lax.broadcasted_iota(jnp.int32, logits.shape, 1).astype(F32)
    big = float(1 << 20)
    neg = -jnp.inf

    def first_argmax(vals):
        m = jnp.max(vals, axis=-1, keepdims=True)
        idx = jnp.min(jnp.where(vals == m, lane, big), axis=-1, keepdims=True)
        return m, idx

    gl = jnp.where(lane < N_GROUPS, logits, neg)
    gm, grp = first_argmax(gl)
    p_grp = 1.0 / jnp.sum(jnp.exp(gl - gm), axis=-1, keepdims=True)
    lo = N_GROUPS + grp * EXPERTS_PER_GROUP
    el = jnp.where((lane >= lo) & (lane < lo + EXPERTS_PER_GROUP), logits, neg)
    v1, i1 = first_argmax(el)
    v2, i2 = first_argmax(jnp.where(lane == i1, neg, el))
    e2 = jnp.exp(v2 - v1)
    g1 = p_grp / (1.0 + e2)
    g2 = p_grp * e2 / (1.0 + e2)
    out = jnp.where(lane == 0, i1 - N_GROUPS, 0.0)
    out = jnp.where(lane == 1, i2 - N_GROUPS, out)
    out = jnp.where(lane == 2, g1, out)
    out = jnp.where(lane == 3, g2, out)
    return out


def _outproj_kernel(x_ref, ya_ref, yb_ref, yc_ref, w_ref, nw_ref, wr_ref, xo_ref, h_ref, r_ref):
    a0, a1 = A_WIDTH, A_WIDTH + B_VW
    x = x_ref[...]
    x = x + _dot(ya_ref[...], w_ref[0:a0, :])
    x = x + _dot(yb_ref[...], w_ref[a0:a1, :])
    x = x + _dot(yc_ref[...], w_ref[a1:, :])
    xo_ref[...] = x
    h = _rms(x, nw_ref[...])
    h_ref[...] = h
    hs = _split3(h)
    ws = (wr_ref[0], wr_ref[1], wr_ref[2])
    logits = None
    for i in range(3):
        for j in range(3 - i):
            t = _dot(hs[i], ws[j])
            logits = t if logits is None else logits + t
    r_ref[...] = _route(logits)


def _outproj(x2d, ya, yb, yc, w, nw, wr3):
    T = x2d.shape[0]
    row = lambda i: (i, 0)
    fixed = lambda i: (0, 0)
    return pl.pallas_call(
        _outproj_kernel,
        grid=(T // TM,),
        in_specs=[pl.BlockSpec((TM, D_MODEL), row),
                  pl.BlockSpec((TM, A_WIDTH), row),
                  pl.BlockSpec((TM, B_VW), row),
                  pl.BlockSpec((TM, C_WIDTH), row),
                  pl.BlockSpec((Y_W, D_MODEL), fixed),
                  pl.BlockSpec((1, D_MODEL), fixed),
                  pl.BlockSpec((3, D_MODEL, ROUTE_W), lambda i: (0, 0, 0))],
        out_specs=[pl.BlockSpec((TM, D_MODEL), row),
                   pl.BlockSpec((TM, D_MODEL), row),
                   pl.BlockSpec((TM, ROUTE_W), row)],
        out_shape=[jax.ShapeDtypeStruct((T, D_MODEL), F32),
                   jax.ShapeDtypeStruct((T, D_MODEL), F32),
                   jax.ShapeDtypeStruct((T, ROUTE_W), F32)],
        compiler_params=_cparams("parallel"),
        name="outproj_router",
    )(x2d, ya, yb, yc, w, nw, wr3)


def _row_gather_start(src_hbm, idx_ref, dst_ref, sem, n_rows):
    def body(r, carry):
        pltpu.make_async_copy(src_hbm.at[pl.ds(idx_ref[r], 1)], dst_ref.at[pl.ds(r, 1)], sem).start()
        return carry
    lax.fori_loop(0, n_rows, body, 0)


def _row_gather_wait(src_hbm, dst_ref, sem, n_rows):
    def body(r, carry):
        pltpu.make_async_copy(src_hbm.at[pl.ds(0, 1)], dst_ref.at[pl.ds(r, 1)], sem).wait()
        return carry
    lax.fori_loop(0, n_rows, body, 0)


def _expert_kernel(blk_exp_ref, nused_ref, tok_ref, h_hbm, wgu_ref, wd_ref, y_ref, xbuf, sem):
    i = pl.program_id(0)
    used = i < nused_ref[0]

    @pl.when(used)
    def _():
        _row_gather_start(h_hbm, tok_ref.at[0, 0], xbuf, sem, MOE_BM)
        _row_gather_wait(h_hbm, xbuf, sem, MOE_BM)
        xb = xbuf[...].astype(BF16)
        gu = _dot(xb, wgu_ref[0])
        act = _silu(gu[:, :D_EXPERT]) * gu[:, D_EXPERT:]
        y_ref[...] = _dot(act.astype(BF16), wd_ref[0])

    @pl.when(jnp.logical_not(used))
    def _():
        y_ref[...] = jnp.zeros_like(y_ref)


def _experts(h, slot_tok, blk_exp, nused, wgu, wd):
    nblk = blk_exp.shape[0]
    grid_spec = pltpu.PrefetchScalarGridSpec(
        num_scalar_prefetch=2,
        grid=(nblk,),
        in_specs=[pl.BlockSpec((1, 1, MOE_BM), lambda i, be, nu: (i, 0, 0), memory_space=pltpu.SMEM),
                  pl.BlockSpec(memory_space=pl.ANY),
                  pl.BlockSpec((1, D_MODEL, 2 * D_EXPERT), lambda i, be, nu: (be[i], 0, 0)),
                  pl.BlockSpec((1, D_EXPERT, D_MODEL), lambda i, be, nu: (be[i], 0, 0))],
        out_specs=pl.BlockSpec((MOE_BM, D_MODEL), lambda i, be, nu: (i, 0)),
        scratch_shapes=[pltpu.VMEM((MOE_BM, D_MODEL), F32), pltpu.SemaphoreType.DMA(())],
    )
    return pl.pallas_call(
        _expert_kernel,
        grid_spec=grid_spec,
        out_shape=jax.ShapeDtypeStruct((nblk * MOE_BM, D_MODEL), F32),
        compiler_params=_cparams("arbitrary"),
        name="experts",
    )(blk_exp, nused, slot_tok.reshape(nblk, 1, MOE_BM), h, wgu, wd)


def _combine_rows(pos_ref, x_ref, r_ref, y_hbm, ybuf, sem):
    n = 2 * TM_COMBINE
    _row_gather_start(y_hbm, pos_ref.at[0, 0], ybuf, sem, n)
    _row_gather_wait(y_hbm, ybuf, sem, n)
    r = r_ref[...]
    g0, g1 = r[:, 2:3], r[:, 3:4]
    y0 = ybuf[0:TM_COMBINE, :].astype(F32)
    y1 = ybuf[TM_COMBINE:, :].astype(F32)
    return x_ref[...] + (g0 * y0 + g1 * y1)


def _combine_inproj_kernel(pos_ref, x_ref, r_ref, y_hbm, nw_ref, w_ref,
                           xo_ref, za_ref, zb_ref, zc_ref, ybuf, sem):
    x = _combine_rows(pos_ref, x_ref, r_ref, y_hbm, ybuf, sem)
    xo_ref[...] = x
    _project(_rms(x, nw_ref[...]), w_ref, za_ref, zb_ref, zc_ref)


def _combine_final_kernel(pos_ref, x_ref, r_ref, y_hbm, nw_ref, o_ref, ybuf, sem):
    x = _combine_rows(pos_ref, x_ref, r_ref, y_hbm, ybuf, sem)
    o_ref[...] = _rms(x, nw_ref[...])


def _combine(x2d, route, ys, pos, nw, w=None):
    T = x2d.shape[0]
    tm = TM_COMBINE
    row = lambda i: (i, 0)
    fixed = lambda i: (0, 0)
    in_specs = [pl.BlockSpec((1, 1, 2 * tm), lambda i: (i, 0, 0), memory_space=pltpu.SMEM),
                pl.BlockSpec((tm, D_MODEL), row),
                pl.BlockSpec((tm, ROUTE_W), row),
                pl.BlockSpec(memory_space=pl.ANY),
                pl.BlockSpec((1, D_MODEL), fixed)]
    scratch = [pltpu.VMEM((2 * tm, D_MODEL), F32), pltpu.SemaphoreType.DMA(())]
    if w is None:
        return pl.pallas_call(
            _combine_final_kernel,
            grid=(T // tm,),
            in_specs=in_specs,
            out_specs=pl.BlockSpec((tm, D_MODEL), row),
            out_shape=jax.ShapeDtypeStruct((T, D_MODEL), F32),
            scratch_shapes=scratch,
            compiler_params=_cparams("arbitrary"),
            name="combine_final",
        )(pos, x2d, route, ys, nw)
    return pl.pallas_call(
        _combine_inproj_kernel,
        grid=(T // tm,),
        in_specs=in_specs + [pl.BlockSpec((D_MODEL, ZA_W + ZB_W + ZC_W), fixed)],
        out_specs=[pl.BlockSpec((tm, D_MODEL), row),
                   pl.BlockSpec((tm, ZA_W), row),
                   pl.BlockSpec((tm, ZB_W), row),
                   pl.BlockSpec((tm, ZC_W), row)],
        out_shape=[jax.ShapeDtypeStruct((T, D_MODEL), F32),
                   jax.ShapeDtypeStruct((T, ZA_W), BF16),
                   jax.ShapeDtypeStruct((T, ZB_W), BF16),
                   jax.ShapeDtypeStruct((T, ZC_W), BF16)],
        scratch_shapes=scratch,
        compiler_params=_cparams("arbitrary"),
        name="combine_inproj",
    )(pos, x2d, route, ys, nw, w)


def _in_column_map():
    a_cols = np.arange(ZA_W)
    off = ZA_W
    qk = np.full((2, 2, B_HEADS, B_QSLOT), -1, np.int64)
    for which in range(2):
        base = off + which * B_HEADS * B_DK
        for h in range(B_HEADS):
            for half in range(2):
                qk[which, half, h, :B_HALF] = base + h * B_DK + half * B_HALF + np.arange(B_HALF)
    vg = np.full((2, B_HEADS, B_VSLOT), -1, np.int64)
    for which in range(2):
        base = off + 2 * B_HEADS * B_DK + which * B_HEADS * B_DV
        for h in range(B_HEADS):
            vg[which, h, :B_DV] = base + h * B_DV + np.arange(B_DV)
    c0 = off + 2 * B_HEADS * B_DK + 2 * B_HEADS * B_DV
    c_cols = c0 + np.arange(ZC_W)
    return np.concatenate([a_cols, qk.reshape(-1), vg.reshape(-1), c_cols])


def _take_or_zero(arr, idx, axis):
    safe = np.maximum(idx, 0)
    taken = jnp.take(arr, safe, axis=axis)
    shape = [1] * arr.ndim
    shape[axis] = idx.shape[0]
    return jnp.where(jnp.asarray(idx >= 0).reshape(shape), taken, 0.0)


def _out_row_map():
    b = np.full((B_HEADS, B_VSLOT), -1, np.int64)
    for h in range(B_HEADS):
        b[h, :B_DV] = A_WIDTH + h * B_DV + np.arange(B_DV)
    c0 = A_WIDTH + B_HEADS * B_DV
    return np.concatenate([np.arange(A_WIDTH), b.reshape(-1), c0 + np.arange(C_WIDTH)])


def _ret_norm_map():
    b = np.full((B_HEADS, B_VSLOT), -1, np.int64)
    for h in range(B_HEADS):
        b[h, :B_DV] = h * B_DV + np.arange(B_DV)
    return b.reshape(-1)


def _retention_tables(S):
    log_gamma = jnp.log1p(-jnp.exp2(-5.0 - jnp.arange(B_HEADS, dtype=F32)))
    idx = jnp.arange(CHUNK, dtype=F32)
    rel = idx[:, None] - idx[None, :]
    decay = jnp.where(rel >= 0, jnp.exp(log_gamma[:, None, None] * jnp.maximum(rel, 0.0)), 0.0)
    dstack = decay.reshape(B_HEADS * CHUNK, CHUNK)
    k_decay = jnp.exp(log_gamma[:, None] * (CHUNK - 1.0 - idx)[None, :])
    q_decay = jnp.exp(log_gamma[:, None] * (idx + 1.0)[None, :])
    chunk_gamma = jnp.exp(log_gamma * CHUNK)

    def lanes(per_head):
        t = jnp.repeat(per_head[..., None], B_QSLOT, axis=-1)
        t = t.reshape(per_head.shape[:-1] + (B_HEADS * B_QSLOT,))
        return jnp.concatenate([t, t], axis=-1)

    qdec = lanes(q_decay.T)
    kdec = lanes(k_decay.T)
    cg = lanes(chunk_gamma[None, :])
    row_head = np.arange(B_VW) // B_VSLOT
    lane_head = (np.arange(B_QW) % (B_QW // 2)) // B_QSLOT
    bd = jnp.asarray((row_head[:, None] == lane_head[None, :]).astype(np.float32))

    half = B_HALF
    inv_freq = ROPE_BASE ** (-jnp.arange(half, dtype=F32) / half)
    ang = jnp.arange(S).astype(F32)[:, None] * inv_freq[None, :]
    pad = ((0, 0), (0, B_QSLOT - half))
    cos = jnp.tile(jnp.pad(jnp.cos(ang), pad), (1, B_HEADS))
    sin = jnp.tile(jnp.pad(jnp.sin(ang), pad), (1, B_HEADS))
    return cos, sin, (dstack, qdec, kdec, cg, bd)


def _attention_bias(rel_bias):
    qi = np.arange(CHUNK)[:, None]
    km = np.arange(C_WIN)[None, :]
    dist = N_PREV_CHUNKS * CHUNK + qi - km
    return rel_bias.astype(F32)[:, np.clip(dist, -MAX_REL, MAX_REL) + MAX_REL]


def _hgrn_tables():
    tri = jnp.asarray(np.tril(np.ones((CHUNK, CHUNK), np.float32))).astype(BF16)
    head = np.arange(A_WIDTH) // A_DK
    same = (head[:, None] == head[None, :]).astype(np.float32)
    return tri, jnp.asarray(same).astype(BF16), jnp.asarray(same)


def _moe_plan(route, T):
    A = 2 * T
    nblk = A // MOE_BM + N_EXPERTS
    eid = route[:, 0:2].astype(jnp.int32)
    flat_e = eid.reshape(A)
    onehot = (flat_e[:, None] == jnp.arange(N_EXPERTS, dtype=jnp.int32)[None, :]).astype(jnp.int32)
    csum = jnp.cumsum(onehot, axis=0)
    counts = csum[-1]
    rank = jnp.sum((csum - onehot) * onehot, axis=1)
    padded = (counts + MOE_BM - 1) // MOE_BM * MOE_BM
    pad_end = jnp.cumsum(padded)
    pad_start = pad_end - padded
    dest = pad_start[flat_e] + rank
    flat_t = jnp.arange(A, dtype=jnp.int32) // 2
    slot_tok = jnp.zeros((nblk * MOE_BM,), jnp.int32).at[dest].set(flat_t)
    nused = (pad_end[-1] // MOE_BM).astype(jnp.int32)
    blk_start = jnp.arange(nblk, dtype=jnp.int32) * MOE_BM
    blk_exp = jnp.sum((pad_end[None, :] <= blk_start[:, None]).astype(jnp.int32), axis=1)
    last = jnp.take(blk_exp, jnp.maximum(nused - 1, 0))
    blk_exp = jnp.where(jnp.arange(nblk) < nused, jnp.minimum(blk_exp, N_EXPERTS - 1), last)
    tm = TM_COMBINE
    pos = dest.reshape(T // tm, tm, 2).transpose(0, 2, 1).reshape(T // tm, 1, 2 * tm)
    return slot_tok, blk_exp, nused.reshape(1), pos


def kernel(x, w_in, w_out, norm_mix, norm_ffn, norm_final, hgrn_lb, hgrn_norm, ret_norm, rel_bias,
           router_group, router_expert, expert_w_gate, expert_w_up, expert_w_down):
    B, S, D = x.shape
    T = B * S
    depth = w_in.shape[0]

    p = jax.nn.softmax(hgrn_lb.astype(F32), axis=0)
    lower_bounds = jnp.clip(jnp.cumsum(p, axis=0) - p[0], 0.0, 1.0)

    in_map, out_map, rn_map = _in_column_map(), _out_row_map(), _ret_norm_map()
    cos, sin, ret_consts = _retention_tables(S)
    bias = _attention_bias(rel_bias)
    tri, eones, bd_a = _hgrn_tables()

    x2d = x.reshape(T, D).astype(F32)
    route = ys = pos = None
    out = None
    for l in range(depth):
        w_in_l = _take_or_zero(w_in[l], in_map, 1).astype(BF16)
        w_out_l = _take_or_zero(w_out[l], out_map, 0).astype(BF16)
        ret_nw = _take_or_zero(ret_norm[l].astype(F32), rn_map, 0).reshape(1, B_VW)
        nw_mix = norm_mix[l].astype(F32).reshape(1, D)
        if l == 0:
            za, zb, zc = _inproj(x2d, nw_mix, w_in_l)
        else:
            x2d, za, zb, zc = _combine(x2d, route, ys, pos, nw_mix, w_in_l)

        ya = _hgrn(za, lower_bounds[l].reshape(1, A_WIDTH), hgrn_norm[l].astype(F32).reshape(1, A_WIDTH),
                   tri, eones, bd_a, B, S)
        yb = _retention(zb, cos, sin, ret_consts, ret_nw, B, S)
        yc = _attention(zc, bias, B, S)

        wr = jnp.concatenate([router_group[l], router_expert[l]], axis=1).astype(F32)
        wr = jnp.pad(wr, ((0, 0), (0, ROUTE_W - wr.shape[1])))
        wr3 = jnp.stack(_split3(wr))
        x2d, h, route = _outproj(x2d, ya, yb, yc, w_out_l, norm_ffn[l].astype(F32).reshape(1, D), wr3)

        slot_tok, blk_exp, nused, pos = _moe_plan(route, T)
        wgu = jnp.concatenate([expert_w_gate[l], expert_w_up[l]], axis=2).astype(BF16)
        wd = expert_w_down[l].astype(BF16)
        ys = _experts(h, slot_tok, blk_exp, nused, wgu, wd)

    out = _combine(x2d, route, ys, pos, norm_final.astype(F32).reshape(1, D))
    return out.reshape(B, S, D).astype(x.dtype)
```

```python
import functools

import numpy as np
import jax
import jax.numpy as jnp
from jax import lax
from jax.experimental import pallas as pl
from jax.experimental.pallas import tpu as pltpu

F32 = jnp.float32
BF16 = jnp.bfloat16

D_MODEL = 1024
CHUNK = 64
NORM_EPS = 1e-6
MASK_VALUE = -1e30
MIN_FORGET = 1e-30

A_HEADS, A_DK, A_DV = 4, 64, 64
A_WIDTH = A_HEADS * A_DV
B_HEADS, B_DK, B_DV = 4, 48, 96
B_HALF = B_DK // 2
B_QSLOT = 32
B_VSLOT = 128
B_QW = 2 * B_HEADS * B_QSLOT
B_VW = B_HEADS * B_VSLOT
ROPE_BASE = 10000.0
C_HEADS, C_DH = 6, 64
C_WIDTH = C_HEADS * C_DH
N_PREV_CHUNKS = 8
MAX_REL = 128
C_QB = 2 * CHUNK
C_BWIN = C_QB + N_PREV_CHUNKS * CHUNK

ZA_W = 4 * A_WIDTH
ZB_W = 2 * B_QW + 2 * B_VW
ZC_W = 3 * C_WIDTH
Y_W = A_WIDTH + B_VW + C_WIDTH

N_GROUPS, EXPERTS_PER_GROUP = 4, 8
N_EXPERTS = N_GROUPS * EXPERTS_PER_GROUP
D_EXPERT = 512
ROUTE_W = 128

SUB = 16
VMEM_LIMIT = 56 * 1024 * 1024

TM = 512
TS_A = 256
TS_B = 256
TQ_C = 512
MOE_BM = 256
TM_COMBINE = 256
TM_DISPATCH = TM_COMBINE


def _cparams(*sem):
    return pltpu.CompilerParams(dimension_semantics=sem, vmem_limit_bytes=VMEM_LIMIT)


def _sigmoid(x):
    return 1.0 / (1.0 + jnp.exp(-x))


def _silu(x):
    return x * _sigmoid(x)


def _split3(x):
    hi = x.astype(BF16)
    r1 = x - hi.astype(F32)
    mid = r1.astype(BF16)
    lo = (r1 - mid.astype(F32)).astype(BF16)
    return hi, mid, lo


def _dot(a, b):
    return jnp.dot(a, b, preferred_element_type=F32)


def _dot_nt(a, b):
    return lax.dot_general(a, b, (((1,), (1,)), ((), ())), preferred_element_type=F32)


def _dot_tn(a, b):
    return lax.dot_general(a, b, (((0,), (0,)), ((), ())), preferred_element_type=F32)


LANES = 128
ROW_TILE = D_MODEL // LANES


def _store_row_tiles(ref, val):
    n = val.shape[0]
    for c in range(ROW_TILE):
        ref[pl.ds(c, n, stride=ROW_TILE), :] = val[:, c * LANES:(c + 1) * LANES]


def _load_row_tiles(ref, first_row, n):
    return jnp.concatenate(
        [ref[pl.ds(first_row * ROW_TILE + c, n, stride=ROW_TILE), :] for c in range(ROW_TILE)], axis=1)


def _rms(x, w):
    ms = jnp.mean(x * x, axis=-1, keepdims=True)
    return x * lax.rsqrt(ms + NORM_EPS) * w


def _project(h, w_ref, za_ref, zb_ref, zc_ref):
    hb = h.astype(BF16)
    za_ref[...] = _dot(hb, w_ref[:, 0:ZA_W]).astype(BF16)
    zb_ref[...] = _dot(hb, w_ref[:, ZA_W:ZA_W + ZB_W]).astype(BF16)
    zc_ref[...] = _dot(hb, w_ref[:, ZA_W + ZB_W:]).astype(BF16)


def _inproj_kernel(x_ref, nw_ref, w_ref, za_ref, zb_ref, zc_ref):
    _project(_rms(x_ref[...], nw_ref[...]), w_ref, za_ref, zb_ref, zc_ref)


def _inproj(x2d, nw, w):
    T = x2d.shape[0]
    row = lambda i: (i, 0)
    fixed = lambda i: (0, 0)
    return pl.pallas_call(
        _inproj_kernel,
        grid=(T // TM,),
        in_specs=[pl.BlockSpec((TM, D_MODEL), row),
                  pl.BlockSpec((1, D_MODEL), fixed),
                  pl.BlockSpec((D_MODEL, ZA_W + ZB_W + ZC_W), fixed)],
        out_specs=[pl.BlockSpec((TM, ZA_W), row),
                   pl.BlockSpec((TM, ZB_W), row),
                   pl.BlockSpec((TM, ZC_W), row)],
        out_shape=[jax.ShapeDtypeStruct((T, ZA_W), BF16),
                   jax.ShapeDtypeStruct((T, ZB_W), BF16),
                   jax.ShapeDtypeStruct((T, ZC_W), BF16)],
        compiler_params=_cparams("parallel"),
        name="inproj",
    )(x2d, nw, w)


def _hgrn_kernel(z_ref, lb_ref, nw_ref, tri_ref, eones_ref, bd_ref, o_ref, st_ref):
    @pl.when(pl.program_id(1) == 0)
    def _():
        st_ref[...] = jnp.zeros_like(st_ref)

    W = A_WIDTH
    lb = lb_ref[...]
    lane_head = lax.broadcasted_iota(jnp.int32, (1, W), 1) // A_DK
    head_masks = [lane_head == h for h in range(A_HEADS)]
    tri = tri_ref[...]
    eones = eones_ref[...]
    bd = bd_ref[...]
    ii = lax.broadcasted_iota(jnp.int32, (SUB, SUB, 1), 0)
    jj = lax.broadcasted_iota(jnp.int32, (SUB, SUB, 1), 1)
    causal = jj <= ii
    col = lax.broadcasted_iota(jnp.int32, (1, CHUNK), 1)

    n_chunks = z_ref.shape[0] // CHUNK
    n_sub = CHUNK // SUB
    outs = []
    for c in range(n_chunks):
        r0 = c * CHUNK
        zq = z_ref[r0:r0 + CHUNK, 0:W].astype(F32)
        zf = z_ref[r0:r0 + CHUNK, W:2 * W].astype(F32)
        vv = z_ref[r0:r0 + CHUNK, 2 * W:3 * W]
        vf = vv.astype(F32)
        f = lb + (1.0 - lb) * _sigmoid(zf)
        lf = jnp.log(jnp.maximum(f, MIN_FORGET))
        kk = (1.0 - lb) * _sigmoid(-zf)
        qf = _silu(zq) * (A_DK ** -0.5)
        hi, mid, lo = _split3(lf)
        G = _dot(tri, hi) + _dot(tri, mid) + _dot(tri, lo)
        g_last = G[CHUNK - 1:CHUNK, :]

        st = st_ref[...]
        inter = _dot_nt((qf * jnp.exp(G)).astype(BF16), st.astype(BF16))
        khat = kk * jnp.exp(g_last - G)
        ut = _dot_tn(vv, khat.astype(BF16))
        st_ref[...] = jnp.exp(g_last) * st + ut * bd

        p_rows = []
        for s in range(1, n_sub):
            b = G[s * SUB - 1:s * SUB, :]
            qt = qf[s * SUB:(s + 1) * SUB, :] * jnp.exp(G[s * SUB:(s + 1) * SUB, :] - b)
            kt = kk * jnp.exp(jnp.minimum(b - G, 0.0))
            lhs = jnp.concatenate([jnp.where(m, qt, 0.0) for m in head_masks], axis=0)
            p = _dot_nt(lhs.astype(BF16), kt.astype(BF16))
            p_rows.append(jnp.where(col < s * SUB, p, 0.0))
        pall = jnp.concatenate(p_rows, axis=0)
        pv = _dot(pall.astype(BF16), vv)

        pieces = []
        for s in range(n_sub):
            sl = slice(s * SUB, (s + 1) * SUB)
            gb, qb, kb, vb = G[sl], qf[sl], kk[sl], vf[sl]
            dec = jnp.exp(jnp.minimum(gb[:, None, :] - gb[None, :, :], 0.0))
            a = (qb[:, None, :] * kb[None, :, :]) * dec
            r = _dot(a.reshape(SUB * SUB, W).astype(BF16), eones).reshape(SUB, SUB, W)
            diag = jnp.sum(jnp.where(causal, r, 0.0) * vb[None, :, :], axis=1)
            piece = inter[sl] + diag
            if s > 0:
                base = (s - 1) * CHUNK
                for h in range(A_HEADS):
                    blk = pv[base + h * SUB:base + (h + 1) * SUB, :]
                    piece = piece + jnp.where(head_masks[h], blk, 0.0)
            pieces.append(piece)
        outs.append(jnp.concatenate(pieces, axis=0))

    o = jnp.concatenate(outs, axis=0)
    g = z_ref[:, 3 * W:4 * W].astype(F32)
    o_ref[...] = (_rms(o, nw_ref[...]) * _silu(g)).astype(BF16)


def _hgrn(za, lb, nw, tri, eones, bd, B, S):
    nt = S // TS_A
    fixed = lambda b, t: (0, 0)
    return pl.pallas_call(
        _hgrn_kernel,
        grid=(B, nt),
        in_specs=[pl.BlockSpec((TS_A, ZA_W), lambda b, t: (b * nt + t, 0)),
                  pl.BlockSpec((1, A_WIDTH), fixed),
                  pl.BlockSpec((1, A_WIDTH), fixed),
                  pl.BlockSpec((CHUNK, CHUNK), fixed),
                  pl.BlockSpec((A_WIDTH, A_WIDTH), fixed),
                  pl.BlockSpec((A_WIDTH, A_WIDTH), fixed)],
        out_specs=pl.BlockSpec((TS_A, A_WIDTH), lambda b, t: (b * nt + t, 0)),
        out_shape=jax.ShapeDtypeStruct((B * S, A_WIDTH), BF16),
        scratch_shapes=[pltpu.VMEM((A_WIDTH, A_WIDTH), F32)],
        compiler_params=_cparams("parallel", "arbitrary"),
        name="hgrn2",
    )(za, lb, nw, tri, eones, bd)


def _ret_kernel(z_ref, cos_ref, sin_ref, dstack_ref, qdec_ref, kdec_ref, cg_ref, bd_ref, nw_ref,
                o_ref, st_ref):
    @pl.when(pl.program_id(1) == 0)
    def _():
        st_ref[...] = jnp.zeros_like(st_ref)

    H = B_QW // 2
    lane_head = (lax.broadcasted_iota(jnp.int32, (1, B_QW), 1) % H) // B_QSLOT
    head_masks = [lane_head == h for h in range(B_HEADS)]
    lane_real = lax.broadcasted_iota(jnp.int32, (1, B_VSLOT), 1) < B_DV
    dstack = dstack_ref[...]
    qdec, kdec = qdec_ref[...], kdec_ref[...]
    cg, bd = cg_ref[...], bd_ref[...]

    n_chunks = z_ref.shape[0] // CHUNK
    outs = []
    for c in range(n_chunks):
        rows = slice(c * CHUNK, (c + 1) * CHUNK)
        cos, sin = cos_ref[rows, :], sin_ref[rows, :]

        def rot(off):
            t1 = z_ref[rows, off:off + H].astype(F32)
            t2 = z_ref[rows, off + H:off + 2 * H].astype(F32)
            return jnp.concatenate([t1 * cos - t2 * sin, t1 * sin + t2 * cos], axis=1)

        qr = rot(0)
        kr = rot(B_QW) * (B_DK ** -0.5)
        vv = z_ref[rows, 2 * B_QW:2 * B_QW + B_VW]

        lhs = jnp.concatenate([jnp.where(m, qr, 0.0) for m in head_masks], axis=0)
        sc = _dot_nt(lhs.astype(BF16), kr.astype(BF16)) * dstack
        pv = _dot(sc.astype(BF16), vv)
        intra = jnp.concatenate(
            [pv[h * CHUNK:(h + 1) * CHUNK, h * B_VSLOT:(h + 1) * B_VSLOT] for h in range(B_HEADS)],
            axis=1)
        st = st_ref[...]
        inter = _dot_nt((qr * qdec).astype(BF16), st.astype(BF16))
        ut = _dot_tn(vv, (kr * kdec).astype(BF16))
        st_ref[...] = cg * st + ut * bd
        outs.append(intra + inter)

    o = jnp.concatenate(outs, axis=0)
    normed = []
    for h in range(B_HEADS):
        oh = o[:, h * B_VSLOT:(h + 1) * B_VSLOT]
        mu = jnp.sum(oh, axis=-1, keepdims=True) * (1.0 / B_DV)
        d = jnp.where(lane_real, oh - mu, 0.0)
        var = jnp.sum(d * d, axis=-1, keepdims=True) * (1.0 / B_DV)
        normed.append(d * lax.rsqrt(var + NORM_EPS))
    y = jnp.concatenate(normed, axis=1)
    g = z_ref[:, 2 * B_QW + B_VW:].astype(F32)
    o_ref[...] = (y * nw_ref[...] * _silu(g)).astype(BF16)


def _retention(zb, cos, sin, consts, nw, B, S):
    nt = S // TS_B
    fixed = lambda b, t: (0, 0)
    dstack, qdec, kdec, cg, bd = consts
    return pl.pallas_call(
        _ret_kernel,
        grid=(B, nt),
        in_specs=[pl.BlockSpec((TS_B, ZB_W), lambda b, t: (b * nt + t, 0)),
                  pl.BlockSpec((TS_B, B_QW // 2), lambda b, t: (t, 0)),
                  pl.BlockSpec((TS_B, B_QW // 2), lambda b, t: (t, 0)),
                  pl.BlockSpec(dstack.shape, fixed),
                  pl.BlockSpec(qdec.shape, fixed),
                  pl.BlockSpec(kdec.shape, fixed),
                  pl.BlockSpec(cg.shape, fixed),
                  pl.BlockSpec(bd.shape, fixed),
                  pl.BlockSpec((1, B_VW), fixed)],
        out_specs=pl.BlockSpec((TS_B, B_VW), lambda b, t: (b * nt + t, 0)),
        out_shape=jax.ShapeDtypeStruct((B * S, B_VW), BF16),
        scratch_shapes=[pltpu.VMEM((B_VW, B_QW), F32)],
        compiler_params=_cparams("parallel", "arbitrary"),
        name="retention",
    )(zb, cos, sin, dstack, qdec, kdec, cg, bd, nw)


def _attn_kernel(q_ref, kp_ref, kc_ref, vp_ref, vc_ref, bias_ref, o_ref, kwin_ref, vwin_ref):
    t = pl.program_id(1)
    kwin_ref[0:TQ_C, :] = kp_ref[...]
    kwin_ref[TQ_C:, :] = kc_ref[...]
    vwin_ref[0:TQ_C, :] = vp_ref[...]
    vwin_ref[TQ_C:, :] = vc_ref[...]
    jcol = lax.broadcasted_iota(jnp.int32, (1, C_BWIN), 1)
    first_head = lax.broadcasted_iota(jnp.int32, (1, 2 * C_DH), 1) < C_DH
    zero = jnp.zeros((), BF16)

    def block(i, carry):
        r0 = pl.multiple_of(i * C_QB, C_QB)
        pen = jnp.where((jcol + (r0 + (t - 1) * TQ_C)) >= 0, 0.0, MASK_VALUE)
        for p in range(C_HEADS // 2):
            lanes = slice(2 * p * C_DH, 2 * (p + 1) * C_DH)
            q = q_ref[pl.ds(r0, C_QB), lanes] * (C_DH ** -0.5)
            lhs = jnp.concatenate([jnp.where(first_head, q, zero), jnp.where(first_head, zero, q)], axis=0)
            k = kwin_ref[pl.ds(r0, C_BWIN), lanes]
            v = vwin_ref[pl.ds(r0, C_BWIN), lanes]
            s = _dot_nt(lhs, k) + bias_ref[p] + pen
            m = jnp.max(s, axis=-1, keepdims=True)
            e = jnp.exp(s - m)
            l = jnp.sum(e, axis=-1, keepdims=True)
            pv = _dot(e.astype(BF16), v) / l
            o = jnp.where(first_head, pv[:C_QB], pv[C_QB:])
            o_ref[pl.ds(r0, C_QB), lanes] = o.astype(BF16)
        return carry

    lax.fori_loop(0, TQ_C // C_QB, block, 0)


def _attention(zc, bias, B, S):
    nt = S // TQ_C
    cur = lambda col: (lambda b, t: (b * nt + t, col))
    prev = lambda col: (lambda b, t: (b * nt + jnp.maximum(t - 1, 0), col))
    blk = (TQ_C, C_WIDTH)
    return pl.pallas_call(
        _attn_kernel,
        grid=(B, nt),
        in_specs=[pl.BlockSpec(blk, cur(0)),
                  pl.BlockSpec(blk, prev(1)), pl.BlockSpec(blk, cur(1)),
                  pl.BlockSpec(blk, prev(2)), pl.BlockSpec(blk, cur(2)),
                  pl.BlockSpec(bias.shape, lambda b, t: (0, 0, 0))],
        out_specs=pl.BlockSpec(blk, cur(0)),
        out_shape=jax.ShapeDtypeStruct((B * S, C_WIDTH), BF16),
        scratch_shapes=[pltpu.VMEM((2 * TQ_C, C_WIDTH), BF16),
                        pltpu.VMEM((2 * TQ_C, C_WIDTH), BF16)],
        compiler_params=_cparams("parallel", "arbitrary"),
        name="chunk_attention",
    )(zc, zc, zc, zc, zc, bias)


def _route(logits):
    lane = lax.broadcasted_iota(jnp.int32, logits.shape, 1).astype(F32)
    big = float(1 << 20)
    neg = -jnp.inf

    def first_argmax(vals):
        m = jnp.max(vals, axis=-1, keepdims=True)
        idx = jnp.min(jnp.where(vals == m, lane, big), axis=-1, keepdims=True)
        return m, idx

    gl = jnp.where(lane < N_GROUPS, logits, neg)
    gm, grp = first_argmax(gl)
    p_grp = 1.0 / jnp.sum(jnp.exp(gl - gm), axis=-1, keepdims=True)
    lo = N_GROUPS + grp * EXPERTS_PER_GROUP
    el = jnp.where((lane >= lo) & (lane < lo + EXPERTS_PER_GROUP), logits, neg)
    v1, i1 = first_argmax(el)
    v2, i2 = first_argmax(jnp.where(lane == i1, neg, el))
    e2 = jnp.exp(v2 - v1)
    g1 = p_grp / (1.0 + e2)
    g2 = p_grp * e2 / (1.0 + e2)
    out = jnp.where(lane == 0, i1 - N_GROUPS, 0.0)
    out = jnp.where(lane == 1, i2 - N_GROUPS, out)
    out = jnp.where(lane == 2, g1, out)
    out = jnp.where(lane == 3, g2, out)
    return out


def _outproj_kernel(x_ref, ya_ref, yb_ref, yc_ref, w_ref, nw_ref, wr_ref, xo_ref, h_ref, r_ref):
    a0, a1 = A_WIDTH, A_WIDTH + B_VW
    x = x_ref[...]
    x = x + _dot(ya_ref[...], w_ref[0:a0, :])
    x = x + _dot(yb_ref[...], w_ref[a0:a1, :])
    x = x + _dot(yc_ref[...], w_ref[a1:, :])
    xo_ref[...] = x
    h = _rms(x, nw_ref[...])
    _store_row_tiles(h_ref, h)
    h_hi = h.astype(BF16)
    h_lo = (h - h_hi.astype(F32)).astype(BF16)
    d_hi = _dot(h_hi, wr_ref[...])
    logits = d_hi + pltpu.roll(d_hi, ROUTE_W // 2, 1) + _dot(h_lo, wr_ref[...])
    r_ref[...] = _route(logits)


def _outproj(x2d, ya, yb, yc, w, nw, wr3):
    T = x2d.shape[0]
    row = lambda i: (i, 0)
    fixed = lambda i: (0, 0)
    return pl.pallas_call(
        _outproj_kernel,
        grid=(T // TM,),
        in_specs=[pl.BlockSpec((TM, D_MODEL), row),
                  pl.BlockSpec((TM, A_WIDTH), row),
                  pl.BlockSpec((TM, B_VW), row),
                  pl.BlockSpec((TM, C_WIDTH), row),
                  pl.BlockSpec((Y_W, D_MODEL), fixed),
                  pl.BlockSpec((1, D_MODEL), fixed),
                  pl.BlockSpec((D_MODEL, ROUTE_W), fixed)],
        out_specs=[pl.BlockSpec((TM, D_MODEL), row),
                   pl.BlockSpec((TM * ROW_TILE, LANES), row),
                   pl.BlockSpec((TM, ROUTE_W), row)],
        out_shape=[jax.ShapeDtypeStruct((T, D_MODEL), F32),
                   jax.ShapeDtypeStruct((T * ROW_TILE, LANES), F32),
                   jax.ShapeDtypeStruct((T, ROUTE_W), F32)],
        compiler_params=_cparams("parallel"),
        name="outproj_router",
    )(x2d, ya, yb, yc, w, nw, wr3)


PAD_PIECES = tuple(MOE_BM >> (k + 1) for k in range(MOE_BM.bit_length() - 1))
DMA_GROUP = 8


def _tile_rows(ref, first_tile_row, n_rows=1):
    return ref.at[pl.ds(pl.multiple_of(first_tile_row, ROW_TILE), n_rows * ROW_TILE)]


def _rows_done(src_hbm, dst, sem, n_rows):
    pltpu.make_async_copy(_tile_rows(src_hbm, 0, n_rows), _tile_rows(dst, 0, n_rows), sem).wait()


def _dispatch_kernel(cnt_ref, pstart_ref, pos_ref, h_hbm, zeros_hbm, xs_hbm, sem, zsem):
    i = pl.program_id(0)
    n = pl.num_programs(0)
    tm = TM_DISPATCH

    def pad_pieces(e):
        cnt = cnt_ref[e]
        first = pstart_ref[e] + cnt
        n_pad = (-cnt) & (MOE_BM - 1)
        for p in PAD_PIECES:
            yield (n_pad & p) != 0, first + (n_pad & (MOE_BM - 2 * p)), p

    def spare_blocks():
        last = N_EXPERTS - 1
        n_used = (pstart_ref[last] + cnt_ref[last] + (MOE_BM - 1)) // MOE_BM
        n_blocks = xs_hbm.shape[0] // (MOE_BM * ROW_TILE)
        for b in range(n_blocks - N_EXPERTS, n_blocks):
            for first in range(0, MOE_BM, PAD_PIECES[0]):
                yield b >= n_used, b * MOE_BM + first, PAD_PIECES[0]

    @pl.when(i == 0)
    def _():
        for phase in ("start", "wait"):
            for e in range(N_EXPERTS + 1):
                for fire, slot, p in (pad_pieces(e) if e < N_EXPERTS else spare_blocks()):
                    @pl.when(fire)
                    def _():
                        cp = pltpu.make_async_copy(_tile_rows(zeros_hbm, 0, p),
                                                   _tile_rows(xs_hbm, slot * ROW_TILE, p), zsem)
                        cp.start() if phase == "start" else cp.wait()

    def body(g, carry):
        rows = [g * DMA_GROUP + k for k in range(DMA_GROUP)]
        slots = [(pos_ref[0, 0, r], pos_ref[0, 0, tm + r]) for r in rows]
        for r, (s0, s1) in zip(rows, slots):
            src = _tile_rows(h_hbm, (i * tm + r) * ROW_TILE)
            pltpu.make_async_copy(src, _tile_rows(xs_hbm, s0), sem).start()
            pltpu.make_async_copy(src, _tile_rows(xs_hbm, s1), sem).start()
        return carry
    lax.fori_loop(0, tm // DMA_GROUP, body, 0)

    @pl.when(i > 0)
    def _():
        _rows_done(h_hbm, xs_hbm, sem, 2 * tm)

    @pl.when(i == n - 1)
    def _():
        _rows_done(h_hbm, xs_hbm, sem, 2 * tm)


def _dispatch(h, pos, counts, pad_start, n_slots):
    T = h.shape[0] // ROW_TILE
    tm = TM_DISPATCH
    zeros = jnp.zeros((PAD_PIECES[0] * ROW_TILE, LANES), F32)
    grid_spec = pltpu.PrefetchScalarGridSpec(
        num_scalar_prefetch=2,
        grid=(T // tm,),
        in_specs=[pl.BlockSpec((1, 1, 2 * tm), lambda i, c, s: (i, 0, 0), memory_space=pltpu.SMEM),
                  pl.BlockSpec(memory_space=pl.ANY),
                  pl.BlockSpec(memory_space=pl.ANY)],
        out_specs=pl.BlockSpec(memory_space=pl.ANY),
        scratch_shapes=[pltpu.SemaphoreType.DMA(()), pltpu.SemaphoreType.DMA(())],
    )
    return pl.pallas_call(
        _dispatch_kernel,
        grid_spec=grid_spec,
        out_shape=jax.ShapeDtypeStruct((n_slots * ROW_TILE, LANES), F32),
        compiler_params=_cparams("arbitrary"),
        name="dispatch",
    )(counts, pad_start, pos, h, zeros)


def _expert_kernel(blk_exp_ref, nused_ref, x_ref, wgu_ref, wd_ref, y_ref):
    used = pl.program_id(0) < nused_ref[0]

    @pl.when(used)
    def _():
        xb = _load_row_tiles(x_ref, 0, MOE_BM).astype(BF16)
        gu = _dot(xb, wgu_ref[0])
        act = _silu(gu[:, :D_EXPERT]) * gu[:, D_EXPERT:]
        _store_row_tiles(y_ref, _dot(act.astype(BF16), wd_ref[0]))

    @pl.when(jnp.logical_not(used))
    def _():
        y_ref[...] = jnp.zeros_like(y_ref)


def _experts(xs, blk_exp, nused, wgu, wd):
    nblk = blk_exp.shape[0]
    last_used = lambda i, nu: jnp.minimum(i, nu[0] - 1)
    grid_spec = pltpu.PrefetchScalarGridSpec(
        num_scalar_prefetch=2,
        grid=(nblk,),
        in_specs=[pl.BlockSpec((MOE_BM * ROW_TILE, LANES), lambda i, be, nu: (last_used(i, nu), 0)),
                  pl.BlockSpec((1, D_MODEL, 2 * D_EXPERT), lambda i, be, nu: (be[i], 0, 0)),
                  pl.BlockSpec((1, D_EXPERT, D_MODEL), lambda i, be, nu: (be[i], 0, 0))],
        out_specs=pl.BlockSpec((MOE_BM * ROW_TILE, LANES), lambda i, be, nu: (i, 0)),
    )
    return pl.pallas_call(
        _expert_kernel,
        grid_spec=grid_spec,
        out_shape=jax.ShapeDtypeStruct((nblk * MOE_BM * ROW_TILE, LANES), F32),
        compiler_params=_cparams("arbitrary"),
        name="experts",
    )(blk_exp, nused, xs, wgu, wd)


def _combine_rows(pos_ref, pos_next_ref, x_ref, r_ref, y_hbm, ybuf, sem):
    i = pl.program_id(0)
    n = pl.num_programs(0)
    n_rows = 2 * TM_COMBINE
    slot = lax.rem(i, 2)

    def request(idx_ref, s):
        def body(g, carry):
            rows = [g * 2 * DMA_GROUP + k for k in range(2 * DMA_GROUP)]
            slots = [idx_ref[0, 0, r] for r in rows]
            for r, src_row in zip(rows, slots):
                pltpu.make_async_copy(_tile_rows(y_hbm, src_row),
                                      _tile_rows(ybuf.at[s], r * ROW_TILE), sem.at[s]).start()
            return carry
        lax.fori_loop(0, n_rows // (2 * DMA_GROUP), body, 0)

    @pl.when(i == 0)
    def _():
        request(pos_ref, 0)

    @pl.when(i + 1 < n)
    def _():
        request(pos_next_ref, 1 - slot)

    _rows_done(y_hbm, ybuf.at[slot], sem.at[slot], n_rows)
    r = r_ref[...]
    g0, g1 = r[:, 2:3], r[:, 3:4]
    y0 = _load_row_tiles(ybuf.at[slot], 0, TM_COMBINE)
    y1 = _load_row_tiles(ybuf.at[slot], TM_COMBINE, TM_COMBINE)
    return x_ref[...] + (g0 * y0 + g1 * y1)


def _combine_inproj_kernel(pos_ref, pos_next_ref, x_ref, r_ref, y_hbm, nw_ref, w_ref,
                           xo_ref, za_ref, zb_ref, zc_ref, ybuf, sem):
    x = _combine_rows(pos_ref, pos_next_ref, x_ref, r_ref, y_hbm, ybuf, sem)
    xo_ref[...] = x
    _project(_rms(x, nw_ref[...]), w_ref, za_ref, zb_ref, zc_ref)


def _combine_final_kernel(pos_ref, pos_next_ref, x_ref, r_ref, y_hbm, nw_ref, o_ref, ybuf, sem):
    x = _combine_rows(pos_ref, pos_next_ref, x_ref, r_ref, y_hbm, ybuf, sem)
    o_ref[...] = _rms(x, nw_ref[...])


def _combine(x2d, route, ys, pos, nw, w=None):
    T = x2d.shape[0]
    tm = TM_COMBINE
    row = lambda i: (i, 0)
    fixed = lambda i: (0, 0)
    n_tiles = T // tm
    in_specs = [pl.BlockSpec((1, 1, 2 * tm), lambda i: (i, 0, 0), memory_space=pltpu.SMEM),
                pl.BlockSpec((1, 1, 2 * tm), lambda i: (jnp.minimum(i + 1, n_tiles - 1), 0, 0),
                             memory_space=pltpu.SMEM),
                pl.BlockSpec((tm, D_MODEL), row),
                pl.BlockSpec((tm, ROUTE_W), row),
                pl.BlockSpec(memory_space=pl.ANY),
                pl.BlockSpec((1, D_MODEL), fixed)]
    scratch = [pltpu.VMEM((2, 2 * tm * ROW_TILE, LANES), F32), pltpu.SemaphoreType.DMA((2,))]
    if w is None:
        return pl.pallas_call(
            _combine_final_kernel,
            grid=(T // tm,),
            in_specs=in_specs,
            out_specs=pl.BlockSpec((tm, D_MODEL), row),
            out_shape=jax.ShapeDtypeStruct((T, D_MODEL), F32),
            scratch_shapes=scratch,
            compiler_params=_cparams("arbitrary"),
            name="combine_final",
        )(pos, pos, x2d, route, ys, nw)
    return pl.pallas_call(
        _combine_inproj_kernel,
        grid=(T // tm,),
        in_specs=in_specs + [pl.BlockSpec((D_MODEL, ZA_W + ZB_W + ZC_W), fixed)],
        out_specs=[pl.BlockSpec((tm, D_MODEL), row),
                   pl.BlockSpec((tm, ZA_W), row),
                   pl.BlockSpec((tm, ZB_W), row),
                   pl.BlockSpec((tm, ZC_W), row)],
        out_shape=[jax.ShapeDtypeStruct((T, D_MODEL), F32),
                   jax.ShapeDtypeStruct((T, ZA_W), BF16),
                   jax.ShapeDtypeStruct((T, ZB_W), BF16),
                   jax.ShapeDtypeStruct((T, ZC_W), BF16)],
        scratch_shapes=scratch,
        compiler_params=_cparams("arbitrary"),
        name="combine_inproj",
    )(pos, pos, x2d, route, ys, nw, w)


def _in_column_map():
    a_cols = np.arange(ZA_W)
    off = ZA_W
    qk = np.full((2, 2, B_HEADS, B_QSLOT), -1, np.int64)
    for which in range(2):
        base = off + which * B_HEADS * B_DK
        for h in range(B_HEADS):
            for half in range(2):
                qk[which, half, h, :B_HALF] = base + h * B_DK + half * B_HALF + np.arange(B_HALF)
    vg = np.full((2, B_HEADS, B_VSLOT), -1, np.int64)
    for which in range(2):
        base = off + 2 * B_HEADS * B_DK + which * B_HEADS * B_DV
        for h in range(B_HEADS):
            vg[which, h, :B_DV] = base + h * B_DV + np.arange(B_DV)
    c0 = off + 2 * B_HEADS * B_DK + 2 * B_HEADS * B_DV
    c_cols = c0 + np.arange(ZC_W)
    return np.concatenate([a_cols, qk.reshape(-1), vg.reshape(-1), c_cols])


def _take_or_zero(arr, idx, axis):
    safe = np.maximum(idx, 0)
    taken = jnp.take(arr, safe, axis=axis)
    shape = [1] * arr.ndim
    shape[axis] = idx.shape[0]
    return jnp.where(jnp.asarray(idx >= 0).reshape(shape), taken, 0.0)


def _out_row_map():
    b = np.full((B_HEADS, B_VSLOT), -1, np.int64)
    for h in range(B_HEADS):
        b[h, :B_DV] = A_WIDTH + h * B_DV + np.arange(B_DV)
    c0 = A_WIDTH + B_HEADS * B_DV
    return np.concatenate([np.arange(A_WIDTH), b.reshape(-1), c0 + np.arange(C_WIDTH)])


def _ret_norm_map():
    b = np.full((B_HEADS, B_VSLOT), -1, np.int64)
    for h in range(B_HEADS):
        b[h, :B_DV] = h * B_DV + np.arange(B_DV)
    return b.reshape(-1)


def _retention_tables(S):
    log_gamma = jnp.log1p(-jnp.exp2(-5.0 - jnp.arange(B_HEADS, dtype=F32)))
    idx = jnp.arange(CHUNK, dtype=F32)
    rel = idx[:, None] - idx[None, :]
    decay = jnp.where(rel >= 0, jnp.exp(log_gamma[:, None, None] * jnp.maximum(rel, 0.0)), 0.0)
    dstack = decay.reshape(B_HEADS * CHUNK, CHUNK)
    k_decay = jnp.exp(log_gamma[:, None] * (CHUNK - 1.0 - idx)[None, :])
    q_decay = jnp.exp(log_gamma[:, None] * (idx + 1.0)[None, :])
    chunk_gamma = jnp.exp(log_gamma * CHUNK)

    def lanes(per_head):
        t = jnp.repeat(per_head[..., None], B_QSLOT, axis=-1)
        t = t.reshape(per_head.shape[:-1] + (B_HEADS * B_QSLOT,))
        return jnp.concatenate([t, t], axis=-1)

    qdec = lanes(q_decay.T)
    kdec = lanes(k_decay.T)
    cg = lanes(chunk_gamma[None, :])
    row_head = np.arange(B_VW) // B_VSLOT
    lane_head = (np.arange(B_QW) % (B_QW // 2)) // B_QSLOT
    bd = jnp.asarray((row_head[:, None] == lane_head[None, :]).astype(np.float32))

    half = B_HALF
    inv_freq = ROPE_BASE ** (-jnp.arange(half, dtype=F32) / half)
    ang = jnp.arange(S).astype(F32)[:, None] * inv_freq[None, :]
    pad = ((0, 0), (0, B_QSLOT - half))
    cos = jnp.tile(jnp.pad(jnp.cos(ang), pad), (1, B_HEADS))
    sin = jnp.tile(jnp.pad(jnp.sin(ang), pad), (1, B_HEADS))
    return cos, sin, (dstack, qdec, kdec, cg, bd)


def _attention_bias(rel_bias):
    qi = np.arange(C_QB)[:, None]
    km = np.arange(C_BWIN)[None, :]
    dist = N_PREV_CHUNKS * CHUNK + qi - km
    lag = (qi // CHUNK + N_PREV_CHUNKS) - km // CHUNK
    in_band = jnp.asarray((lag >= 0) & (lag <= N_PREV_CHUNKS))
    b = rel_bias.astype(F32)[:, np.clip(dist, -MAX_REL, MAX_REL) + MAX_REL]
    b = jnp.where(in_band[None], b, MASK_VALUE)
    return b.reshape(C_HEADS // 2, 2 * C_QB, C_BWIN)


def _hgrn_tables():
    tri = jnp.asarray(np.tril(np.ones((CHUNK, CHUNK), np.float32))).astype(BF16)
    head = np.arange(A_WIDTH) // A_DK
    same = (head[:, None] == head[None, :]).astype(np.float32)
    return tri, jnp.asarray(same).astype(BF16), jnp.asarray(same)


def _moe_plan(route, T):
    A = 2 * T
    nblk = A // MOE_BM + N_EXPERTS
    eid = route[:, 0:2].astype(jnp.int32)
    flat_e = eid.reshape(A)
    onehot = (flat_e[:, None] == jnp.arange(N_EXPERTS, dtype=jnp.int32)[None, :]).astype(jnp.int32)
    csum = jnp.cumsum(onehot, axis=0)
    counts = csum[-1]
    rank = jnp.sum((csum - onehot) * onehot, axis=1)
    padded = (counts + MOE_BM - 1) // MOE_BM * MOE_BM
    pad_end = jnp.cumsum(padded)
    pad_start = pad_end - padded
    dest = jnp.sum(onehot * pad_start[None, :], axis=1) + rank
    nused = (pad_end[-1] // MOE_BM).astype(jnp.int32)
    blk_start = jnp.arange(nblk, dtype=jnp.int32) * MOE_BM
    blk_exp = jnp.sum((pad_end[None, :] <= blk_start[:, None]).astype(jnp.int32), axis=1)
    last = jnp.take(blk_exp, jnp.maximum(nused - 1, 0))
    blk_exp = jnp.where(jnp.arange(nblk) < nused, jnp.minimum(blk_exp, N_EXPERTS - 1), last)
    tm = TM_COMBINE
    pos = (dest * ROW_TILE).reshape(T // tm, tm, 2).transpose(0, 2, 1).reshape(T // tm, 1, 2 * tm)
    return pos, counts.astype(jnp.int32), pad_start.astype(jnp.int32), blk_exp, nused.reshape(1)


def kernel(x, w_in, w_out, norm_mix, norm_ffn, norm_final, hgrn_lb, hgrn_norm, ret_norm, rel_bias,
           router_group, router_expert, expert_w_gate, expert_w_up, expert_w_down):
    B, S, D = x.shape
    T = B * S
    depth = w_in.shape[0]

    p = jax.nn.softmax(hgrn_lb.astype(F32), axis=0)
    lower_bounds = jnp.clip(jnp.cumsum(p, axis=0) - p[0], 0.0, 1.0)

    in_map, out_map, rn_map = _in_column_map(), _out_row_map(), _ret_norm_map()
    cos, sin, ret_consts = _retention_tables(S)
    bias = _attention_bias(rel_bias)
    tri, eones, bd_a = _hgrn_tables()

    x2d = x.reshape(T, D).astype(F32)
    route = ys = pos = None
    out = None
    for l in range(depth):
        w_in_l = _take_or_zero(w_in[l], in_map, 1).astype(BF16)
        w_out_l = _take_or_zero(w_out[l], out_map, 0).astype(BF16)
        ret_nw = _take_or_zero(ret_norm[l].astype(F32), rn_map, 0).reshape(1, B_VW)
        nw_mix = norm_mix[l].astype(F32).reshape(1, D)
        if l == 0:
            za, zb, zc = _inproj(x2d, nw_mix, w_in_l)
        else:
            x2d, za, zb, zc = _combine(x2d, route, ys, pos, nw_mix, w_in_l)

        ya = _hgrn(za, lower_bounds[l].reshape(1, A_WIDTH), hgrn_norm[l].astype(F32).reshape(1, A_WIDTH),
                   tri, eones, bd_a, B, S)
        yb = _retention(zb, cos, sin, ret_consts, ret_nw, B, S)
        yc = _attention(zc, bias, B, S)

        wr = jnp.concatenate([router_group[l], router_expert[l]], axis=1).astype(F32)
        wr_hi = wr.astype(BF16)
        wr_lo = (wr - wr_hi.astype(F32)).astype(BF16)
        gap = jnp.zeros((D, ROUTE_W // 2 - wr.shape[1]), BF16)
        wr2 = jnp.concatenate([wr_hi, gap, wr_lo, gap], axis=1)
        x2d, h, route = _outproj(x2d, ya, yb, yc, w_out_l, norm_ffn[l].astype(F32).reshape(1, D), wr2)

        pos, counts, pad_start, blk_exp, nused = _moe_plan(route, T)
        wgu = jnp.concatenate([expert_w_gate[l], expert_w_up[l]], axis=2).astype(BF16)
        wd = expert_w_down[l].astype(BF16)
        xs = _dispatch(h, pos, counts, pad_start, blk_exp.shape[0] * MOE_BM)
        ys = _experts(xs, blk_exp, nused, wgu, wd)

    out = _combine(x2d, route, ys, pos, norm_final.astype(F32).reshape(1, D))
    return out.reshape(B, S, D).astype(x.dtype)
```

```python
import functools

import numpy as np
import jax
import jax.numpy as jnp
from jax import lax
from jax.experimental import pallas as pl
from jax.experimental.pallas import tpu as pltpu

F32 = jnp.float32
BF16 = jnp.bfloat16

D_MODEL = 1024
CHUNK = 64
NORM_EPS = 1e-6
MASK_VALUE = -1e30
MIN_FORGET = 1e-30

A_HEADS, A_DK, A_DV = 4, 64, 64
A_WIDTH = A_HEADS * A_DV
B_HEADS, B_DK, B_DV = 4, 48, 96
B_HALF = B_DK // 2
B_QSLOT = 32
B_VSLOT = 128
B_QW = 2 * B_HEADS * B_QSLOT
B_VW = B_HEADS * B_VSLOT
ROPE_BASE = 10000.0
C_HEADS, C_DH = 6, 64
C_WIDTH = C_HEADS * C_DH
N_PREV_CHUNKS = 8
MAX_REL = 128
C_QB = 2 * CHUNK
C_BWIN = C_QB + N_PREV_CHUNKS * CHUNK

ZA_W = 4 * A_WIDTH
ZB_W = 2 * B_QW + 2 * B_VW
ZC_W = 3 * C_WIDTH
Y_W = A_WIDTH + B_VW + C_WIDTH

N_GROUPS, EXPERTS_PER_GROUP = 4, 8
N_EXPERTS = N_GROUPS * EXPERTS_PER_GROUP
D_EXPERT = 512
ROUTE_W = 128

SUB = 16
VMEM_LIMIT = 56 * 1024 * 1024

TM = 512
TS_A = 256
TS_B = 256
TQ_C = 512
MOE_BM = 256
TM_COMBINE = 256
TM_DISPATCH = TM_COMBINE


def _cparams(*sem):
    return pltpu.CompilerParams(dimension_semantics=sem, vmem_limit_bytes=VMEM_LIMIT)


def _sigmoid(x):
    return 1.0 / (1.0 + jnp.exp(-x))


def _silu(x):
    return x * _sigmoid(x)


def _split3(x):
    hi = x.astype(BF16)
    r1 = x - hi.astype(F32)
    mid = r1.astype(BF16)
    lo = (r1 - mid.astype(F32)).astype(BF16)
    return hi, mid, lo


def _dot(a, b):
    return jnp.dot(a, b, preferred_element_type=F32)


def _dot_nt(a, b):
    return lax.dot_general(a, b, (((1,), (1,)), ((), ())), preferred_element_type=F32)


def _dot_tn(a, b):
    return lax.dot_general(a, b, (((0,), (0,)), ((), ())), preferred_element_type=F32)


LANES = 128
ROW_TILE = D_MODEL // LANES


def _store_row_tiles(ref, val):
    n = val.shape[0]
    for c in range(ROW_TILE):
        ref[pl.ds(c, n, stride=ROW_TILE), :] = val[:, c * LANES:(c + 1) * LANES]


def _load_row_tiles(ref, first_row, n):
    return jnp.concatenate(
        [ref[pl.ds(first_row * ROW_TILE + c, n, stride=ROW_TILE), :] for c in range(ROW_TILE)], axis=1)


def _rms(x, w):
    ms = jnp.mean(x * x, axis=-1, keepdims=True)
    return x * lax.rsqrt(ms + NORM_EPS) * w


def _project(h, w_ref, za_ref, zb_ref, zc_ref):
    hb = h.astype(BF16)
    za_ref[...] = _dot(hb, w_ref[:, 0:ZA_W]).astype(BF16)
    zb_ref[...] = _dot(hb, w_ref[:, ZA_W:ZA_W + ZB_W]).astype(BF16)
    zc_ref[...] = _dot(hb, w_ref[:, ZA_W + ZB_W:]).astype(BF16)


def _inproj_kernel(x_ref, nw_ref, w_ref, za_ref, zb_ref, zc_ref):
    _project(_rms(x_ref[...], nw_ref[...]), w_ref, za_ref, zb_ref, zc_ref)


def _inproj(x2d, nw, w):
    T = x2d.shape[0]
    row = lambda i: (i, 0)
    fixed = lambda i: (0, 0)
    return pl.pallas_call(
        _inproj_kernel,
        grid=(T // TM,),
        in_specs=[pl.BlockSpec((TM, D_MODEL), row),
                  pl.BlockSpec((1, D_MODEL), fixed),
                  pl.BlockSpec((D_MODEL, ZA_W + ZB_W + ZC_W), fixed)],
        out_specs=[pl.BlockSpec((TM, ZA_W), row),
                   pl.BlockSpec((TM, ZB_W), row),
                   pl.BlockSpec((TM, ZC_W), row)],
        out_shape=[jax.ShapeDtypeStruct((T, ZA_W), BF16),
                   jax.ShapeDtypeStruct((T, ZB_W), BF16),
                   jax.ShapeDtypeStruct((T, ZC_W), BF16)],
        compiler_params=_cparams("parallel"),
        name="inproj",
    )(x2d, nw, w)


def _hgrn_kernel(z_ref, lb_ref, nw_ref, tri_ref, eones_ref, bd_ref, o_ref, st_ref):
    @pl.when(pl.program_id(1) == 0)
    def _():
        st_ref[...] = jnp.zeros_like(st_ref)

    W = A_WIDTH
    lb = lb_ref[...]
    lane_head = lax.broadcasted_iota(jnp.int32, (1, W), 1) // A_DK
    head_masks = [lane_head == h for h in range(A_HEADS)]
    tri = tri_ref[...]
    eones = eones_ref[...]
    bd = bd_ref[...]
    ii = lax.broadcasted_iota(jnp.int32, (SUB, SUB, 1), 0)
    jj = lax.broadcasted_iota(jnp.int32, (SUB, SUB, 1), 1)
    causal = jj <= ii
    col = lax.broadcasted_iota(jnp.int32, (1, CHUNK), 1)

    n_chunks = z_ref.shape[0] // CHUNK
    n_sub = CHUNK // SUB
    outs = []
    for c in range(n_chunks):
        r0 = c * CHUNK
        zq = z_ref[r0:r0 + CHUNK, 0:W].astype(F32)
        zf = z_ref[r0:r0 + CHUNK, W:2 * W].astype(F32)
        vv = z_ref[r0:r0 + CHUNK, 2 * W:3 * W]
        vf = vv.astype(F32)
        f = lb + (1.0 - lb) * _sigmoid(zf)
        lf = jnp.log(jnp.maximum(f, MIN_FORGET))
        kk = (1.0 - lb) * _sigmoid(-zf)
        qf = _silu(zq) * (A_DK ** -0.5)
        hi, mid, lo = _split3(lf)
        G = _dot(tri, hi) + _dot(tri, mid) + _dot(tri, lo)
        g_last = G[CHUNK - 1:CHUNK, :]

        st = st_ref[...]
        inter = _dot_nt((qf * jnp.exp(G)).astype(BF16), st.astype(BF16))
        khat = kk * jnp.exp(g_last - G)
        ut = _dot_tn(vv, khat.astype(BF16))
        st_ref[...] = jnp.exp(g_last) * st + ut * bd

        p_rows = []
        for s in range(1, n_sub):
            b = G[s * SUB - 1:s * SUB, :]
            qt = qf[s * SUB:(s + 1) * SUB, :] * jnp.exp(G[s * SUB:(s + 1) * SUB, :] - b)
            kt = kk * jnp.exp(jnp.minimum(b - G, 0.0))
            lhs = jnp.concatenate([jnp.where(m, qt, 0.0) for m in head_masks], axis=0)
            p = _dot_nt(lhs.astype(BF16), kt.astype(BF16))
            p_rows.append(jnp.where(col < s * SUB, p, 0.0))
        pall = jnp.concatenate(p_rows, axis=0)
        pv = _dot(pall.astype(BF16), vv)

        pieces = []
        for s in range(n_sub):
            sl = slice(s * SUB, (s + 1) * SUB)
            gb, qb, kb, vb = G[sl], qf[sl], kk[sl], vf[sl]
            dec = jnp.exp(jnp.minimum(gb[:, None, :] - gb[None, :, :], 0.0))
            a = (qb[:, None, :] * kb[None, :, :]) * dec
            r = _dot(a.reshape(SUB * SUB, W).astype(BF16), eones).reshape(SUB, SUB, W)
            diag = jnp.sum(jnp.where(causal, r, 0.0) * vb[None, :, :], axis=1)
            piece = inter[sl] + diag
            if s > 0:
                base = (s - 1) * CHUNK
                for h in range(A_HEADS):
                    blk = pv[base + h * SUB:base + (h + 1) * SUB, :]
                    piece = piece + jnp.where(head_masks[h], blk, 0.0)
            pieces.append(piece)
        outs.append(jnp.concatenate(pieces, axis=0))

    o = jnp.concatenate(outs, axis=0)
    g = z_ref[:, 3 * W:4 * W].astype(F32)
    o_ref[...] = (_rms(o, nw_ref[...]) * _silu(g)).astype(BF16)


def _hgrn(za, lb, nw, tri, eones, bd, B, S):
    nt = S // TS_A
    fixed = lambda b, t: (0, 0)
    return pl.pallas_call(
        _hgrn_kernel,
        grid=(B, nt),
        in_specs=[pl.BlockSpec((TS_A, ZA_W), lambda b, t: (b * nt + t, 0)),
                  pl.BlockSpec((1, A_WIDTH), fixed),
                  pl.BlockSpec((1, A_WIDTH), fixed),
                  pl.BlockSpec((CHUNK, CHUNK), fixed),
                  pl.BlockSpec((A_WIDTH, A_WIDTH), fixed),
                  pl.BlockSpec((A_WIDTH, A_WIDTH), fixed)],
        out_specs=pl.BlockSpec((TS_A, A_WIDTH), lambda b, t: (b * nt + t, 0)),
        out_shape=jax.ShapeDtypeStruct((B * S, A_WIDTH), BF16),
        scratch_shapes=[pltpu.VMEM((A_WIDTH, A_WIDTH), F32)],
        compiler_params=_cparams("parallel", "arbitrary"),
        name="hgrn2",
    )(za, lb, nw, tri, eones, bd)


def _ret_kernel(z_ref, cos_ref, sin_ref, dstack_ref, qdec_ref, kdec_ref, cg_ref, bd_ref, nw_ref,
                o_ref, st_ref):
    @pl.when(pl.program_id(1) == 0)
    def _():
        st_ref[...] = jnp.zeros_like(st_ref)

    H = B_QW // 2
    lane_head = (lax.broadcasted_iota(jnp.int32, (1, B_QW), 1) % H) // B_QSLOT
    head_masks = [lane_head == h for h in range(B_HEADS)]
    lane_real = lax.broadcasted_iota(jnp.int32, (1, B_VSLOT), 1) < B_DV
    dstack = dstack_ref[...]
    qdec, kdec = qdec_ref[...], kdec_ref[...]
    cg, bd = cg_ref[...], bd_ref[...]

    n_chunks = z_ref.shape[0] // CHUNK
    outs = []
    for c in range(n_chunks):
        rows = slice(c * CHUNK, (c + 1) * CHUNK)
        cos, sin = cos_ref[rows, :], sin_ref[rows, :]

        def rot(off):
            t1 = z_ref[rows, off:off + H].astype(F32)
            t2 = z_ref[rows, off + H:off + 2 * H].astype(F32)
            return jnp.concatenate([t1 * cos - t2 * sin, t1 * sin + t2 * cos], axis=1)

        qr = rot(0)
        kr = rot(B_QW) * (B_DK ** -0.5)
        vv = z_ref[rows, 2 * B_QW:2 * B_QW + B_VW]

        lhs = jnp.concatenate([jnp.where(m, qr, 0.0) for m in head_masks], axis=0)
        sc = _dot_nt(lhs.astype(BF16), kr.astype(BF16)) * dstack
        pv = _dot(sc.astype(BF16), vv)
        intra = jnp.concatenate(
            [pv[h * CHUNK:(h + 1) * CHUNK, h * B_VSLOT:(h + 1) * B_VSLOT] for h in range(B_HEADS)],
            axis=1)
        st = st_ref[...]
        inter = _dot_nt((qr * qdec).astype(BF16), st.astype(BF16))
        ut = _dot_tn(vv, (kr * kdec).astype(BF16))
        st_ref[...] = cg * st + ut * bd
        outs.append(intra + inter)

    o = jnp.concatenate(outs, axis=0)
    normed = []
    for h in range(B_HEADS):
        oh = o[:, h * B_VSLOT:(h + 1) * B_VSLOT]
        mu = jnp.sum(oh, axis=-1, keepdims=True) * (1.0 / B_DV)
        d = jnp.where(lane_real, oh - mu, 0.0)
        var = jnp.sum(d * d, axis=-1, keepdims=True) * (1.0 / B_DV)
        normed.append(d * lax.rsqrt(var + NORM_EPS))
    y = jnp.concatenate(normed, axis=1)
    g = z_ref[:, 2 * B_QW + B_VW:].astype(F32)
    o_ref[...] = (y * nw_ref[...] * _silu(g)).astype(BF16)


def _retention(zb, cos, sin, consts, nw, B, S):
    nt = S // TS_B
    fixed = lambda b, t: (0, 0)
    dstack, qdec, kdec, cg, bd = consts
    return pl.pallas_call(
        _ret_kernel,
        grid=(B, nt),
        in_specs=[pl.BlockSpec((TS_B, ZB_W), lambda b, t: (b * nt + t, 0)),
                  pl.BlockSpec((TS_B, B_QW // 2), lambda b, t: (t, 0)),
                  pl.BlockSpec((TS_B, B_QW // 2), lambda b, t: (t, 0)),
                  pl.BlockSpec(dstack.shape, fixed),
                  pl.BlockSpec(qdec.shape, fixed),
                  pl.BlockSpec(kdec.shape, fixed),
                  pl.BlockSpec(cg.shape, fixed),
                  pl.BlockSpec(bd.shape, fixed),
                  pl.BlockSpec((1, B_VW), fixed)],
        out_specs=pl.BlockSpec((TS_B, B_VW), lambda b, t: (b * nt + t, 0)),
        out_shape=jax.ShapeDtypeStruct((B * S, B_VW), BF16),
        scratch_shapes=[pltpu.VMEM((B_VW, B_QW), F32)],
        compiler_params=_cparams("parallel", "arbitrary"),
        name="retention",
    )(zb, cos, sin, dstack, qdec, kdec, cg, bd, nw)


def _attn_kernel(q_ref, kp_ref, kc_ref, vp_ref, vc_ref, bias_ref, o_ref, kwin_ref, vwin_ref):
    t = pl.program_id(1)
    kwin_ref[0:TQ_C, :] = kp_ref[...]
    kwin_ref[TQ_C:, :] = kc_ref[...]
    vwin_ref[0:TQ_C, :] = vp_ref[...]
    vwin_ref[TQ_C:, :] = vc_ref[...]
    jcol = lax.broadcasted_iota(jnp.int32, (1, C_BWIN), 1)
    first_head = lax.broadcasted_iota(jnp.int32, (1, 2 * C_DH), 1) < C_DH
    zero = jnp.zeros((), BF16)

    def block(i, carry):
        r0 = pl.multiple_of(i * C_QB, C_QB)
        pen = jnp.where((jcol + (r0 + (t - 1) * TQ_C)) >= 0, 0.0, MASK_VALUE)
        for p in range(C_HEADS // 2):
            lanes = slice(2 * p * C_DH, 2 * (p + 1) * C_DH)
            q = q_ref[pl.ds(r0, C_QB), lanes] * (C_DH ** -0.5)
            lhs = jnp.concatenate([jnp.where(first_head, q, zero), jnp.where(first_head, zero, q)], axis=0)
            k = kwin_ref[pl.ds(r0, C_BWIN), lanes]
            v = vwin_ref[pl.ds(r0, C_BWIN), lanes]
            s = _dot_nt(lhs, k) + bias_ref[p] + pen
            m = jnp.max(s, axis=-1, keepdims=True)
            e = jnp.exp(s - m)
            l = jnp.sum(e, axis=-1, keepdims=True)
            pv = _dot(e.astype(BF16), v) / l
            o = jnp.where(first_head, pv[:C_QB], pv[C_QB:])
            o_ref[pl.ds(r0, C_QB), lanes] = o.astype(BF16)
        return carry

    lax.fori_loop(0, TQ_C // C_QB, block, 0)


def _attention(zc, bias, B, S):
    nt = S // TQ_C
    cur = lambda col: (lambda b, t: (b * nt + t, col))
    prev = lambda col: (lambda b, t: (b * nt + jnp.maximum(t - 1, 0), col))
    blk = (TQ_C, C_WIDTH)
    return pl.pallas_call(
        _attn_kernel,
        grid=(B, nt),
        in_specs=[pl.BlockSpec(blk, cur(0)),
                  pl.BlockSpec(blk, prev(1)), pl.BlockSpec(blk, cur(1)),
                  pl.BlockSpec(blk, prev(2)), pl.BlockSpec(blk, cur(2)),
                  pl.BlockSpec(bias.shape, lambda b, t: (0, 0, 0))],
        out_specs=pl.BlockSpec(blk, cur(0)),
        out_shape=jax.ShapeDtypeStruct((B * S, C_WIDTH), BF16),
        scratch_shapes=[pltpu.VMEM((2 * TQ_C, C_WIDTH), BF16),
                        pltpu.VMEM((2 * TQ_C, C_WIDTH), BF16)],
        compiler_params=_cparams("parallel", "arbitrary"),
        name="chunk_attention",
    )(zc, zc, zc, zc, zc, bias)


def _route(logits):
    lane = lax.broadcasted_iota(jnp.int32, logits.shape, 1).astype(F32)
    big = float(1 << 20)
    neg = -jnp.inf

    def first_argmax(vals):
        m = jnp.max(vals, axis=-1, keepdims=True)
        idx = jnp.min(jnp.where(vals == m, lane, big), axis=-1, keepdims=True)
        return m, idx

    gl = jnp.where(lane < N_GROUPS, logits, neg)
    gm, grp = first_argmax(gl)
    p_grp = 1.0 / jnp.sum(jnp.exp(gl - gm), axis=-1, keepdims=True)
    lo = N_GROUPS + grp * EXPERTS_PER_GROUP
    el = jnp.where((lane >= lo) & (lane < lo + EXPERTS_PER_GROUP), logits, neg)
    v1, i1 = first_argmax(el)
    v2, i2 = first_argmax(jnp.where(lane == i1, neg, el))
    e2 = jnp.exp(v2 - v1)
    g1 = p_grp / (1.0 + e2)
    g2 = p_grp * e2 / (1.0 + e2)
    out = jnp.where(lane == 0, i1 - N_GROUPS, 0.0)
    out = jnp.where(lane == 1, i2 - N_GROUPS, out)
    out = jnp.where(lane == 2, g1, out)
    out = jnp.where(lane == 3, g2, out)
    return out


def _outproj_kernel(x_ref, ya_ref, yb_ref, yc_ref, w_ref, nw_ref, wr_ref, xo_ref, h_ref, r_ref):
    a0, a1 = A_WIDTH, A_WIDTH + B_VW
    x = x_ref[...]
    x = x + _dot(ya_ref[...], w_ref[0:a0, :])
    x = x + _dot(yb_ref[...], w_ref[a0:a1, :])
    x = x + _dot(yc_ref[...], w_ref[a1:, :])
    xo_ref[...] = x
    h = _rms(x, nw_ref[...])
    _store_row_tiles(h_ref, h)
    h_hi = h.astype(BF16)
    h_lo = (h - h_hi.astype(F32)).astype(BF16)
    d_hi = _dot(h_hi, wr_ref[...])
    logits = d_hi + pltpu.roll(d_hi, ROUTE_W // 2, 1) + _dot(h_lo, wr_ref[...])
    r_ref[...] = _route(logits)


def _outproj(x2d, ya, yb, yc, w, nw, wr3):
    T = x2d.shape[0]
    row = lambda i: (i, 0)
    fixed = lambda i: (0, 0)
    return pl.pallas_call(
        _outproj_kernel,
        grid=(T // TM,),
        in_specs=[pl.BlockSpec((TM, D_MODEL), row),
                  pl.BlockSpec((TM, A_WIDTH), row),
                  pl.BlockSpec((TM, B_VW), row),
                  pl.BlockSpec((TM, C_WIDTH), row),
                  pl.BlockSpec((Y_W, D_MODEL), fixed),
                  pl.BlockSpec((1, D_MODEL), fixed),
                  pl.BlockSpec((D_MODEL, ROUTE_W), fixed)],
        out_specs=[pl.BlockSpec((TM, D_MODEL), row),
                   pl.BlockSpec((TM * ROW_TILE, LANES), row),
                   pl.BlockSpec((TM, ROUTE_W), row)],
        out_shape=[jax.ShapeDtypeStruct((T, D_MODEL), F32),
                   jax.ShapeDtypeStruct((T * ROW_TILE, LANES), F32),
                   jax.ShapeDtypeStruct((T, ROUTE_W), F32)],
        compiler_params=_cparams("parallel"),
        name="outproj_router",
    )(x2d, ya, yb, yc, w, nw, wr3)


PAD_PIECES = tuple(MOE_BM >> (k + 1) for k in range(MOE_BM.bit_length() - 1))
DMA_GROUP = 8


def _tile_rows(ref, first_tile_row, n_rows=1):
    return ref.at[pl.ds(pl.multiple_of(first_tile_row, ROW_TILE), n_rows * ROW_TILE)]


def _rows_done(src_hbm, dst, sem, n_rows):
    pltpu.make_async_copy(_tile_rows(src_hbm, 0, n_rows), _tile_rows(dst, 0, n_rows), sem).wait()


def _dispatch_kernel(cnt_ref, pstart_ref, pos_ref, h_ref, xs_hbm, hbuf, zbuf, sem, zsem):
    i = pl.program_id(0)
    n = pl.num_programs(0)
    tm = TM_DISPATCH
    cur = lax.rem(i, 2)
    stage = hbuf.at[cur]
    stage[...] = h_ref[...]

    def pad_pieces(e):
        cnt = cnt_ref[e]
        first = pstart_ref[e] + cnt
        n_pad = (-cnt) & (MOE_BM - 1)
        for p in PAD_PIECES:
            yield (n_pad & p) != 0, first + (n_pad & (MOE_BM - 2 * p)), p

    def spare_blocks():
        last = N_EXPERTS - 1
        n_used = (pstart_ref[last] + cnt_ref[last] + (MOE_BM - 1)) // MOE_BM
        n_blocks = xs_hbm.shape[0] // (MOE_BM * ROW_TILE)
        for b in range(n_blocks - N_EXPERTS, n_blocks):
            for first in range(0, MOE_BM, PAD_PIECES[0]):
                yield b >= n_used, b * MOE_BM + first, PAD_PIECES[0]

    @pl.when(i == 0)
    def _():
        zbuf[...] = jnp.zeros_like(zbuf)
        for phase in ("start", "wait"):
            for e in range(N_EXPERTS + 1):
                for fire, slot, p in (pad_pieces(e) if e < N_EXPERTS else spare_blocks()):
                    @pl.when(fire)
                    def _():
                        cp = pltpu.make_async_copy(_tile_rows(zbuf, 0, p),
                                                   _tile_rows(xs_hbm, slot * ROW_TILE, p), zsem)
                        cp.start() if phase == "start" else cp.wait()

    def body(g, carry):
        rows = [g * DMA_GROUP + k for k in range(DMA_GROUP)]
        slots = [(pos_ref[0, 0, r], pos_ref[0, 0, tm + r]) for r in rows]
        for r, (s0, s1) in zip(rows, slots):
            src = _tile_rows(stage, r * ROW_TILE)
            pltpu.make_async_copy(src, _tile_rows(xs_hbm, s0), sem.at[cur]).start()
            pltpu.make_async_copy(src, _tile_rows(xs_hbm, s1), sem.at[cur]).start()
        return carry
    lax.fori_loop(0, tm // DMA_GROUP, body, 0)

    @pl.when(i > 0)
    def _():
        _rows_done(stage, xs_hbm, sem.at[1 - cur], 2 * tm)

    @pl.when(i == n - 1)
    def _():
        _rows_done(stage, xs_hbm, sem.at[cur], 2 * tm)


def _dispatch(h, pos, counts, pad_start, n_slots):
    T = h.shape[0] // ROW_TILE
    tm = TM_DISPATCH
    grid_spec = pltpu.PrefetchScalarGridSpec(
        num_scalar_prefetch=2,
        grid=(T // tm,),
        in_specs=[pl.BlockSpec((1, 1, 2 * tm), lambda i, c, s: (i, 0, 0), memory_space=pltpu.SMEM),
                  pl.BlockSpec((tm * ROW_TILE, LANES), lambda i, c, s: (i, 0))],
        out_specs=pl.BlockSpec(memory_space=pl.ANY),
        scratch_shapes=[pltpu.VMEM((2, tm * ROW_TILE, LANES), F32),
                        pltpu.VMEM((PAD_PIECES[0] * ROW_TILE, LANES), F32),
                        pltpu.SemaphoreType.DMA((2,)), pltpu.SemaphoreType.DMA(())],
    )
    return pl.pallas_call(
        _dispatch_kernel,
        grid_spec=grid_spec,
        out_shape=jax.ShapeDtypeStruct((n_slots * ROW_TILE, LANES), F32),
        compiler_params=_cparams("arbitrary"),
        name="dispatch",
    )(counts, pad_start, pos, h)


def _expert_kernel(blk_exp_ref, nused_ref, x_ref, wg_ref, wu_ref, wd_ref, y_ref, wg_bf, wu_bf, wd_bf):
    i = pl.program_id(0)
    used = i < nused_ref[0]

    @pl.when((i == 0) | (blk_exp_ref[i] != blk_exp_ref[jnp.maximum(i - 1, 0)]))
    def _():
        wg_bf[...] = wg_ref[0, 0].astype(BF16)
        wu_bf[...] = wu_ref[0, 0].astype(BF16)
        wd_bf[...] = wd_ref[0, 0].astype(BF16)

    @pl.when(used)
    def _():
        xb = _load_row_tiles(x_ref, 0, MOE_BM).astype(BF16)
        act = _silu(_dot(xb, wg_bf[...])) * _dot(xb, wu_bf[...])
        _store_row_tiles(y_ref, _dot(act.astype(BF16), wd_bf[...]))

    @pl.when(jnp.logical_not(used))
    def _():
        y_ref[...] = jnp.zeros_like(y_ref)


def _experts(xs, blk_exp, nused, layer, w_gate, w_up, w_down):
    nblk = blk_exp.shape[0]
    last_used = lambda i, nu: jnp.minimum(i, nu[0] - 1)
    expert = lambda i, be, nu: (layer, be[i], 0, 0)
    grid_spec = pltpu.PrefetchScalarGridSpec(
        num_scalar_prefetch=2,
        grid=(nblk,),
        in_specs=[pl.BlockSpec((MOE_BM * ROW_TILE, LANES), lambda i, be, nu: (last_used(i, nu), 0)),
                  pl.BlockSpec((1, 1, D_MODEL, D_EXPERT), expert),
                  pl.BlockSpec((1, 1, D_MODEL, D_EXPERT), expert),
                  pl.BlockSpec((1, 1, D_EXPERT, D_MODEL), expert)],
        out_specs=pl.BlockSpec((MOE_BM * ROW_TILE, LANES), lambda i, be, nu: (i, 0)),
        scratch_shapes=[pltpu.VMEM((D_MODEL, D_EXPERT), BF16), pltpu.VMEM((D_MODEL, D_EXPERT), BF16),
                        pltpu.VMEM((D_EXPERT, D_MODEL), BF16)],
    )
    return pl.pallas_call(
        _expert_kernel,
        grid_spec=grid_spec,
        out_shape=jax.ShapeDtypeStruct((nblk * MOE_BM * ROW_TILE, LANES), F32),
        compiler_params=_cparams("arbitrary"),
        name="experts",
    )(blk_exp, nused, xs, w_gate, w_up, w_down)


def _combine_rows(pos_ref, pos_next_ref, x_ref, r_ref, y_hbm, ybuf, sem):
    i = pl.program_id(0)
    n = pl.num_programs(0)
    n_rows = 2 * TM_COMBINE
    slot = lax.rem(i, 2)

    def request(idx_ref, s):
        def body(g, carry):
            rows = [g * 2 * DMA_GROUP + k for k in range(2 * DMA_GROUP)]
            slots = [idx_ref[0, 0, r] for r in rows]
            for r, src_row in zip(rows, slots):
                pltpu.make_async_copy(_tile_rows(y_hbm, src_row),
                                      _tile_rows(ybuf.at[s], r * ROW_TILE), sem.at[s]).start()
            return carry
        lax.fori_loop(0, n_rows // (2 * DMA_GROUP), body, 0)

    @pl.when(i == 0)
    def _():
        request(pos_ref, 0)

    @pl.when(i + 1 < n)
    def _():
        request(pos_next_ref, 1 - slot)

    _rows_done(y_hbm, ybuf.at[slot], sem.at[slot], n_rows)
    r = r_ref[...]
    g0, g1 = r[:, 2:3], r[:, 3:4]
    y0 = _load_row_tiles(ybuf.at[slot], 0, TM_COMBINE)
    y1 = _load_row_tiles(ybuf.at[slot], TM_COMBINE, TM_COMBINE)
    return x_ref[...] + (g0 * y0 + g1 * y1)


def _combine_inproj_kernel(pos_ref, pos_next_ref, x_ref, r_ref, y_hbm, nw_ref, w_ref,
                           xo_ref, za_ref, zb_ref, zc_ref, ybuf, sem):
    x = _combine_rows(pos_ref, pos_next_ref, x_ref, r_ref, y_hbm, ybuf, sem)
    xo_ref[...] = x
    _project(_rms(x, nw_ref[...]), w_ref, za_ref, zb_ref, zc_ref)


def _combine_final_kernel(pos_ref, pos_next_ref, x_ref, r_ref, y_hbm, nw_ref, o_ref, ybuf, sem):
    x = _combine_rows(pos_ref, pos_next_ref, x_ref, r_ref, y_hbm, ybuf, sem)
    o_ref[...] = _rms(x, nw_ref[...])


def _combine(x2d, route, ys, pos, nw, w=None):
    T = x2d.shape[0]
    tm = TM_COMBINE
    row = lambda i: (i, 0)
    fixed = lambda i: (0, 0)
    n_tiles = T // tm
    in_specs = [pl.BlockSpec((1, 1, 2 * tm), lambda i: (i, 0, 0), memory_space=pltpu.SMEM),
                pl.BlockSpec((1, 1, 2 * tm), lambda i: (jnp.minimum(i + 1, n_tiles - 1), 0, 0),
                             memory_space=pltpu.SMEM),
                pl.BlockSpec((tm, D_MODEL), row),
                pl.BlockSpec((tm, ROUTE_W), row),
                pl.BlockSpec(memory_space=pl.ANY),
                pl.BlockSpec((1, D_MODEL), fixed)]
    scratch = [pltpu.VMEM((2, 2 * tm * ROW_TILE, LANES), F32), pltpu.SemaphoreType.DMA((2,))]
    if w is None:
        return pl.pallas_call(
            _combine_final_kernel,
            grid=(T // tm,),
            in_specs=in_specs,
            out_specs=pl.BlockSpec((tm, D_MODEL), row),
            out_shape=jax.ShapeDtypeStruct((T, D_MODEL), F32),
            scratch_shapes=scratch,
            compiler_params=_cparams("arbitrary"),
            name="combine_final",
        )(pos, pos, x2d, route, ys, nw)
    return pl.pallas_call(
        _combine_inproj_kernel,
        grid=(T // tm,),
        in_specs=in_specs + [pl.BlockSpec((D_MODEL, ZA_W + ZB_W + ZC_W), fixed)],
        out_specs=[pl.BlockSpec((tm, D_MODEL), row),
                   pl.BlockSpec((tm, ZA_W), row),
                   pl.BlockSpec((tm, ZB_W), row),
                   pl.BlockSpec((tm, ZC_W), row)],
        out_shape=[jax.ShapeDtypeStruct((T, D_MODEL), F32),
                   jax.ShapeDtypeStruct((T, ZA_W), BF16),
                   jax.ShapeDtypeStruct((T, ZB_W), BF16),
                   jax.ShapeDtypeStruct((T, ZC_W), BF16)],
        scratch_shapes=scratch,
        compiler_params=_cparams("arbitrary"),
        name="combine_inproj",
    )(pos, pos, x2d, route, ys, nw, w)


def _pad_value_heads(t, axis):
    shape = t.shape
    t = t.reshape(shape[:axis] + (B_HEADS, B_DV) + shape[axis + 1:])
    pad = [(0, 0)] * t.ndim
    pad[axis + 1] = (0, B_VSLOT - B_DV)
    return jnp.pad(t, pad).reshape(shape[:axis] + (B_VW,) + shape[axis + 1:])


def _layout_in_weights(w):
    D = w.shape[0]
    nqk, nv = B_HEADS * B_DK, B_HEADS * B_DV
    o = ZA_W

    def qk(block):
        t = block.reshape(D, B_HEADS, 2, B_HALF).transpose(0, 2, 1, 3)
        t = jnp.pad(t, ((0, 0), (0, 0), (0, 0), (0, B_QSLOT - B_HALF)))
        return t.reshape(D, B_QW)

    return jnp.concatenate([
        w[:, :o], qk(w[:, o:o + nqk]), qk(w[:, o + nqk:o + 2 * nqk]),
        _pad_value_heads(w[:, o + 2 * nqk:o + 2 * nqk + nv], 1),
        _pad_value_heads(w[:, o + 2 * nqk + nv:o + 2 * nqk + 2 * nv], 1),
        w[:, o + 2 * nqk + 2 * nv:]], axis=1)


def _layout_out_weights(w):
    nv = B_HEADS * B_DV
    return jnp.concatenate([w[:A_WIDTH], _pad_value_heads(w[A_WIDTH:A_WIDTH + nv], 0),
                            w[A_WIDTH + nv:]], axis=0)


def _retention_tables(S):
    log_gamma = jnp.log1p(-jnp.exp2(-5.0 - jnp.arange(B_HEADS, dtype=F32)))
    idx = jnp.arange(CHUNK, dtype=F32)
    rel = idx[:, None] - idx[None, :]
    decay = jnp.where(rel >= 0, jnp.exp(log_gamma[:, None, None] * jnp.maximum(rel, 0.0)), 0.0)
    dstack = decay.reshape(B_HEADS * CHUNK, CHUNK)
    k_decay = jnp.exp(log_gamma[:, None] * (CHUNK - 1.0 - idx)[None, :])
    q_decay = jnp.exp(log_gamma[:, None] * (idx + 1.0)[None, :])
    chunk_gamma = jnp.exp(log_gamma * CHUNK)

    def lanes(per_head):
        t = jnp.repeat(per_head[..., None], B_QSLOT, axis=-1)
        t = t.reshape(per_head.shape[:-1] + (B_HEADS * B_QSLOT,))
        return jnp.concatenate([t, t], axis=-1)

    qdec = lanes(q_decay.T)
    kdec = lanes(k_decay.T)
    cg = lanes(chunk_gamma[None, :])
    row_head = np.arange(B_VW) // B_VSLOT
    lane_head = (np.arange(B_QW) % (B_QW // 2)) // B_QSLOT
    bd = jnp.asarray((row_head[:, None] == lane_head[None, :]).astype(np.float32))

    half = B_HALF
    inv_freq = ROPE_BASE ** (-jnp.arange(half, dtype=F32) / half)
    ang = jnp.arange(S).astype(F32)[:, None] * inv_freq[None, :]
    pad = ((0, 0), (0, B_QSLOT - half))
    cos = jnp.tile(jnp.pad(jnp.cos(ang), pad), (1, B_HEADS))
    sin = jnp.tile(jnp.pad(jnp.sin(ang), pad), (1, B_HEADS))
    return cos, sin, (dstack, qdec, kdec, cg, bd)


def _attention_bias(rel_bias):
    qi = np.arange(C_QB)[:, None]
    km = np.arange(C_BWIN)[None, :]
    dist = N_PREV_CHUNKS * CHUNK + qi - km
    lag = (qi // CHUNK + N_PREV_CHUNKS) - km // CHUNK
    in_band = jnp.asarray((lag >= 0) & (lag <= N_PREV_CHUNKS))
    b = rel_bias.astype(F32)[:, np.clip(dist, -MAX_REL, MAX_REL) + MAX_REL]
    b = jnp.where(in_band[None], b, MASK_VALUE)
    return b.reshape(C_HEADS // 2, 2 * C_QB, C_BWIN)


def _hgrn_tables():
    tri = jnp.asarray(np.tril(np.ones((CHUNK, CHUNK), np.float32))).astype(BF16)
    head = np.arange(A_WIDTH) // A_DK
    same = (head[:, None] == head[None, :]).astype(np.float32)
    return tri, jnp.asarray(same).astype(BF16), jnp.asarray(same)


def _moe_plan(route, T):
    A = 2 * T
    nblk = A // MOE_BM + N_EXPERTS
    eid = route[:, 0:2].astype(jnp.int32)
    flat_e = eid.reshape(A)
    onehot = (flat_e[:, None] == jnp.arange(N_EXPERTS, dtype=jnp.int32)[None, :]).astype(jnp.int32)
    csum = jnp.cumsum(onehot, axis=0)
    counts = csum[-1]
    rank = jnp.sum((csum - onehot) * onehot, axis=1)
    padded = (counts + MOE_BM - 1) // MOE_BM * MOE_BM
    pad_end = jnp.cumsum(padded)
    pad_start = pad_end - padded
    dest = jnp.sum(onehot * pad_start[None, :], axis=1) + rank
    nused = (pad_end[-1] // MOE_BM).astype(jnp.int32)
    blk_start = jnp.arange(nblk, dtype=jnp.int32) * MOE_BM
    blk_exp = jnp.sum((pad_end[None, :] <= blk_start[:, None]).astype(jnp.int32), axis=1)
    last = jnp.take(blk_exp, jnp.maximum(nused - 1, 0))
    blk_exp = jnp.where(jnp.arange(nblk) < nused, jnp.minimum(blk_exp, N_EXPERTS - 1), last)
    tm = TM_COMBINE
    pos = (dest * ROW_TILE).reshape(T // tm, tm, 2).transpose(0, 2, 1).reshape(T // tm, 1, 2 * tm)
    return pos, counts.astype(jnp.int32), pad_start.astype(jnp.int32), blk_exp, nused.reshape(1)


def kernel(x, w_in, w_out, norm_mix, norm_ffn, norm_final, hgrn_lb, hgrn_norm, ret_norm, rel_bias,
           router_group, router_expert, expert_w_gate, expert_w_up, expert_w_down):
    B, S, D = x.shape
    T = B * S
    depth = w_in.shape[0]

    p = jax.nn.softmax(hgrn_lb.astype(F32), axis=0)
    lower_bounds = jnp.clip(jnp.cumsum(p, axis=0) - p[0], 0.0, 1.0)

    cos, sin, ret_consts = _retention_tables(S)
    bias = _attention_bias(rel_bias)
    tri, eones, bd_a = _hgrn_tables()

    x2d = x.reshape(T, D).astype(F32)
    route = ys = pos = None
    out = None
    for l in range(depth):
        w_in_l = _layout_in_weights(w_in[l]).astype(BF16)
        w_out_l = _layout_out_weights(w_out[l]).astype(BF16)
        ret_nw = _pad_value_heads(ret_norm[l].astype(F32), 0).reshape(1, B_VW)
        nw_mix = norm_mix[l].astype(F32).reshape(1, D)
        if l == 0:
            za, zb, zc = _inproj(x2d, nw_mix, w_in_l)
        else:
            x2d, za, zb, zc = _combine(x2d, route, ys, pos, nw_mix, w_in_l)

        ya = _hgrn(za, lower_bounds[l].reshape(1, A_WIDTH), hgrn_norm[l].astype(F32).reshape(1, A_WIDTH),
                   tri, eones, bd_a, B, S)
        yb = _retention(zb, cos, sin, ret_consts, ret_nw, B, S)
        yc = _attention(zc, bias, B, S)

        wr = jnp.concatenate([router_group[l], router_expert[l]], axis=1).astype(F32)
        wr_hi = wr.astype(BF16)
        wr_lo = (wr - wr_hi.astype(F32)).astype(BF16)
        gap = jnp.zeros((D, ROUTE_W // 2 - wr.shape[1]), BF16)
        wr2 = jnp.concatenate([wr_hi, gap, wr_lo, gap], axis=1)
        x2d, h, route = _outproj(x2d, ya, yb, yc, w_out_l, norm_ffn[l].astype(F32).reshape(1, D), wr2)

        pos, counts, pad_start, blk_exp, nused = _moe_plan(route, T)
        xs = _dispatch(h, pos, counts, pad_start, blk_exp.shape[0] * MOE_BM)
        ys = _experts(xs, blk_exp, nused, l, expert_w_gate, expert_w_up, expert_w_down)

    out = _combine(x2d, route, ys, pos, norm_final.astype(F32).reshape(1, D))
    return out.reshape(B, S, D).astype(x.dtype)
```

```python
import functools

import numpy as np
import jax
import jax.numpy as jnp
from jax import lax
from jax.experimental import pallas as pl
from jax.experimental.pallas import tpu as pltpu

F32 = jnp.float32
BF16 = jnp.bfloat16

D_MODEL = 1024
CHUNK = 64
NORM_EPS = 1e-6
MASK_VALUE = -1e30
MIN_FORGET = 1e-30

A_HEADS, A_DK, A_DV = 4, 64, 64
A_WIDTH = A_HEADS * A_DV
B_HEADS, B_DK, B_DV = 4, 48, 96
B_HALF = B_DK // 2
B_QSLOT = 32
B_VSLOT = 128
B_QW = 2 * B_HEADS * B_QSLOT
B_VW = B_HEADS * B_VSLOT
ROPE_BASE = 10000.0
C_HEADS, C_DH = 6, 64
C_WIDTH = C_HEADS * C_DH
N_PREV_CHUNKS = 8
MAX_REL = 128
C_QB = 2 * CHUNK
C_BWIN = C_QB + N_PREV_CHUNKS * CHUNK

ZA_W = 4 * A_WIDTH
ZB_W = 2 * B_QW + 2 * B_VW
ZC_W = 3 * C_WIDTH
Y_W = A_WIDTH + B_VW + C_WIDTH

N_GROUPS, EXPERTS_PER_GROUP = 4, 8
N_EXPERTS = N_GROUPS * EXPERTS_PER_GROUP
D_EXPERT = 512
ROUTE_W = 128

SUB = 8
VMEM_LIMIT = 56 * 1024 * 1024

TM = 512
TS_A = 256
TS_B = 256
TQ_C = 512
MOE_BM = 512
TM_COMBINE = 256
TM_DISPATCH = TM_COMBINE


def _cparams(*sem):
    return pltpu.CompilerParams(dimension_semantics=sem, vmem_limit_bytes=VMEM_LIMIT)


def _sigmoid(x):
    return 1.0 / (1.0 + jnp.exp(-x))


def _silu(x):
    return x * _sigmoid(x)


def _split3(x):
    hi = x.astype(BF16)
    r1 = x - hi.astype(F32)
    mid = r1.astype(BF16)
    lo = (r1 - mid.astype(F32)).astype(BF16)
    return hi, mid, lo


def _dot(a, b):
    return jnp.dot(a, b, preferred_element_type=F32)


def _dot_nt(a, b):
    return lax.dot_general(a, b, (((1,), (1,)), ((), ())), preferred_element_type=F32)


def _dot_tn(a, b):
    return lax.dot_general(a, b, (((0,), (0,)), ((), ())), preferred_element_type=F32)


LANES = 128
ROW_TILE = D_MODEL // LANES


def _store_row_tiles(ref, val):
    n = val.shape[0]
    for c in range(ROW_TILE):
        ref[pl.ds(c, n, stride=ROW_TILE), :] = val[:, c * LANES:(c + 1) * LANES]


def _load_row_tiles(ref, first_row, n):
    return jnp.concatenate(
        [ref[pl.ds(first_row * ROW_TILE + c, n, stride=ROW_TILE), :] for c in range(ROW_TILE)], axis=1)


def _rms(x, w):
    ms = jnp.mean(x * x, axis=-1, keepdims=True)
    return x * lax.rsqrt(ms + NORM_EPS) * w


def _project(h, w_ref, za_ref, zb_ref, zc_ref):
    hb = h.astype(BF16)
    za_ref[...] = _dot(hb, w_ref[:, 0:ZA_W]).astype(BF16)
    zb_ref[...] = _dot(hb, w_ref[:, ZA_W:ZA_W + ZB_W]).astype(BF16)
    zc_ref[...] = _dot(hb, w_ref[:, ZA_W + ZB_W:]).astype(BF16)


def _inproj_kernel(x_ref, nw_ref, w_ref, za_ref, zb_ref, zc_ref):
    _project(_rms(x_ref[...], nw_ref[...]), w_ref, za_ref, zb_ref, zc_ref)


def _inproj(x2d, nw, w):
    T = x2d.shape[0]
    row = lambda i: (i, 0)
    fixed = lambda i: (0, 0)
    return pl.pallas_call(
        _inproj_kernel,
        grid=(T // TM,),
        in_specs=[pl.BlockSpec((TM, D_MODEL), row),
                  pl.BlockSpec((1, D_MODEL), fixed),
                  pl.BlockSpec((D_MODEL, ZA_W + ZB_W + ZC_W), fixed)],
        out_specs=[pl.BlockSpec((TM, ZA_W), row),
                   pl.BlockSpec((TM, ZB_W), row),
                   pl.BlockSpec((TM, ZC_W), row)],
        out_shape=[jax.ShapeDtypeStruct((T, ZA_W), BF16),
                   jax.ShapeDtypeStruct((T, ZB_W), BF16),
                   jax.ShapeDtypeStruct((T, ZC_W), BF16)],
        compiler_params=_cparams("parallel"),
        name="inproj",
    )(x2d, nw, w)


def _hgrn_kernel(z_ref, lb_ref, nw_ref, tri_ref, eones_ref, bd_ref, o_ref, st_ref):
    @pl.when(pl.program_id(1) == 0)
    def _():
        st_ref[...] = jnp.zeros_like(st_ref)

    W = A_WIDTH
    lb = lb_ref[...]
    lane_head = lax.broadcasted_iota(jnp.int32, (1, W), 1) // A_DK
    head_masks = [lane_head == h for h in range(A_HEADS)]
    tri = tri_ref[...]
    eones = eones_ref[...]
    bd = bd_ref[...]
    ii = lax.broadcasted_iota(jnp.int32, (SUB, SUB, 1), 0)
    jj = lax.broadcasted_iota(jnp.int32, (SUB, SUB, 1), 1)
    causal = jj <= ii
    col = lax.broadcasted_iota(jnp.int32, (1, CHUNK), 1)

    n_chunks = z_ref.shape[0] // CHUNK
    n_sub = CHUNK // SUB
    outs = []
    for c in range(n_chunks):
        r0 = c * CHUNK
        zq = z_ref[r0:r0 + CHUNK, 0:W].astype(F32)
        zf = z_ref[r0:r0 + CHUNK, W:2 * W].astype(F32)
        vv = z_ref[r0:r0 + CHUNK, 2 * W:3 * W]
        vf = vv.astype(F32)
        f = lb + (1.0 - lb) * _sigmoid(zf)
        lf = jnp.log(jnp.maximum(f, MIN_FORGET))
        kk = (1.0 - lb) * _sigmoid(-zf)
        qf = _silu(zq) * (A_DK ** -0.5)
        hi, mid, lo = _split3(lf)
        G = _dot(tri, hi) + _dot(tri, mid) + _dot(tri, lo)
        g_last = G[CHUNK - 1:CHUNK, :]

        st = st_ref[...]
        inter = _dot_nt((qf * jnp.exp(G)).astype(BF16), st.astype(BF16))
        khat = kk * jnp.exp(g_last - G)
        ut = _dot_tn(vv, khat.astype(BF16))
        st_ref[...] = jnp.exp(g_last) * st + ut * bd

        p_rows = []
        for s in range(1, n_sub):
            b = G[s * SUB - 1:s * SUB, :]
            qt = qf[s * SUB:(s + 1) * SUB, :] * jnp.exp(G[s * SUB:(s + 1) * SUB, :] - b)
            kt = kk * jnp.exp(jnp.minimum(b - G, 0.0))
            lhs = jnp.concatenate([jnp.where(m, qt, 0.0) for m in head_masks], axis=0)
            p = _dot_nt(lhs.astype(BF16), kt.astype(BF16))
            p_rows.append(jnp.where(col < s * SUB, p, 0.0))
        pall = jnp.concatenate(p_rows, axis=0)
        pv = _dot(pall.astype(BF16), vv)

        pieces = []
        for s in range(n_sub):
            sl = slice(s * SUB, (s + 1) * SUB)
            gb, qb, kb, vb = G[sl], qf[sl], kk[sl], vf[sl]
            dec = jnp.exp(jnp.minimum(gb[:, None, :] - gb[None, :, :], 0.0))
            a = (qb[:, None, :] * kb[None, :, :]) * dec
            r = _dot(a.reshape(SUB * SUB, W).astype(BF16), eones).reshape(SUB, SUB, W)
            diag = jnp.sum(jnp.where(causal, r, 0.0) * vb[None, :, :], axis=1)
            piece = inter[sl] + diag
            if s > 0:
                base = (s - 1) * A_HEADS * SUB
                for h in range(A_HEADS):
                    blk = pv[base + h * SUB:base + (h + 1) * SUB, :]
                    piece = piece + jnp.where(head_masks[h], blk, 0.0)
            pieces.append(piece)
        outs.append(jnp.concatenate(pieces, axis=0))

    o = jnp.concatenate(outs, axis=0)
    g = z_ref[:, 3 * W:4 * W].astype(F32)
    o_ref[...] = (_rms(o, nw_ref[...]) * _silu(g)).astype(BF16)


def _hgrn(za, lb, nw, tri, eones, bd, B, S):
    nt = S // TS_A
    fixed = lambda b, t: (0, 0)
    return pl.pallas_call(
        _hgrn_kernel,
        grid=(B, nt),
        in_specs=[pl.BlockSpec((TS_A, ZA_W), lambda b, t: (b * nt + t, 0)),
                  pl.BlockSpec((1, A_WIDTH), fixed),
                  pl.BlockSpec((1, A_WIDTH), fixed),
                  pl.BlockSpec((CHUNK, CHUNK), fixed),
                  pl.BlockSpec((A_WIDTH, A_WIDTH), fixed),
                  pl.BlockSpec((A_WIDTH, A_WIDTH), fixed)],
        out_specs=pl.BlockSpec((TS_A, A_WIDTH), lambda b, t: (b * nt + t, 0)),
        out_shape=jax.ShapeDtypeStruct((B * S, A_WIDTH), BF16),
        scratch_shapes=[pltpu.VMEM((A_WIDTH, A_WIDTH), F32)],
        compiler_params=_cparams("parallel", "arbitrary"),
        name="hgrn2",
    )(za, lb, nw, tri, eones, bd)


def _ret_kernel(z_ref, cos_ref, sin_ref, dstack_ref, qdec_ref, kdec_ref, cg_ref, bd_ref, nw_ref,
                o_ref, st_ref):
    @pl.when(pl.program_id(1) == 0)
    def _():
        st_ref[...] = jnp.zeros_like(st_ref)

    H = B_QW // 2
    lane_head = (lax.broadcasted_iota(jnp.int32, (1, B_QW), 1) % H) // B_QSLOT
    head_masks = [lane_head == h for h in range(B_HEADS)]
    lane_real = lax.broadcasted_iota(jnp.int32, (1, B_VSLOT), 1) < B_DV
    dstack = dstack_ref[...]
    qdec, kdec = qdec_ref[...], kdec_ref[...]
    cg, bd = cg_ref[...], bd_ref[...]

    n_chunks = z_ref.shape[0] // CHUNK
    outs = []
    for c in range(n_chunks):
        rows = slice(c * CHUNK, (c + 1) * CHUNK)
        cos, sin = cos_ref[rows, :], sin_ref[rows, :]

        def rot(off):
            t1 = z_ref[rows, off:off + H].astype(F32)
            t2 = z_ref[rows, off + H:off + 2 * H].astype(F32)
            return jnp.concatenate([t1 * cos - t2 * sin, t1 * sin + t2 * cos], axis=1)

        qr = rot(0)
        kr = rot(B_QW) * (B_DK ** -0.5)
        vv = z_ref[rows, 2 * B_QW:2 * B_QW + B_VW]

        lhs = jnp.concatenate([jnp.where(m, qr, 0.0) for m in head_masks], axis=0)
        sc = _dot_nt(lhs.astype(BF16), kr.astype(BF16)) * dstack
        pv = _dot(sc.astype(BF16), vv)
        intra = jnp.concatenate(
            [pv[h * CHUNK:(h + 1) * CHUNK, h * B_VSLOT:(h + 1) * B_VSLOT] for h in range(B_HEADS)],
            axis=1)
        st = st_ref[...]
        inter = _dot_nt((qr * qdec).astype(BF16), st.astype(BF16))
        ut = _dot_tn(vv, (kr * kdec).astype(BF16))
        st_ref[...] = cg * st + ut * bd
        outs.append(intra + inter)

    o = jnp.concatenate(outs, axis=0)
    normed = []
    for h in range(B_HEADS):
        oh = o[:, h * B_VSLOT:(h + 1) * B_VSLOT]
        mu = jnp.sum(oh, axis=-1, keepdims=True) * (1.0 / B_DV)
        d = jnp.where(lane_real, oh - mu, 0.0)
        var = jnp.sum(d * d, axis=-1, keepdims=True) * (1.0 / B_DV)
        normed.append(d * lax.rsqrt(var + NORM_EPS))
    y = jnp.concatenate(normed, axis=1)
    g = z_ref[:, 2 * B_QW + B_VW:].astype(F32)
    o_ref[...] = (y * nw_ref[...] * _silu(g)).astype(BF16)


def _retention(zb, cos, sin, consts, nw, B, S):
    nt = S // TS_B
    fixed = lambda b, t: (0, 0)
    dstack, qdec, kdec, cg, bd = consts
    return pl.pallas_call(
        _ret_kernel,
        grid=(B, nt),
        in_specs=[pl.BlockSpec((TS_B, ZB_W), lambda b, t: (b * nt + t, 0)),
                  pl.BlockSpec((TS_B, B_QW // 2), lambda b, t: (t, 0)),
                  pl.BlockSpec((TS_B, B_QW // 2), lambda b, t: (t, 0)),
                  pl.BlockSpec(dstack.shape, fixed),
                  pl.BlockSpec(qdec.shape, fixed),
                  pl.BlockSpec(kdec.shape, fixed),
                  pl.BlockSpec(cg.shape, fixed),
                  pl.BlockSpec(bd.shape, fixed),
                  pl.BlockSpec((1, B_VW), fixed)],
        out_specs=pl.BlockSpec((TS_B, B_VW), lambda b, t: (b * nt + t, 0)),
        out_shape=jax.ShapeDtypeStruct((B * S, B_VW), BF16),
        scratch_shapes=[pltpu.VMEM((B_VW, B_QW), F32)],
        compiler_params=_cparams("parallel", "arbitrary"),
        name="retention",
    )(zb, cos, sin, dstack, qdec, kdec, cg, bd, nw)


def _attn_kernel(q_ref, kp_ref, kc_ref, vp_ref, vc_ref, bias_ref, o_ref, kwin_ref, vwin_ref):
    t = pl.program_id(1)
    kwin_ref[0:TQ_C, :] = kp_ref[...]
    kwin_ref[TQ_C:, :] = kc_ref[...]
    vwin_ref[0:TQ_C, :] = vp_ref[...]
    vwin_ref[TQ_C:, :] = vc_ref[...]
    jcol = lax.broadcasted_iota(jnp.int32, (1, C_BWIN), 1)
    first_head = lax.broadcasted_iota(jnp.int32, (1, 2 * C_DH), 1) < C_DH
    zero = jnp.zeros((), BF16)

    def block(i, carry):
        r0 = pl.multiple_of(i * C_QB, C_QB)
        pen = jnp.where((jcol + (r0 + (t - 1) * TQ_C)) >= 0, 0.0, MASK_VALUE)
        for p in range(C_HEADS // 2):
            lanes = slice(2 * p * C_DH, 2 * (p + 1) * C_DH)
            q = q_ref[pl.ds(r0, C_QB), lanes] * (C_DH ** -0.5)
            lhs = jnp.concatenate([jnp.where(first_head, q, zero), jnp.where(first_head, zero, q)], axis=0)
            k = kwin_ref[pl.ds(r0, C_BWIN), lanes]
            v = vwin_ref[pl.ds(r0, C_BWIN), lanes]
            s = _dot_nt(lhs, k) + bias_ref[p] + pen
            m = jnp.max(s, axis=-1, keepdims=True)
            e = jnp.exp(s - m)
            l = jnp.sum(e, axis=-1, keepdims=True)
            pv = _dot(e.astype(BF16), v) / l
            o = jnp.where(first_head, pv[:C_QB], pv[C_QB:])
            o_ref[pl.ds(r0, C_QB), lanes] = o.astype(BF16)
        return carry

    lax.fori_loop(0, TQ_C // C_QB, block, 0)


def _attention(zc, bias, B, S):
    nt = S // TQ_C
    cur = lambda col: (lambda b, t: (b * nt + t, col))
    prev = lambda col: (lambda b, t: (b * nt + jnp.maximum(t - 1, 0), col))
    blk = (TQ_C, C_WIDTH)
    return pl.pallas_call(
        _attn_kernel,
        grid=(B, nt),
        in_specs=[pl.BlockSpec(blk, cur(0)),
                  pl.BlockSpec(blk, prev(1)), pl.BlockSpec(blk, cur(1)),
                  pl.BlockSpec(blk, prev(2)), pl.BlockSpec(blk, cur(2)),
                  pl.BlockSpec(bias.shape, lambda b, t: (0, 0, 0))],
        out_specs=pl.BlockSpec(blk, cur(0)),
        out_shape=jax.ShapeDtypeStruct((B * S, C_WIDTH), BF16),
        scratch_shapes=[pltpu.VMEM((2 * TQ_C, C_WIDTH), BF16),
                        pltpu.VMEM((2 * TQ_C, C_WIDTH), BF16)],
        compiler_params=_cparams("parallel", "arbitrary"),
        name="chunk_attention",
    )(zc, zc, zc, zc, zc, bias)


def _route(logits):
    lane = lax.broadcasted_iota(jnp.int32, logits.shape, 1).astype(F32)
    big = float(1 << 20)
    neg = -jnp.inf

    def first_argmax(vals):
        m = jnp.max(vals, axis=-1, keepdims=True)
        idx = jnp.min(jnp.where(vals == m, lane, big), axis=-1, keepdims=True)
        return m, idx

    gl = jnp.where(lane < N_GROUPS, logits, neg)
    gm, grp = first_argmax(gl)
    p_grp = 1.0 / jnp.sum(jnp.exp(gl - gm), axis=-1, keepdims=True)
    lo = N_GROUPS + grp * EXPERTS_PER_GROUP
    el = jnp.where((lane >= lo) & (lane < lo + EXPERTS_PER_GROUP), logits, neg)
    v1, i1 = first_argmax(el)
    v2, i2 = first_argmax(jnp.where(lane == i1, neg, el))
    e2 = jnp.exp(v2 - v1)
    g1 = p_grp / (1.0 + e2)
    g2 = p_grp * e2 / (1.0 + e2)
    out = jnp.where(lane == 0, i1 - N_GROUPS, 0.0)
    out = jnp.where(lane == 1, i2 - N_GROUPS, out)
    out = jnp.where(lane == 2, g1, out)
    out = jnp.where(lane == 3, g2, out)
    return out


def _outproj_kernel(x_ref, ya_ref, yb_ref, yc_ref, w_ref, nw_ref, wr_ref, xo_ref, r_ref):
    a0, a1 = A_WIDTH, A_WIDTH + B_VW
    x = x_ref[...]
    x = x + _dot(ya_ref[...], w_ref[0:a0, :])
    x = x + _dot(yb_ref[...], w_ref[a0:a1, :])
    x = x + _dot(yc_ref[...], w_ref[a1:, :])
    xo_ref[...] = x
    h = _rms(x, nw_ref[...])
    h_hi = h.astype(BF16)
    h_lo = (h - h_hi.astype(F32)).astype(BF16)
    d_hi = _dot(h_hi, wr_ref[...])
    logits = d_hi + pltpu.roll(d_hi, ROUTE_W // 2, 1) + _dot(h_lo, wr_ref[...])
    r_ref[...] = _route(logits)


def _outproj(x2d, ya, yb, yc, w, nw, wr3):
    T = x2d.shape[0]
    row = lambda i: (i, 0)
    fixed = lambda i: (0, 0)
    return pl.pallas_call(
        _outproj_kernel,
        grid=(T // TM,),
        in_specs=[pl.BlockSpec((TM, D_MODEL), row),
                  pl.BlockSpec((TM, A_WIDTH), row),
                  pl.BlockSpec((TM, B_VW), row),
                  pl.BlockSpec((TM, C_WIDTH), row),
                  pl.BlockSpec((Y_W, D_MODEL), fixed),
                  pl.BlockSpec((1, D_MODEL), fixed),
                  pl.BlockSpec((D_MODEL, ROUTE_W), fixed)],
        out_specs=[pl.BlockSpec((TM, D_MODEL), row),
                   pl.BlockSpec((TM, ROUTE_W), row)],
        out_shape=[jax.ShapeDtypeStruct((T, D_MODEL), F32),
                   jax.ShapeDtypeStruct((T, ROUTE_W), F32)],
        compiler_params=_cparams("parallel"),
        name="outproj_router",
    )(x2d, ya, yb, yc, w, nw, wr3)


PAD_PIECES = tuple(MOE_BM >> (k + 1) for k in range(MOE_BM.bit_length() - 1))
DMA_GROUP = 8


def _tile_rows(ref, first_tile_row, n_rows=1):
    return ref.at[pl.ds(pl.multiple_of(first_tile_row, ROW_TILE), n_rows * ROW_TILE)]


def _rows_done(src_hbm, dst, sem, n_rows):
    pltpu.make_async_copy(_tile_rows(src_hbm, 0, n_rows), _tile_rows(dst, 0, n_rows), sem).wait()


def _dispatch_kernel(cnt_ref, pstart_ref, pos_ref, x_ref, nw_ref, xs_hbm, hbuf, zbuf, sem, zsem):
    i = pl.program_id(0)
    n = pl.num_programs(0)
    tm = TM_DISPATCH
    cur = lax.rem(i, 2)
    stage = hbuf.at[cur]
    _store_row_tiles(stage, _rms(x_ref[...], nw_ref[...]))

    def pad_pieces(e):
        cnt = cnt_ref[e]
        first = pstart_ref[e] + cnt
        n_pad = (-cnt) & (MOE_BM - 1)
        for p in PAD_PIECES:
            yield (n_pad & p) != 0, first + (n_pad & (MOE_BM - 2 * p)), p

    def spare_blocks():
        last = N_EXPERTS - 1
        n_used = (pstart_ref[last] + cnt_ref[last] + (MOE_BM - 1)) // MOE_BM
        n_blocks = xs_hbm.shape[0] // (MOE_BM * ROW_TILE)
        for b in range(n_blocks - N_EXPERTS, n_blocks):
            for first in range(0, MOE_BM, PAD_PIECES[0]):
                yield b >= n_used, b * MOE_BM + first, PAD_PIECES[0]

    @pl.when(i == 0)
    def _():
        zbuf[...] = jnp.zeros_like(zbuf)
        for phase in ("start", "wait"):
            for e in range(N_EXPERTS + 1):
                for fire, slot, p in (pad_pieces(e) if e < N_EXPERTS else spare_blocks()):
                    @pl.when(fire)
                    def _():
                        cp = pltpu.make_async_copy(_tile_rows(zbuf, 0, p),
                                                   _tile_rows(xs_hbm, slot * ROW_TILE, p), zsem)
                        cp.start() if phase == "start" else cp.wait()

    def body(g, carry):
        rows = [g * DMA_GROUP + k for k in range(DMA_GROUP)]
        slots = [(pos_ref[0, 0, r], pos_ref[0, 0, tm + r]) for r in rows]
        for r, (s0, s1) in zip(rows, slots):
            src = _tile_rows(stage, r * ROW_TILE)
            pltpu.make_async_copy(src, _tile_rows(xs_hbm, s0), sem.at[cur]).start()
            pltpu.make_async_copy(src, _tile_rows(xs_hbm, s1), sem.at[cur]).start()
        return carry
    lax.fori_loop(0, tm // DMA_GROUP, body, 0)

    @pl.when(i > 0)
    def _():
        _rows_done(stage, xs_hbm, sem.at[1 - cur], 2 * tm)

    @pl.when(i == n - 1)
    def _():
        _rows_done(stage, xs_hbm, sem.at[cur], 2 * tm)


def _dispatch(x2d, nw, pos, counts, pad_start, n_slots):
    T = x2d.shape[0]
    tm = TM_DISPATCH
    grid_spec = pltpu.PrefetchScalarGridSpec(
        num_scalar_prefetch=2,
        grid=(T // tm,),
        in_specs=[pl.BlockSpec((1, 1, 2 * tm), lambda i, c, s: (i, 0, 0), memory_space=pltpu.SMEM),
                  pl.BlockSpec((tm, D_MODEL), lambda i, c, s: (i, 0)),
                  pl.BlockSpec((1, D_MODEL), lambda i, c, s: (0, 0))],
        out_specs=pl.BlockSpec(memory_space=pl.ANY),
        scratch_shapes=[pltpu.VMEM((2, tm * ROW_TILE, LANES), F32),
                        pltpu.VMEM((PAD_PIECES[0] * ROW_TILE, LANES), F32),
                        pltpu.SemaphoreType.DMA((2,)), pltpu.SemaphoreType.DMA(())],
    )
    return pl.pallas_call(
        _dispatch_kernel,
        grid_spec=grid_spec,
        out_shape=jax.ShapeDtypeStruct((n_slots * ROW_TILE, LANES), F32),
        compiler_params=_cparams("arbitrary"),
        name="dispatch",
    )(counts, pad_start, pos, x2d, nw)


def _expert_kernel(blk_exp_ref, nused_ref, x_ref, wg_ref, wu_ref, wd_ref, y_ref, wg_bf, wu_bf, wd_bf):
    i = pl.program_id(0)
    used = i < nused_ref[0]

    @pl.when((i == 0) | (blk_exp_ref[i] != blk_exp_ref[jnp.maximum(i - 1, 0)]))
    def _():
        wg_bf[...] = wg_ref[0, 0].astype(BF16)
        wu_bf[...] = wu_ref[0, 0].astype(BF16)
        wd_bf[...] = wd_ref[0, 0].astype(BF16)

    @pl.when(used)
    def _():
        xb = _load_row_tiles(x_ref, 0, MOE_BM).astype(BF16)
        act = _silu(_dot(xb, wg_bf[...])) * _dot(xb, wu_bf[...])
        _store_row_tiles(y_ref, _dot(act.astype(BF16), wd_bf[...]))

    @pl.when(jnp.logical_not(used))
    def _():
        y_ref[...] = jnp.zeros_like(y_ref)


def _experts(xs, blk_exp, nused, layer, w_gate, w_up, w_down):
    nblk = blk_exp.shape[0]
    last_used = lambda i, nu: jnp.minimum(i, nu[0] - 1)
    expert = lambda i, be, nu: (layer, be[i], 0, 0)
    grid_spec = pltpu.PrefetchScalarGridSpec(
        num_scalar_prefetch=2,
        grid=(nblk,),
        in_specs=[pl.BlockSpec((MOE_BM * ROW_TILE, LANES), lambda i, be, nu: (last_used(i, nu), 0)),
                  pl.BlockSpec((1, 1, D_MODEL, D_EXPERT), expert),
                  pl.BlockSpec((1, 1, D_MODEL, D_EXPERT), expert),
                  pl.BlockSpec((1, 1, D_EXPERT, D_MODEL), expert)],
        out_specs=pl.BlockSpec((MOE_BM * ROW_TILE, LANES), lambda i, be, nu: (i, 0)),
        scratch_shapes=[pltpu.VMEM((D_MODEL, D_EXPERT), BF16), pltpu.VMEM((D_MODEL, D_EXPERT), BF16),
                        pltpu.VMEM((D_EXPERT, D_MODEL), BF16)],
    )
    return pl.pallas_call(
        _expert_kernel,
        grid_spec=grid_spec,
        out_shape=jax.ShapeDtypeStruct((nblk * MOE_BM * ROW_TILE, LANES), F32),
        compiler_params=_cparams("arbitrary"),
        name="experts",
    )(blk_exp, nused, xs, w_gate, w_up, w_down)


def _combine_rows(pos_ref, pos_next_ref, x_ref, r_ref, y_hbm, ybuf, sem):
    i = pl.program_id(0)
    n_rows = 2 * TM_COMBINE
    slot = lax.rem(i, 2)

    @pl.when(i == 0)
    def _():
        _request_rows(pos_ref, y_hbm, ybuf, sem, 0, unrolled=False)

    _rows_done(y_hbm, ybuf.at[slot], sem.at[slot], n_rows)
    r = r_ref[...]
    g0, g1 = r[:, 2:3], r[:, 3:4]
    y0 = _load_row_tiles(ybuf.at[slot], 0, TM_COMBINE)
    y1 = _load_row_tiles(ybuf.at[slot], TM_COMBINE, TM_COMBINE)
    return x_ref[...] + (g0 * y0 + g1 * y1)


def _request_rows(idx_ref, y_hbm, ybuf, sem, s, unrolled):
    n_rows = 2 * TM_COMBINE
    group = 2 * DMA_GROUP

    def body(g, carry):
        rows = [g * group + k for k in range(group)]
        slots = [idx_ref[0, 0, r] for r in rows]
        for r, src_row in zip(rows, slots):
            pltpu.make_async_copy(_tile_rows(y_hbm, src_row),
                                  _tile_rows(ybuf.at[s], r * ROW_TILE), sem.at[s]).start()
        return carry

    if unrolled:
        for g in range(n_rows // group):
            body(g, 0)
    else:
        lax.fori_loop(0, n_rows // group, body, 0)


def _request_next(pos_next_ref, y_hbm, ybuf, sem):
    _request_rows(pos_next_ref, y_hbm, ybuf, sem, 1 - lax.rem(pl.program_id(0), 2), unrolled=True)


def _drain_last(y_hbm, ybuf, sem):
    i = pl.program_id(0)

    @pl.when(i == pl.num_programs(0) - 1)
    def _():
        other = 1 - lax.rem(i, 2)
        _rows_done(y_hbm, ybuf.at[other], sem.at[other], 2 * TM_COMBINE)


def _combine_inproj_kernel(pos_ref, pos_next_ref, x_ref, r_ref, y_hbm, nw_ref, w_ref,
                           xo_ref, za_ref, zb_ref, zc_ref, ybuf, sem):
    x = _combine_rows(pos_ref, pos_next_ref, x_ref, r_ref, y_hbm, ybuf, sem)
    xo_ref[...] = x
    _request_next(pos_next_ref, y_hbm, ybuf, sem)
    _project(_rms(x, nw_ref[...]), w_ref, za_ref, zb_ref, zc_ref)
    _drain_last(y_hbm, ybuf, sem)


def _combine_final_kernel(pos_ref, pos_next_ref, x_ref, r_ref, y_hbm, nw_ref, o_ref, ybuf, sem):
    x = _combine_rows(pos_ref, pos_next_ref, x_ref, r_ref, y_hbm, ybuf, sem)
    _request_next(pos_next_ref, y_hbm, ybuf, sem)
    o_ref[...] = _rms(x, nw_ref[...])
    _drain_last(y_hbm, ybuf, sem)


def _combine(x2d, route, ys, pos, nw, w=None):
    T = x2d.shape[0]
    tm = TM_COMBINE
    row = lambda i: (i, 0)
    fixed = lambda i: (0, 0)
    n_tiles = T // tm
    in_specs = [pl.BlockSpec((1, 1, 2 * tm), lambda i: (i, 0, 0), memory_space=pltpu.SMEM),
                pl.BlockSpec((1, 1, 2 * tm), lambda i: (jnp.minimum(i + 1, n_tiles - 1), 0, 0),
                             memory_space=pltpu.SMEM),
                pl.BlockSpec((tm, D_MODEL), row),
                pl.BlockSpec((tm, ROUTE_W), row),
                pl.BlockSpec(memory_space=pl.ANY),
                pl.BlockSpec((1, D_MODEL), fixed)]
    scratch = [pltpu.VMEM((2, 2 * tm * ROW_TILE, LANES), F32), pltpu.SemaphoreType.DMA((2,))]
    if w is None:
        return pl.pallas_call(
            _combine_final_kernel,
            grid=(T // tm,),
            in_specs=in_specs,
            out_specs=pl.BlockSpec((tm, D_MODEL), row),
            out_shape=jax.ShapeDtypeStruct((T, D_MODEL), F32),
            scratch_shapes=scratch,
            compiler_params=_cparams("arbitrary"),
            name="combine_final",
        )(pos, pos, x2d, route, ys, nw)
    return pl.pallas_call(
        _combine_inproj_kernel,
        grid=(T // tm,),
        in_specs=in_specs + [pl.BlockSpec((D_MODEL, ZA_W + ZB_W + ZC_W), fixed)],
        out_specs=[pl.BlockSpec((tm, D_MODEL), row),
                   pl.BlockSpec((tm, ZA_W), row),
                   pl.BlockSpec((tm, ZB_W), row),
                   pl.BlockSpec((tm, ZC_W), row)],
        out_shape=[jax.ShapeDtypeStruct((T, D_MODEL), F32),
                   jax.ShapeDtypeStruct((T, ZA_W), BF16),
                   jax.ShapeDtypeStruct((T, ZB_W), BF16),
                   jax.ShapeDtypeStruct((T, ZC_W), BF16)],
        scratch_shapes=scratch,
        compiler_params=_cparams("arbitrary"),
        name="combine_inproj",
    )(pos, pos, x2d, route, ys, nw, w)


def _pad_value_heads(t, axis):
    shape = t.shape
    t = t.reshape(shape[:axis] + (B_HEADS, B_DV) + shape[axis + 1:])
    pad = [(0, 0)] * t.ndim
    pad[axis + 1] = (0, B_VSLOT - B_DV)
    return jnp.pad(t, pad).reshape(shape[:axis] + (B_VW,) + shape[axis + 1:])


def _layout_in_weights(w):
    D = w.shape[0]
    nqk, nv = B_HEADS * B_DK, B_HEADS * B_DV
    o = ZA_W

    def qk(block):
        t = block.reshape(D, B_HEADS, 2, B_HALF).transpose(0, 2, 1, 3)
        t = jnp.pad(t, ((0, 0), (0, 0), (0, 0), (0, B_QSLOT - B_HALF)))
        return t.reshape(D, B_QW)

    return jnp.concatenate([
        w[:, :o], qk(w[:, o:o + nqk]), qk(w[:, o + nqk:o + 2 * nqk]),
        _pad_value_heads(w[:, o + 2 * nqk:o + 2 * nqk + nv], 1),
        _pad_value_heads(w[:, o + 2 * nqk + nv:o + 2 * nqk + 2 * nv], 1),
        w[:, o + 2 * nqk + 2 * nv:]], axis=1)


def _layout_out_weights(w):
    nv = B_HEADS * B_DV
    return jnp.concatenate([w[:A_WIDTH], _pad_value_heads(w[A_WIDTH:A_WIDTH + nv], 0),
                            w[A_WIDTH + nv:]], axis=0)


def _retention_tables(S):
    log_gamma = jnp.log1p(-jnp.exp2(-5.0 - jnp.arange(B_HEADS, dtype=F32)))
    idx = jnp.arange(CHUNK, dtype=F32)
    rel = idx[:, None] - idx[None, :]
    decay = jnp.where(rel >= 0, jnp.exp(log_gamma[:, None, None] * jnp.maximum(rel, 0.0)), 0.0)
    dstack = decay.reshape(B_HEADS * CHUNK, CHUNK)
    k_decay = jnp.exp(log_gamma[:, None] * (CHUNK - 1.0 - idx)[None, :])
    q_decay = jnp.exp(log_gamma[:, None] * (idx + 1.0)[None, :])
    chunk_gamma = jnp.exp(log_gamma * CHUNK)

    def lanes(per_head):
        t = jnp.repeat(per_head[..., None], B_QSLOT, axis=-1)
        t = t.reshape(per_head.shape[:-1] + (B_HEADS * B_QSLOT,))
        return jnp.concatenate([t, t], axis=-1)

    qdec = lanes(q_decay.T)
    kdec = lanes(k_decay.T)
    cg = lanes(chunk_gamma[None, :])
    row_head = np.arange(B_VW) // B_VSLOT
    lane_head = (np.arange(B_QW) % (B_QW // 2)) // B_QSLOT
    bd = jnp.asarray((row_head[:, None] == lane_head[None, :]).astype(np.float32))

    half = B_HALF
    inv_freq = ROPE_BASE ** (-jnp.arange(half, dtype=F32) / half)
    ang = jnp.arange(S).astype(F32)[:, None] * inv_freq[None, :]
    pad = ((0, 0), (0, B_QSLOT - half))
    cos = jnp.tile(jnp.pad(jnp.cos(ang), pad), (1, B_HEADS))
    sin = jnp.tile(jnp.pad(jnp.sin(ang), pad), (1, B_HEADS))
    return cos, sin, (dstack, qdec, kdec, cg, bd)


def _attention_bias(rel_bias):
    qi = np.arange(C_QB)[:, None]
    km = np.arange(C_BWIN)[None, :]
    lag = (qi // CHUNK + N_PREV_CHUNKS) - km // CHUNK
    in_band = jnp.asarray((lag >= 0) & (lag <= N_PREV_CHUNKS))
    P = C_QB + C_BWIN
    t = np.arange(P)
    t = np.where(t < C_BWIN, t, t - P)
    dist = N_PREV_CHUNKS * CHUNK - t
    row = rel_bias.astype(F32)[:, np.clip(dist, -MAX_REL, MAX_REL) + MAX_REL]
    b = jnp.tile(row, (1, C_QB))[:, :C_QB * (P - 1)].reshape(C_HEADS, C_QB, P - 1)[:, :, :C_BWIN]
    b = jnp.where(in_band[None], b, MASK_VALUE)
    return b.reshape(C_HEADS // 2, 2 * C_QB, C_BWIN)


def _hgrn_tables():
    tri = jnp.asarray(np.tril(np.ones((CHUNK, CHUNK), np.float32))).astype(BF16)
    head = np.arange(A_WIDTH) // A_DK
    same = (head[:, None] == head[None, :]).astype(np.float32)
    return tri, jnp.asarray(same).astype(BF16), jnp.asarray(same)


def _moe_plan(route, T):
    A = 2 * T
    nblk = A // MOE_BM + N_EXPERTS
    eid = route[:, 0:2].astype(jnp.int32)
    flat_e = eid.reshape(A)
    onehot = (flat_e[:, None] == jnp.arange(N_EXPERTS, dtype=jnp.int32)[None, :]).astype(jnp.int32)
    csum = jnp.cumsum(onehot, axis=0)
    counts = csum[-1]
    rank = jnp.sum((csum - onehot) * onehot, axis=1)
    padded = (counts + MOE_BM - 1) // MOE_BM * MOE_BM
    pad_end = jnp.cumsum(padded)
    pad_start = pad_end - padded
    dest = jnp.sum(onehot * pad_start[None, :], axis=1) + rank
    nused = (pad_end[-1] // MOE_BM).astype(jnp.int32)
    blk_start = jnp.arange(nblk, dtype=jnp.int32) * MOE_BM
    blk_exp = jnp.sum((pad_end[None, :] <= blk_start[:, None]).astype(jnp.int32), axis=1)
    last = jnp.take(blk_exp, jnp.maximum(nused - 1, 0))
    blk_exp = jnp.where(jnp.arange(nblk) < nused, jnp.minimum(blk_exp, N_EXPERTS - 1), last)
    tm = TM_COMBINE
    pos = (dest * ROW_TILE).reshape(T // tm, tm, 2).transpose(0, 2, 1).reshape(T // tm, 1, 2 * tm)
    return pos, counts.astype(jnp.int32), pad_start.astype(jnp.int32), blk_exp, nused.reshape(1)


def kernel(x, w_in, w_out, norm_mix, norm_ffn, norm_final, hgrn_lb, hgrn_norm, ret_norm, rel_bias,
           router_group, router_expert, expert_w_gate, expert_w_up, expert_w_down):
    B, S, D = x.shape
    T = B * S
    depth = w_in.shape[0]

    p = jax.nn.softmax(hgrn_lb.astype(F32), axis=0)
    lower_bounds = jnp.clip(jnp.cumsum(p, axis=0) - p[0], 0.0, 1.0)

    cos, sin, ret_consts = _retention_tables(S)
    bias = _attention_bias(rel_bias)
    tri, eones, bd_a = _hgrn_tables()

    x2d = x.reshape(T, D).astype(F32)
    route = ys = pos = None
    out = None
    for l in range(depth):
        w_in_l = _layout_in_weights(w_in[l]).astype(BF16)
        w_out_l = _layout_out_weights(w_out[l]).astype(BF16)
        ret_nw = _pad_value_heads(ret_norm[l].astype(F32), 0).reshape(1, B_VW)
        nw_mix = norm_mix[l].astype(F32).reshape(1, D)
        if l == 0:
            za, zb, zc = _inproj(x2d, nw_mix, w_in_l)
        else:
            x2d, za, zb, zc = _combine(x2d, route, ys, pos, nw_mix, w_in_l)

        ya = _hgrn(za, lower_bounds[l].reshape(1, A_WIDTH), hgrn_norm[l].astype(F32).reshape(1, A_WIDTH),
                   tri, eones, bd_a, B, S)
        yb = _retention(zb, cos, sin, ret_consts, ret_nw, B, S)
        yc = _attention(zc, bias, B, S)

        wr = jnp.concatenate([router_group[l], router_expert[l]], axis=1).astype(F32)
        wr_hi = wr.astype(BF16)
        wr_lo = (wr - wr_hi.astype(F32)).astype(BF16)
        gap = jnp.zeros((D, ROUTE_W // 2 - wr.shape[1]), BF16)
        wr2 = jnp.concatenate([wr_hi, gap, wr_lo, gap], axis=1)
        nw_ffn = norm_ffn[l].astype(F32).reshape(1, D)
        x2d, route = _outproj(x2d, ya, yb, yc, w_out_l, nw_ffn, wr2)

        pos, counts, pad_start, blk_exp, nused = _moe_plan(route, T)
        xs = _dispatch(x2d, nw_ffn, pos, counts, pad_start, blk_exp.shape[0] * MOE_BM)
        ys = _experts(xs, blk_exp, nused, l, expert_w_gate, expert_w_up, expert_w_down)

    out = _combine(x2d, route, ys, pos, norm_final.astype(F32).reshape(1, D))
    return out.reshape(B, S, D).astype(x.dtype)
```

```python
import functools

import numpy as np
import jax
import jax.numpy as jnp
from jax import lax
from jax.experimental import pallas as pl
from jax.experimental.pallas import tpu as pltpu

F32 = jnp.float32
BF16 = jnp.bfloat16

D_MODEL = 1024
CHUNK = 64
NORM_EPS = 1e-6
MASK_VALUE = -1e30
MIN_FORGET = 1e-30

A_HEADS, A_DK, A_DV = 4, 64, 64
A_WIDTH = A_HEADS * A_DV
B_HEADS, B_DK, B_DV = 4, 48, 96
B_HALF = B_DK // 2
B_QSLOT = 32
B_VSLOT = 128
B_QW = 2 * B_HEADS * B_QSLOT
B_VW = B_HEADS * B_VSLOT
ROPE_BASE = 10000.0
C_HEADS, C_DH = 6, 64
C_WIDTH = C_HEADS * C_DH
N_PREV_CHUNKS = 8
MAX_REL = 128
C_QB = 2 * CHUNK
C_BWIN = C_QB + N_PREV_CHUNKS * CHUNK

ZA_W = 4 * A_WIDTH
ZB_W = 2 * B_QW + 2 * B_VW
ZC_W = 3 * C_WIDTH
Y_W = A_WIDTH + B_VW + C_WIDTH

N_GROUPS, EXPERTS_PER_GROUP = 4, 8
N_EXPERTS = N_GROUPS * EXPERTS_PER_GROUP
D_EXPERT = 512
ROUTE_W = 128

SUB = 8
VMEM_LIMIT = 56 * 1024 * 1024

TM = 512
TS_A = 256
TS_B = 256
TQ_C = 512
MOE_BM = 512
TM_COMBINE = 256
TM_DISPATCH = TM_COMBINE


def _cparams(*sem):
    return pltpu.CompilerParams(dimension_semantics=sem, vmem_limit_bytes=VMEM_LIMIT)


def _sigmoid(x):
    return 1.0 / (1.0 + jnp.exp(-x))


def _silu(x):
    return x * _sigmoid(x)


def _split3(x):
    hi = x.astype(BF16)
    r1 = x - hi.astype(F32)
    mid = r1.astype(BF16)
    lo = (r1 - mid.astype(F32)).astype(BF16)
    return hi, mid, lo


def _dot(a, b):
    return jnp.dot(a, b, preferred_element_type=F32)


def _dot_nt(a, b):
    return lax.dot_general(a, b, (((1,), (1,)), ((), ())), preferred_element_type=F32)


def _dot_tn(a, b):
    return lax.dot_general(a, b, (((0,), (0,)), ((), ())), preferred_element_type=F32)


LANES = 128
ROW_TILE = D_MODEL // LANES


def _store_row_tiles(ref, val):
    n = val.shape[0]
    for c in range(ROW_TILE):
        ref[pl.ds(c, n, stride=ROW_TILE), :] = val[:, c * LANES:(c + 1) * LANES]


def _load_row_tiles(ref, first_row, n):
    return jnp.concatenate(
        [ref[pl.ds(first_row * ROW_TILE + c, n, stride=ROW_TILE), :] for c in range(ROW_TILE)], axis=1)


def _rms(x, w):
    ms = jnp.mean(x * x, axis=-1, keepdims=True)
    return x * lax.rsqrt(ms + NORM_EPS) * w


def _project(h, w_ref, za_ref, zb_ref, zc_ref):
    hb = h.astype(BF16)
    za_ref[...] = _dot(hb, w_ref[:, 0:ZA_W]).astype(BF16)
    zb_ref[...] = _dot(hb, w_ref[:, ZA_W:ZA_W + ZB_W]).astype(BF16)
    zc_ref[...] = _dot(hb, w_ref[:, ZA_W + ZB_W:]).astype(BF16)


def _inproj_kernel(x_ref, nw_ref, w_ref, za_ref, zb_ref, zc_ref):
    _project(_rms(x_ref[...], nw_ref[...]), w_ref, za_ref, zb_ref, zc_ref)


def _inproj(x2d, nw, w):
    T = x2d.shape[0]
    row = lambda i: (i, 0)
    fixed = lambda i: (0, 0)
    return pl.pallas_call(
        _inproj_kernel,
        grid=(T // TM,),
        in_specs=[pl.BlockSpec((TM, D_MODEL), row),
                  pl.BlockSpec((1, D_MODEL), fixed),
                  pl.BlockSpec((D_MODEL, ZA_W + ZB_W + ZC_W), fixed)],
        out_specs=[pl.BlockSpec((TM, ZA_W), row),
                   pl.BlockSpec((TM, ZB_W), row),
                   pl.BlockSpec((TM, ZC_W), row)],
        out_shape=[jax.ShapeDtypeStruct((T, ZA_W), BF16),
                   jax.ShapeDtypeStruct((T, ZB_W), BF16),
                   jax.ShapeDtypeStruct((T, ZC_W), BF16)],
        compiler_params=_cparams("parallel"),
        name="inproj",
    )(x2d, nw, w)


def _hgrn_kernel(z_ref, lb_ref, nw_ref, tri_ref, eones_ref, bd_ref, o_ref, st_ref):
    @pl.when(pl.program_id(1) == 0)
    def _():
        st_ref[...] = jnp.zeros_like(st_ref)

    W = A_WIDTH
    lb = lb_ref[...]
    lane_head = lax.broadcasted_iota(jnp.int32, (1, W), 1) // A_DK
    head_masks = [lane_head == h for h in range(A_HEADS)]
    tri = tri_ref[...]
    eones = eones_ref[...]
    bd = bd_ref[...]
    ii = lax.broadcasted_iota(jnp.int32, (SUB, SUB, 1), 0)
    jj = lax.broadcasted_iota(jnp.int32, (SUB, SUB, 1), 1)
    causal = jj <= ii
    col = lax.broadcasted_iota(jnp.int32, (1, CHUNK), 1)

    n_chunks = z_ref.shape[0] // CHUNK
    n_sub = CHUNK // SUB
    outs = []
    for c in range(n_chunks):
        r0 = c * CHUNK
        zq = z_ref[r0:r0 + CHUNK, 0:W].astype(F32)
        zf = z_ref[r0:r0 + CHUNK, W:2 * W].astype(F32)
        vv = z_ref[r0:r0 + CHUNK, 2 * W:3 * W]
        vf = vv.astype(F32)
        f = lb + (1.0 - lb) * _sigmoid(zf)
        lf = jnp.log(jnp.maximum(f, MIN_FORGET))
        kk = (1.0 - lb) * _sigmoid(-zf)
        qf = _silu(zq) * (A_DK ** -0.5)
        hi, mid, lo = _split3(lf)
        G = _dot(tri, hi) + _dot(tri, mid) + _dot(tri, lo)
        g_last = G[CHUNK - 1:CHUNK, :]

        st = st_ref[...]
        inter = _dot_nt((qf * jnp.exp(G)).astype(BF16), st.astype(BF16))
        khat = kk * jnp.exp(g_last - G)
        ut = _dot_tn(vv, khat.astype(BF16))
        st_ref[...] = jnp.exp(g_last) * st + ut * bd

        p_rows = []
        for s in range(1, n_sub):
            b = G[s * SUB - 1:s * SUB, :]
            qt = qf[s * SUB:(s + 1) * SUB, :] * jnp.exp(G[s * SUB:(s + 1) * SUB, :] - b)
            kt = kk * jnp.exp(jnp.minimum(b - G, 0.0))
            lhs = jnp.concatenate([jnp.where(m, qt, 0.0) for m in head_masks], axis=0)
            p = _dot_nt(lhs.astype(BF16), kt.astype(BF16))
            p_rows.append(jnp.where(col < s * SUB, p, 0.0))
        pall = jnp.concatenate(p_rows, axis=0)
        pv = _dot(pall.astype(BF16), vv)

        pieces = []
        for s in range(n_sub):
            sl = slice(s * SUB, (s + 1) * SUB)
            gb, qb, kb, vb = G[sl], qf[sl], kk[sl], vf[sl]
            dec = jnp.exp(jnp.minimum(gb[:, None, :] - gb[None, :, :], 0.0))
            a = (qb[:, None, :] * kb[None, :, :]) * dec
            r = _dot(a.reshape(SUB * SUB, W).astype(BF16), eones).reshape(SUB, SUB, W)
            diag = jnp.sum(jnp.where(causal, r, 0.0) * vb[None, :, :], axis=1)
            piece = inter[sl] + diag
            if s > 0:
                base = (s - 1) * A_HEADS * SUB
                for h in range(A_HEADS):
                    blk = pv[base + h * SUB:base + (h + 1) * SUB, :]
                    piece = piece + jnp.where(head_masks[h], blk, 0.0)
            pieces.append(piece)
        outs.append(jnp.concatenate(pieces, axis=0))

    o = jnp.concatenate(outs, axis=0)
    g = z_ref[:, 3 * W:4 * W].astype(F32)
    o_ref[...] = (_rms(o, nw_ref[...]) * _silu(g)).astype(BF16)


def _hgrn(za, lb, nw, tri, eones, bd, B, S):
    nt = S // TS_A
    fixed = lambda b, t: (0, 0)
    return pl.pallas_call(
        _hgrn_kernel,
        grid=(B, nt),
        in_specs=[pl.BlockSpec((TS_A, ZA_W), lambda b, t: (b * nt + t, 0)),
                  pl.BlockSpec((1, A_WIDTH), fixed),
                  pl.BlockSpec((1, A_WIDTH), fixed),
                  pl.BlockSpec((CHUNK, CHUNK), fixed),
                  pl.BlockSpec((A_WIDTH, A_WIDTH), fixed),
                  pl.BlockSpec((A_WIDTH, A_WIDTH), fixed)],
        out_specs=pl.BlockSpec((TS_A, A_WIDTH), lambda b, t: (b * nt + t, 0)),
        out_shape=jax.ShapeDtypeStruct((B * S, A_WIDTH), BF16),
        scratch_shapes=[pltpu.VMEM((A_WIDTH, A_WIDTH), F32)],
        compiler_params=_cparams("parallel", "arbitrary"),
        name="hgrn2",
    )(za, lb, nw, tri, eones, bd)


def _ret_kernel(z_ref, cos_ref, sin_ref, dstack_ref, qdec_ref, kdec_ref, cg_ref, bd_ref, nw_ref,
                o_ref, st_ref):
    @pl.when(pl.program_id(1) == 0)
    def _():
        st_ref[...] = jnp.zeros_like(st_ref)

    H = B_QW // 2
    lane_head = (lax.broadcasted_iota(jnp.int32, (1, B_QW), 1) % H) // B_QSLOT
    head_masks = [lane_head == h for h in range(B_HEADS)]
    lane_real = lax.broadcasted_iota(jnp.int32, (1, B_VSLOT), 1) < B_DV
    dstack = dstack_ref[...]
    qdec, kdec = qdec_ref[...], kdec_ref[...]
    cg, bd = cg_ref[...], bd_ref[...]

    n_chunks = z_ref.shape[0] // CHUNK
    outs = []
    for c in range(n_chunks):
        rows = slice(c * CHUNK, (c + 1) * CHUNK)
        cos, sin = cos_ref[rows, :], sin_ref[rows, :]

        def rot(off):
            t1 = z_ref[rows, off:off + H].astype(F32)
            t2 = z_ref[rows, off + H:off + 2 * H].astype(F32)
            return jnp.concatenate([t1 * cos - t2 * sin, t1 * sin + t2 * cos], axis=1)

        qr = rot(0)
        kr = rot(B_QW) * (B_DK ** -0.5)
        vv = z_ref[rows, 2 * B_QW:2 * B_QW + B_VW]

        lhs = jnp.concatenate([jnp.where(m, qr, 0.0) for m in head_masks], axis=0)
        sc = _dot_nt(lhs.astype(BF16), kr.astype(BF16)) * dstack
        pv = _dot(sc.astype(BF16), vv)
        intra = jnp.concatenate(
            [pv[h * CHUNK:(h + 1) * CHUNK, h * B_VSLOT:(h + 1) * B_VSLOT] for h in range(B_HEADS)],
            axis=1)
        st = st_ref[...]
        inter = _dot_nt((qr * qdec).astype(BF16), st.astype(BF16))
        ut = _dot_tn(vv, (kr * kdec).astype(BF16))
        st_ref[...] = cg * st + ut * bd
        outs.append(intra + inter)

    o = jnp.concatenate(outs, axis=0)
    normed = []
    for h in range(B_HEADS):
        oh = o[:, h * B_VSLOT:(h + 1) * B_VSLOT]
        mu = jnp.sum(oh, axis=-1, keepdims=True) * (1.0 / B_DV)
        d = jnp.where(lane_real, oh - mu, 0.0)
        var = jnp.sum(d * d, axis=-1, keepdims=True) * (1.0 / B_DV)
        normed.append(d * lax.rsqrt(var + NORM_EPS))
    y = jnp.concatenate(normed, axis=1)
    g = z_ref[:, 2 * B_QW + B_VW:].astype(F32)
    o_ref[...] = (y * nw_ref[...] * _silu(g)).astype(BF16)


def _retention(zb, cos, sin, consts, nw, B, S):
    nt = S // TS_B
    fixed = lambda b, t: (0, 0)
    dstack, qdec, kdec, cg, bd = consts
    return pl.pallas_call(
        _ret_kernel,
        grid=(B, nt),
        in_specs=[pl.BlockSpec((TS_B, ZB_W), lambda b, t: (b * nt + t, 0)),
                  pl.BlockSpec((TS_B, B_QW // 2), lambda b, t: (t, 0)),
                  pl.BlockSpec((TS_B, B_QW // 2), lambda b, t: (t, 0)),
                  pl.BlockSpec(dstack.shape, fixed),
                  pl.BlockSpec(qdec.shape, fixed),
                  pl.BlockSpec(kdec.shape, fixed),
                  pl.BlockSpec(cg.shape, fixed),
                  pl.BlockSpec(bd.shape, fixed),
                  pl.BlockSpec((1, B_VW), fixed)],
        out_specs=pl.BlockSpec((TS_B, B_VW), lambda b, t: (b * nt + t, 0)),
        out_shape=jax.ShapeDtypeStruct((B * S, B_VW), BF16),
        scratch_shapes=[pltpu.VMEM((B_VW, B_QW), F32)],
        compiler_params=_cparams("parallel", "arbitrary"),
        name="retention",
    )(zb, cos, sin, dstack, qdec, kdec, cg, bd, nw)


def _attn_kernel(q_ref, kp_ref, kc_ref, vp_ref, vc_ref, bias_ref, o_ref, kwin_ref, vwin_ref):
    t = pl.program_id(1)
    kwin_ref[0:TQ_C, :] = kp_ref[...]
    kwin_ref[TQ_C:, :] = kc_ref[...]
    vwin_ref[0:TQ_C, :] = vp_ref[...]
    vwin_ref[TQ_C:, :] = vc_ref[...]
    jcol = lax.broadcasted_iota(jnp.int32, (1, C_BWIN), 1)
    first_head = lax.broadcasted_iota(jnp.int32, (1, 2 * C_DH), 1) < C_DH
    zero = jnp.zeros((), BF16)

    def block(i, carry):
        r0 = pl.multiple_of(i * C_QB, C_QB)
        pen = jnp.where((jcol + (r0 + (t - 1) * TQ_C)) >= 0, 0.0, MASK_VALUE)
        for p in range(C_HEADS // 2):
            lanes = slice(2 * p * C_DH, 2 * (p + 1) * C_DH)
            q = q_ref[pl.ds(r0, C_QB), lanes] * (C_DH ** -0.5)
            lhs = jnp.concatenate([jnp.where(first_head, q, zero), jnp.where(first_head, zero, q)], axis=0)
            k = kwin_ref[pl.ds(r0, C_BWIN), lanes]
            v = vwin_ref[pl.ds(r0, C_BWIN), lanes]
            s = _dot_nt(lhs, k) + bias_ref[p] + pen
            m = jnp.max(s, axis=-1, keepdims=True)
            e = jnp.exp(s - m)
            l = jnp.sum(e, axis=-1, keepdims=True)
            pv = _dot(e.astype(BF16), v) / l
            o = jnp.where(first_head, pv[:C_QB], pv[C_QB:])
            o_ref[pl.ds(r0, C_QB), lanes] = o.astype(BF16)
        return carry

    lax.fori_loop(0, TQ_C // C_QB, block, 0)


def _attention(zc, bias, B, S):
    nt = S // TQ_C
    cur = lambda col: (lambda b, t: (b * nt + t, col))
    prev = lambda col: (lambda b, t: (b * nt + jnp.maximum(t - 1, 0), col))
    blk = (TQ_C, C_WIDTH)
    return pl.pallas_call(
        _attn_kernel,
        grid=(B, nt),
        in_specs=[pl.BlockSpec(blk, cur(0)),
                  pl.BlockSpec(blk, prev(1)), pl.BlockSpec(blk, cur(1)),
                  pl.BlockSpec(blk, prev(2)), pl.BlockSpec(blk, cur(2)),
                  pl.BlockSpec(bias.shape, lambda b, t: (0, 0, 0))],
        out_specs=pl.BlockSpec(blk, cur(0)),
        out_shape=jax.ShapeDtypeStruct((B * S, C_WIDTH), BF16),
        scratch_shapes=[pltpu.VMEM((2 * TQ_C, C_WIDTH), BF16),
                        pltpu.VMEM((2 * TQ_C, C_WIDTH), BF16)],
        compiler_params=_cparams("parallel", "arbitrary"),
        name="chunk_attention",
    )(zc, zc, zc, zc, zc, bias)


def _route(logits):
    lane = lax.broadcasted_iota(jnp.int32, logits.shape, 1).astype(F32)
    big = float(1 << 20)
    neg = -jnp.inf

    def first_argmax(vals):
        m = jnp.max(vals, axis=-1, keepdims=True)
        idx = jnp.min(jnp.where(vals == m, lane, big), axis=-1, keepdims=True)
        return m, idx

    gl = jnp.where(lane < N_GROUPS, logits, neg)
    gm, grp = first_argmax(gl)
    p_grp = 1.0 / jnp.sum(jnp.exp(gl - gm), axis=-1, keepdims=True)
    lo = N_GROUPS + grp * EXPERTS_PER_GROUP
    el = jnp.where((lane >= lo) & (lane < lo + EXPERTS_PER_GROUP), logits, neg)
    v1, i1 = first_argmax(el)
    v2, i2 = first_argmax(jnp.where(lane == i1, neg, el))
    e2 = jnp.exp(v2 - v1)
    g1 = p_grp / (1.0 + e2)
    g2 = p_grp * e2 / (1.0 + e2)
    out = jnp.where(lane == 0, i1 - N_GROUPS, 0.0)
    out = jnp.where(lane == 1, i2 - N_GROUPS, out)
    out = jnp.where(lane == 2, g1, out)
    out = jnp.where(lane == 3, g2, out)
    return out


def _outproj_kernel(x_ref, ya_ref, yb_ref, yc_ref, w_ref, nw_ref, wr_ref, xo_ref, r_ref):
    a0, a1 = A_WIDTH, A_WIDTH + B_VW
    x = x_ref[...]
    x = x + _dot(ya_ref[...], w_ref[0:a0, :])
    x = x + _dot(yb_ref[...], w_ref[a0:a1, :])
    x = x + _dot(yc_ref[...], w_ref[a1:, :])
    xo_ref[...] = x
    h = _rms(x, nw_ref[...])
    h_hi = h.astype(BF16)
    h_lo = (h - h_hi.astype(F32)).astype(BF16)
    d_hi = _dot(h_hi, wr_ref[...])
    logits = d_hi + pltpu.roll(d_hi, ROUTE_W // 2, 1) + _dot(h_lo, wr_ref[...])
    r_ref[...] = _route(logits)


def _outproj(x2d, ya, yb, yc, w, nw, wr3):
    T = x2d.shape[0]
    row = lambda i: (i, 0)
    fixed = lambda i: (0, 0)
    return pl.pallas_call(
        _outproj_kernel,
        grid=(T // TM,),
        in_specs=[pl.BlockSpec((TM, D_MODEL), row),
                  pl.BlockSpec((TM, A_WIDTH), row),
                  pl.BlockSpec((TM, B_VW), row),
                  pl.BlockSpec((TM, C_WIDTH), row),
                  pl.BlockSpec((Y_W, D_MODEL), fixed),
                  pl.BlockSpec((1, D_MODEL), fixed),
                  pl.BlockSpec((D_MODEL, ROUTE_W), fixed)],
        out_specs=[pl.BlockSpec((TM, D_MODEL), row),
                   pl.BlockSpec((TM, ROUTE_W), row)],
        out_shape=[jax.ShapeDtypeStruct((T, D_MODEL), F32),
                   jax.ShapeDtypeStruct((T, ROUTE_W), F32)],
        compiler_params=_cparams("parallel"),
        name="outproj_router",
    )(x2d, ya, yb, yc, w, nw, wr3)


PAD_PIECES = tuple(MOE_BM >> (k + 1) for k in range(MOE_BM.bit_length() - 1))
DMA_GROUP = 8


def _tile_rows(ref, first_tile_row, n_rows=1):
    return ref.at[pl.ds(pl.multiple_of(first_tile_row, ROW_TILE), n_rows * ROW_TILE)]


def _rows_done(src_hbm, dst, sem, n_rows):
    pltpu.make_async_copy(_tile_rows(src_hbm, 0, n_rows), _tile_rows(dst, 0, n_rows), sem).wait()


def _dispatch_kernel(cnt_ref, pstart_ref, pos_ref, x_ref, nw_ref, xs_hbm, hbuf, zbuf, sem, zsem):
    i = pl.program_id(0)
    n = pl.num_programs(0)
    tm = TM_DISPATCH
    cur = lax.rem(i, 2)
    stage = hbuf.at[cur]
    _store_row_tiles(stage, _rms(x_ref[...], nw_ref[...]))

    def pad_pieces(e):
        cnt = cnt_ref[e]
        first = pstart_ref[e] + cnt
        n_pad = (-cnt) & (MOE_BM - 1)
        for p in PAD_PIECES:
            yield (n_pad & p) != 0, first + (n_pad & (MOE_BM - 2 * p)), p

    def spare_blocks():
        last = N_EXPERTS - 1
        n_used = (pstart_ref[last] + cnt_ref[last] + (MOE_BM - 1)) // MOE_BM
        n_blocks = xs_hbm.shape[0] // (MOE_BM * ROW_TILE)
        for b in range(n_blocks - N_EXPERTS, n_blocks):
            for first in range(0, MOE_BM, PAD_PIECES[0]):
                yield b >= n_used, b * MOE_BM + first, PAD_PIECES[0]

    @pl.when(i == 0)
    def _():
        zbuf[...] = jnp.zeros_like(zbuf)
        for phase in ("start", "wait"):
            for e in range(N_EXPERTS + 1):
                for fire, slot, p in (pad_pieces(e) if e < N_EXPERTS else spare_blocks()):
                    @pl.when(fire)
                    def _():
                        cp = pltpu.make_async_copy(_tile_rows(zbuf, 0, p),
                                                   _tile_rows(xs_hbm, slot * ROW_TILE, p), zsem)
                        cp.start() if phase == "start" else cp.wait()

    def body(g, carry):
        rows = [g * DMA_GROUP + k for k in range(DMA_GROUP)]
        slots = [(pos_ref[0, 0, r], pos_ref[0, 0, tm + r]) for r in rows]
        for r, (s0, s1) in zip(rows, slots):
            src = _tile_rows(stage, r * ROW_TILE)
            pltpu.make_async_copy(src, _tile_rows(xs_hbm, s0), sem.at[cur]).start(priority=0)
            pltpu.make_async_copy(src, _tile_rows(xs_hbm, s1), sem.at[cur]).start(priority=1)
        return carry
    lax.fori_loop(0, tm // DMA_GROUP, body, 0)

    @pl.when(i > 0)
    def _():
        _rows_done(stage, xs_hbm, sem.at[1 - cur], 2 * tm)

    @pl.when(i == n - 1)
    def _():
        _rows_done(stage, xs_hbm, sem.at[cur], 2 * tm)


def _dispatch(x2d, nw, pos, counts, pad_start, n_slots):
    T = x2d.shape[0]
    tm = TM_DISPATCH
    grid_spec = pltpu.PrefetchScalarGridSpec(
        num_scalar_prefetch=2,
        grid=(T // tm,),
        in_specs=[pl.BlockSpec((1, 1, 2 * tm), lambda i, c, s: (i, 0, 0), memory_space=pltpu.SMEM),
                  pl.BlockSpec((tm, D_MODEL), lambda i, c, s: (i, 0)),
                  pl.BlockSpec((1, D_MODEL), lambda i, c, s: (0, 0))],
        out_specs=pl.BlockSpec(memory_space=pl.ANY),
        scratch_shapes=[pltpu.VMEM((2, tm * ROW_TILE, LANES), F32),
                        pltpu.VMEM((PAD_PIECES[0] * ROW_TILE, LANES), F32),
                        pltpu.SemaphoreType.DMA((2,)), pltpu.SemaphoreType.DMA(())],
    )
    return pl.pallas_call(
        _dispatch_kernel,
        grid_spec=grid_spec,
        out_shape=jax.ShapeDtypeStruct((n_slots * ROW_TILE, LANES), F32),
        compiler_params=_cparams("arbitrary"),
        name="dispatch",
    )(counts, pad_start, pos, x2d, nw)


def _expert_kernel(blk_exp_ref, nused_ref, x_ref, wg_ref, wu_ref, wd_ref, y_ref, wg_bf, wu_bf, wd_bf):
    i = pl.program_id(0)
    used = i < nused_ref[0]

    @pl.when((i == 0) | (blk_exp_ref[i] != blk_exp_ref[jnp.maximum(i - 1, 0)]))
    def _():
        wg_bf[...] = wg_ref[0, 0].astype(BF16)
        wu_bf[...] = wu_ref[0, 0].astype(BF16)
        wd_bf[...] = wd_ref[0, 0].astype(BF16)

    @pl.when(used)
    def _():
        xb = _load_row_tiles(x_ref, 0, MOE_BM).astype(BF16)
        act = _silu(_dot(xb, wg_bf[...])) * _dot(xb, wu_bf[...])
        _store_row_tiles(y_ref, _dot(act.astype(BF16), wd_bf[...]))

    @pl.when(jnp.logical_not(used))
    def _():
        y_ref[...] = jnp.zeros_like(y_ref)


def _experts(xs, blk_exp, nused, layer, w_gate, w_up, w_down):
    nblk = blk_exp.shape[0]
    last_used = lambda i, nu: jnp.minimum(i, nu[0] - 1)
    expert = lambda i, be, nu: (layer, be[i], 0, 0)
    grid_spec = pltpu.PrefetchScalarGridSpec(
        num_scalar_prefetch=2,
        grid=(nblk,),
        in_specs=[pl.BlockSpec((MOE_BM * ROW_TILE, LANES), lambda i, be, nu: (last_used(i, nu), 0)),
                  pl.BlockSpec((1, 1, D_MODEL, D_EXPERT), expert),
                  pl.BlockSpec((1, 1, D_MODEL, D_EXPERT), expert),
                  pl.BlockSpec((1, 1, D_EXPERT, D_MODEL), expert)],
        out_specs=pl.BlockSpec((MOE_BM * ROW_TILE, LANES), lambda i, be, nu: (i, 0)),
        scratch_shapes=[pltpu.VMEM((D_MODEL, D_EXPERT), BF16), pltpu.VMEM((D_MODEL, D_EXPERT), BF16),
                        pltpu.VMEM((D_EXPERT, D_MODEL), BF16)],
    )
    return pl.pallas_call(
        _expert_kernel,
        grid_spec=grid_spec,
        out_shape=jax.ShapeDtypeStruct((nblk * MOE_BM * ROW_TILE, LANES), F32),
        compiler_params=_cparams("arbitrary"),
        name="experts",
    )(blk_exp, nused, xs, w_gate, w_up, w_down)


def _combine_rows(pos_ref, pos_next_ref, x_ref, r_ref, y_hbm, ybuf, sem, request_next_first):
    i = pl.program_id(0)
    n_rows = 2 * TM_COMBINE
    slot = lax.rem(i, 2)

    @pl.when(i == 0)
    def _():
        _request_rows(pos_ref, y_hbm, ybuf, sem, 0, unrolled=False)

    if request_next_first:
        _request_next(pos_next_ref, y_hbm, ybuf, sem, unrolled=False)
    _rows_done(y_hbm, ybuf.at[slot], sem.at[slot], n_rows)
    r = r_ref[...]
    g0, g1 = r[:, 2:3], r[:, 3:4]
    y0 = _load_row_tiles(ybuf.at[slot], 0, TM_COMBINE)
    y1 = _load_row_tiles(ybuf.at[slot], TM_COMBINE, TM_COMBINE)
    return x_ref[...] + (g0 * y0 + g1 * y1)


def _request_rows(idx_ref, y_hbm, ybuf, sem, s, unrolled):
    n_rows = 2 * TM_COMBINE
    group = 2 * DMA_GROUP

    def body(g, carry):
        rows = [g * group + k for k in range(group)]
        slots = [idx_ref[0, 0, r] for r in rows]
        for k, src_row in enumerate(slots):
            pltpu.make_async_copy(_tile_rows(y_hbm, src_row),
                                  _tile_rows(ybuf.at[s], rows[k] * ROW_TILE), sem.at[s]).start(priority=k % 2)
        return carry

    if unrolled:
        for g in range(n_rows // group):
            body(g, 0)
    else:
        lax.fori_loop(0, n_rows // group, body, 0)


def _request_next(pos_next_ref, y_hbm, ybuf, sem, unrolled):
    _request_rows(pos_next_ref, y_hbm, ybuf, sem, 1 - lax.rem(pl.program_id(0), 2), unrolled)


def _drain_last(y_hbm, ybuf, sem):
    i = pl.program_id(0)

    @pl.when(i == pl.num_programs(0) - 1)
    def _():
        other = 1 - lax.rem(i, 2)
        _rows_done(y_hbm, ybuf.at[other], sem.at[other], 2 * TM_COMBINE)


def _combine_inproj_kernel(pos_ref, pos_next_ref, x_ref, r_ref, y_hbm, nw_ref, w_ref,
                           xo_ref, za_ref, zb_ref, zc_ref, ybuf, sem):
    x = _combine_rows(pos_ref, pos_next_ref, x_ref, r_ref, y_hbm, ybuf, sem, request_next_first=False)
    xo_ref[...] = x
    _request_next(pos_next_ref, y_hbm, ybuf, sem, unrolled=True)
    _project(_rms(x, nw_ref[...]), w_ref, za_ref, zb_ref, zc_ref)
    _drain_last(y_hbm, ybuf, sem)


def _combine_final_kernel(pos_ref, pos_next_ref, x_ref, r_ref, y_hbm, nw_ref, o_ref, ybuf, sem):
    x = _combine_rows(pos_ref, pos_next_ref, x_ref, r_ref, y_hbm, ybuf, sem, request_next_first=True)
    o_ref[...] = _rms(x, nw_ref[...])
    _drain_last(y_hbm, ybuf, sem)


def _combine(x2d, route, ys, pos, nw, w=None):
    T = x2d.shape[0]
    tm = TM_COMBINE
    row = lambda i: (i, 0)
    fixed = lambda i: (0, 0)
    n_tiles = T // tm
    in_specs = [pl.BlockSpec((1, 1, 2 * tm), lambda i: (i, 0, 0), memory_space=pltpu.SMEM),
                pl.BlockSpec((1, 1, 2 * tm), lambda i: (jnp.minimum(i + 1, n_tiles - 1), 0, 0),
                             memory_space=pltpu.SMEM),
                pl.BlockSpec((tm, D_MODEL), row),
                pl.BlockSpec((tm, ROUTE_W), row),
                pl.BlockSpec(memory_space=pl.ANY),
                pl.BlockSpec((1, D_MODEL), fixed)]
    scratch = [pltpu.VMEM((2, 2 * tm * ROW_TILE, LANES), F32), pltpu.SemaphoreType.DMA((2,))]
    if w is None:
        return pl.pallas_call(
            _combine_final_kernel,
            grid=(T // tm,),
            in_specs=in_specs,
            out_specs=pl.BlockSpec((tm, D_MODEL), row),
            out_shape=jax.ShapeDtypeStruct((T, D_MODEL), F32),
            scratch_shapes=scratch,
            compiler_params=_cparams("arbitrary"),
            name="combine_final",
        )(pos, pos, x2d, route, ys, nw)
    return pl.pallas_call(
        _combine_inproj_kernel,
        grid=(T // tm,),
        in_specs=in_specs + [pl.BlockSpec((D_MODEL, ZA_W + ZB_W + ZC_W), fixed)],
        out_specs=[pl.BlockSpec((tm, D_MODEL), row),
                   pl.BlockSpec((tm, ZA_W), row),
                   pl.BlockSpec((tm, ZB_W), row),
                   pl.BlockSpec((tm, ZC_W), row)],
        out_shape=[jax.ShapeDtypeStruct((T, D_MODEL), F32),
                   jax.ShapeDtypeStruct((T, ZA_W), BF16),
                   jax.ShapeDtypeStruct((T, ZB_W), BF16),
                   jax.ShapeDtypeStruct((T, ZC_W), BF16)],
        scratch_shapes=scratch,
        compiler_params=_cparams("arbitrary"),
        name="combine_inproj",
    )(pos, pos, x2d, route, ys, nw, w)


def _pad_value_heads(t, axis):
    shape = t.shape
    t = t.reshape(shape[:axis] + (B_HEADS, B_DV) + shape[axis + 1:])
    pad = [(0, 0)] * t.ndim
    pad[axis + 1] = (0, B_VSLOT - B_DV)
    return jnp.pad(t, pad).reshape(shape[:axis] + (B_VW,) + shape[axis + 1:])


def _layout_in_weights(w):
    D = w.shape[0]
    nqk, nv = B_HEADS * B_DK, B_HEADS * B_DV
    o = ZA_W

    def qk(block):
        t = block.reshape(D, B_HEADS, 2, B_HALF).transpose(0, 2, 1, 3)
        t = jnp.pad(t, ((0, 0), (0, 0), (0, 0), (0, B_QSLOT - B_HALF)))
        return t.reshape(D, B_QW)

    return jnp.concatenate([
        w[:, :o], qk(w[:, o:o + nqk]), qk(w[:, o + nqk:o + 2 * nqk]),
        _pad_value_heads(w[:, o + 2 * nqk:o + 2 * nqk + nv], 1),
        _pad_value_heads(w[:, o + 2 * nqk + nv:o + 2 * nqk + 2 * nv], 1),
        w[:, o + 2 * nqk + 2 * nv:]], axis=1)


def _layout_out_weights(w):
    nv = B_HEADS * B_DV
    return jnp.concatenate([w[:A_WIDTH], _pad_value_heads(w[A_WIDTH:A_WIDTH + nv], 0),
                            w[A_WIDTH + nv:]], axis=0)


def _retention_tables(S):
    log_gamma = jnp.log1p(-jnp.exp2(-5.0 - jnp.arange(B_HEADS, dtype=F32)))
    idx = jnp.arange(CHUNK, dtype=F32)
    rel = idx[:, None] - idx[None, :]
    decay = jnp.where(rel >= 0, jnp.exp(log_gamma[:, None, None] * jnp.maximum(rel, 0.0)), 0.0)
    dstack = decay.reshape(B_HEADS * CHUNK, CHUNK)
    k_decay = jnp.exp(log_gamma[:, None] * (CHUNK - 1.0 - idx)[None, :])
    q_decay = jnp.exp(log_gamma[:, None] * (idx + 1.0)[None, :])
    chunk_gamma = jnp.exp(log_gamma * CHUNK)

    def lanes(per_head):
        t = jnp.repeat(per_head[..., None], B_QSLOT, axis=-1)
        t = t.reshape(per_head.shape[:-1] + (B_HEADS * B_QSLOT,))
        return jnp.concatenate([t, t], axis=-1)

    qdec = lanes(q_decay.T)
    kdec = lanes(k_decay.T)
    cg = lanes(chunk_gamma[None, :])
    row_head = np.arange(B_VW) // B_VSLOT
    lane_head = (np.arange(B_QW) % (B_QW // 2)) // B_QSLOT
    bd = jnp.asarray((row_head[:, None] == lane_head[None, :]).astype(np.float32))

    half = B_HALF
    inv_freq = ROPE_BASE ** (-jnp.arange(half, dtype=F32) / half)
    ang = jnp.arange(S).astype(F32)[:, None] * inv_freq[None, :]
    pad = ((0, 0), (0, B_QSLOT - half))
    cos = jnp.tile(jnp.pad(jnp.cos(ang), pad), (1, B_HEADS))
    sin = jnp.tile(jnp.pad(jnp.sin(ang), pad), (1, B_HEADS))
    return cos, sin, (dstack, qdec, kdec, cg, bd)


def _attention_bias(rel_bias):
    qi = np.arange(C_QB)[:, None]
    km = np.arange(C_BWIN)[None, :]
    lag = (qi // CHUNK + N_PREV_CHUNKS) - km // CHUNK
    in_band = jnp.asarray((lag >= 0) & (lag <= N_PREV_CHUNKS))
    P = C_QB + C_BWIN
    t = np.arange(P)
    t = np.where(t < C_BWIN, t, t - P)
    dist = N_PREV_CHUNKS * CHUNK - t
    row = rel_bias.astype(F32)[:, np.clip(dist, -MAX_REL, MAX_REL) + MAX_REL]
    b = jnp.tile(row, (1, C_QB))[:, :C_QB * (P - 1)].reshape(C_HEADS, C_QB, P - 1)[:, :, :C_BWIN]
    b = jnp.where(in_band[None], b, MASK_VALUE)
    return b.reshape(C_HEADS // 2, 2 * C_QB, C_BWIN)


def _hgrn_tables():
    tri = jnp.asarray(np.tril(np.ones((CHUNK, CHUNK), np.float32))).astype(BF16)
    head = np.arange(A_WIDTH) // A_DK
    same = (head[:, None] == head[None, :]).astype(np.float32)
    return tri, jnp.asarray(same).astype(BF16), jnp.asarray(same)


def _moe_plan(route, T):
    A = 2 * T
    nblk = A // MOE_BM + N_EXPERTS
    eid = route[:, 0:2].astype(jnp.int32)
    flat_e = eid.reshape(A)
    onehot = (flat_e[:, None] == jnp.arange(N_EXPERTS, dtype=jnp.int32)[None, :]).astype(jnp.int32)
    csum = jnp.cumsum(onehot, axis=0)
    counts = csum[-1]
    rank = jnp.sum((csum - onehot) * onehot, axis=1)
    padded = (counts + MOE_BM - 1) // MOE_BM * MOE_BM
    pad_end = jnp.cumsum(padded)
    pad_start = pad_end - padded
    dest = jnp.sum(onehot * pad_start[None, :], axis=1) + rank
    nused = (pad_end[-1] // MOE_BM).astype(jnp.int32)
    blk_start = jnp.arange(nblk, dtype=jnp.int32) * MOE_BM
    blk_exp = jnp.sum((pad_end[None, :] <= blk_start[:, None]).astype(jnp.int32), axis=1)
    last = jnp.take(blk_exp, jnp.maximum(nused - 1, 0))
    blk_exp = jnp.where(jnp.arange(nblk) < nused, jnp.minimum(blk_exp, N_EXPERTS - 1), last)
    tm = TM_COMBINE
    pos = (dest * ROW_TILE).reshape(T // tm, tm, 2).transpose(0, 2, 1).reshape(T // tm, 1, 2 * tm)
    return pos, counts.astype(jnp.int32), pad_start.astype(jnp.int32), blk_exp, nused.reshape(1)


def kernel(x, w_in, w_out, norm_mix, norm_ffn, norm_final, hgrn_lb, hgrn_norm, ret_norm, rel_bias,
           router_group, router_expert, expert_w_gate, expert_w_up, expert_w_down):
    B, S, D = x.shape
    T = B * S
    depth = w_in.shape[0]

    p = jax.nn.softmax(hgrn_lb.astype(F32), axis=0)
    lower_bounds = jnp.clip(jnp.cumsum(p, axis=0) - p[0], 0.0, 1.0)

    cos, sin, ret_consts = _retention_tables(S)
    bias = _attention_bias(rel_bias)
    tri, eones, bd_a = _hgrn_tables()

    x2d = x.reshape(T, D).astype(F32)
    route = ys = pos = None
    out = None
    for l in range(depth):
        w_in_l = _layout_in_weights(w_in[l]).astype(BF16)
        w_out_l = _layout_out_weights(w_out[l]).astype(BF16)
        ret_nw = _pad_value_heads(ret_norm[l].astype(F32), 0).reshape(1, B_VW)
        nw_mix = norm_mix[l].astype(F32).reshape(1, D)
        if l == 0:
            za, zb, zc = _inproj(x2d, nw_mix, w_in_l)
        else:
            x2d, za, zb, zc = _combine(x2d, route, ys, pos, nw_mix, w_in_l)

        ya = _hgrn(za, lower_bounds[l].reshape(1, A_WIDTH), hgrn_norm[l].astype(F32).reshape(1, A_WIDTH),
                   tri, eones, bd_a, B, S)
        yb = _retention(zb, cos, sin, ret_consts, ret_nw, B, S)
        yc = _attention(zc, bias, B, S)

        wr = jnp.concatenate([router_group[l], router_expert[l]], axis=1).astype(F32)
        wr_hi = wr.astype(BF16)
        wr_lo = (wr - wr_hi.astype(F32)).astype(BF16)
        gap = jnp.zeros((D, ROUTE_W // 2 - wr.shape[1]), BF16)
        wr2 = jnp.concatenate([wr_hi, gap, wr_lo, gap], axis=1)
        nw_ffn = norm_ffn[l].astype(F32).reshape(1, D)
        x2d, route = _outproj(x2d, ya, yb, yc, w_out_l, nw_ffn, wr2)

        pos, counts, pad_start, blk_exp, nused = _moe_plan(route, T)
        xs = _dispatch(x2d, nw_ffn, pos, counts, pad_start, blk_exp.shape[0] * MOE_BM)
        ys = _experts(xs, blk_exp, nused, l, expert_w_gate, expert_w_up, expert_w_down)

    out = _combine(x2d, route, ys, pos, norm_final.astype(F32).reshape(1, D))
    return out.reshape(B, S, D).astype(x.dtype)
```

```python
import functools

import numpy as np
import jax
import jax.numpy as jnp
from jax import lax
from jax.experimental import pallas as pl
from jax.experimental.pallas import tpu as pltpu

F32 = jnp.float32
BF16 = jnp.bfloat16

D_MODEL = 1024
CHUNK = 64
NORM_EPS = 1e-6
MASK_VALUE = -1e30
MIN_FORGET = 1e-30
LOG2_E = 1.4426950408889634

A_HEADS, A_DK, A_DV = 4, 64, 64
A_WIDTH = A_HEADS * A_DV
B_HEADS, B_DK, B_DV = 4, 48, 96
B_HALF = B_DK // 2
B_QSLOT = 32
B_VSLOT = 128
B_QW = 2 * B_HEADS * B_QSLOT
B_VW = B_HEADS * B_VSLOT
ROPE_BASE = 10000.0
C_HEADS, C_DH = 6, 64
C_WIDTH = C_HEADS * C_DH
N_PREV_CHUNKS = 8
MAX_REL = 128
C_QB = 2 * CHUNK
C_BWIN = C_QB + N_PREV_CHUNKS * CHUNK
C_KT = 128

ZA_W = 4 * A_WIDTH
ZB_W = 2 * B_QW + 2 * B_VW
ZC_W = 3 * C_WIDTH
Y_W = A_WIDTH + B_VW + C_WIDTH

N_GROUPS, EXPERTS_PER_GROUP = 4, 8
N_EXPERTS = N_GROUPS * EXPERTS_PER_GROUP
D_EXPERT = 512
ROUTE_W = 128

SUB = 8
VMEM_LIMIT = 56 * 1024 * 1024

TM = 512
TS_A = 256
TS_B = 256
TQ_C = 512
MOE_BM = 512
TM_COMBINE = 256
TM_DISPATCH = TM_COMBINE


def _cparams(*sem):
    return pltpu.CompilerParams(dimension_semantics=sem, vmem_limit_bytes=VMEM_LIMIT)


def _sigmoid(x):
    return 1.0 / (1.0 + jnp.exp(-x))


def _silu(x):
    return x * _sigmoid(x)


def _split3(x):
    hi = x.astype(BF16)
    r1 = x - hi.astype(F32)
    mid = r1.astype(BF16)
    lo = (r1 - mid.astype(F32)).astype(BF16)
    return hi, mid, lo


def _dot(a, b):
    return jnp.dot(a, b, preferred_element_type=F32)


def _dot_nt(a, b):
    return lax.dot_general(a, b, (((1,), (1,)), ((), ())), preferred_element_type=F32)


def _dot_tn(a, b):
    return lax.dot_general(a, b, (((0,), (0,)), ((), ())), preferred_element_type=F32)


LANES = 128
ROW_TILE = D_MODEL // LANES


def _store_row_tiles(ref, val):
    n = val.shape[0]
    for c in range(ROW_TILE):
        ref[pl.ds(c, n, stride=ROW_TILE), :] = val[:, c * LANES:(c + 1) * LANES]


def _load_row_tiles(ref, first_row, n):
    return jnp.concatenate(
        [ref[pl.ds(first_row * ROW_TILE + c, n, stride=ROW_TILE), :] for c in range(ROW_TILE)], axis=1)


def _rms(x, w):
    ms = jnp.mean(x * x, axis=-1, keepdims=True)
    return x * lax.rsqrt(ms + NORM_EPS) * w


def _project(h, w_ref, za_ref, zb_ref, zc_ref):
    hb = h.astype(BF16)
    za_ref[...] = _dot(hb, w_ref[:, 0:ZA_W]).astype(BF16)
    zb_ref[...] = _dot(hb, w_ref[:, ZA_W:ZA_W + ZB_W]).astype(BF16)
    zc_ref[...] = _dot(hb, w_ref[:, ZA_W + ZB_W:]).astype(BF16)


def _inproj_kernel(x_ref, nw_ref, w_ref, za_ref, zb_ref, zc_ref):
    _project(_rms(x_ref[...], nw_ref[...]), w_ref, za_ref, zb_ref, zc_ref)


def _inproj(x2d, nw, w):
    T = x2d.shape[0]
    row = lambda i: (i, 0)
    fixed = lambda i: (0, 0)
    return pl.pallas_call(
        _inproj_kernel,
        grid=(T // TM,),
        in_specs=[pl.BlockSpec((TM, D_MODEL), row),
                  pl.BlockSpec((1, D_MODEL), fixed),
                  pl.BlockSpec((D_MODEL, ZA_W + ZB_W + ZC_W), fixed)],
        out_specs=[pl.BlockSpec((TM, ZA_W), row),
                   pl.BlockSpec((TM, ZB_W), row),
                   pl.BlockSpec((TM, ZC_W), row)],
        out_shape=[jax.ShapeDtypeStruct((T, ZA_W), BF16),
                   jax.ShapeDtypeStruct((T, ZB_W), BF16),
                   jax.ShapeDtypeStruct((T, ZC_W), BF16)],
        compiler_params=_cparams("parallel"),
        name="inproj",
    )(x2d, nw, w)


def _hgrn_kernel(z_ref, lb_ref, nw_ref, tri_ref, eones_ref, bd_ref, o_ref, st_ref):
    @pl.when(pl.program_id(1) == 0)
    def _():
        st_ref[...] = jnp.zeros_like(st_ref)

    W = A_WIDTH
    lb = lb_ref[...]
    lane_head = lax.broadcasted_iota(jnp.int32, (1, W), 1) // A_DK
    head_masks = [lane_head == h for h in range(A_HEADS)]
    tri = tri_ref[...]
    eones = eones_ref[...]
    bd = bd_ref[...]
    jj = lax.broadcasted_iota(jnp.int32, (SUB, SUB, 1), 0)
    ii = lax.broadcasted_iota(jnp.int32, (SUB, SUB, 1), 1)
    causal = jj <= ii
    col = lax.broadcasted_iota(jnp.int32, (1, CHUNK), 1)

    n_chunks = z_ref.shape[0] // CHUNK
    n_sub = CHUNK // SUB
    outs = []
    for c in range(n_chunks):
        r0 = c * CHUNK
        zq = z_ref[r0:r0 + CHUNK, 0:W].astype(F32)
        zf = z_ref[r0:r0 + CHUNK, W:2 * W].astype(F32)
        vv = z_ref[r0:r0 + CHUNK, 2 * W:3 * W]
        vf = vv.astype(F32)
        f = lb + (1.0 - lb) * _sigmoid(zf)
        lf = jnp.log(jnp.maximum(f, MIN_FORGET))
        kk = (1.0 - lb) * _sigmoid(-zf)
        qf = _silu(zq) * (A_DK ** -0.5)
        hi, mid, lo = _split3(lf)
        G = (_dot(tri, hi) + _dot(tri, mid) + _dot(tri, lo)) * LOG2_E
        g_last = G[CHUNK - 1:CHUNK, :]

        st = st_ref[...]
        inter = _dot_nt((qf * jnp.exp2(G)).astype(BF16), st.astype(BF16))
        khat = kk * jnp.exp2(g_last - G)
        ut = _dot_tn(vv, khat.astype(BF16))
        st_ref[...] = jnp.exp2(g_last) * st + ut * bd

        p_rows = []
        for s in range(1, n_sub):
            b = G[s * SUB - 1:s * SUB, :]
            qt = qf[s * SUB:(s + 1) * SUB, :] * jnp.exp2(G[s * SUB:(s + 1) * SUB, :] - b)
            kt = kk * jnp.exp2(jnp.minimum(b - G, 0.0))
            lhs = jnp.concatenate([jnp.where(m, qt, 0.0) for m in head_masks], axis=0)
            p = _dot_nt(lhs.astype(BF16), kt.astype(BF16))
            p_rows.append(jnp.where(col < s * SUB, p, 0.0))
        pall = jnp.concatenate(p_rows, axis=0)
        pv = _dot(pall.astype(BF16), vv)

        pieces = []
        for s in range(n_sub):
            sl = slice(s * SUB, (s + 1) * SUB)
            gb, qb, kb, vb = G[sl], qf[sl], kk[sl], vf[sl]
            dec = jnp.exp2(jnp.minimum(gb[None, :, :] - gb[:, None, :], 0.0))
            a = (qb[None, :, :] * kb[:, None, :]) * dec
            r = _dot(a.reshape(SUB * SUB, W).astype(BF16), eones).reshape(SUB, SUB, W)
            diag = jnp.sum(jnp.where(causal, r, 0.0) * vb[:, None, :], axis=0)
            piece = inter[sl] + diag
            if s > 0:
                base = (s - 1) * A_HEADS * SUB
                for h in range(A_HEADS):
                    blk = pv[base + h * SUB:base + (h + 1) * SUB, :]
                    piece = piece + jnp.where(head_masks[h], blk, 0.0)
            pieces.append(piece)
        outs.append(jnp.concatenate(pieces, axis=0))

    o = jnp.concatenate(outs, axis=0)
    g = z_ref[:, 3 * W:4 * W].astype(F32)
    o_ref[...] = (_rms(o, nw_ref[...]) * _silu(g)).astype(BF16)


def _hgrn(za, lb, nw, tri, eones, bd, B, S):
    nt = S // TS_A
    fixed = lambda b, t: (0, 0)
    return pl.pallas_call(
        _hgrn_kernel,
        grid=(B, nt),
        in_specs=[pl.BlockSpec((TS_A, ZA_W), lambda b, t: (b * nt + t, 0)),
                  pl.BlockSpec((1, A_WIDTH), fixed),
                  pl.BlockSpec((1, A_WIDTH), fixed),
                  pl.BlockSpec((CHUNK, CHUNK), fixed),
                  pl.BlockSpec((A_WIDTH, A_WIDTH), fixed),
                  pl.BlockSpec((A_WIDTH, A_WIDTH), fixed)],
        out_specs=pl.BlockSpec((TS_A, A_WIDTH), lambda b, t: (b * nt + t, 0)),
        out_shape=jax.ShapeDtypeStruct((B * S, A_WIDTH), BF16),
        scratch_shapes=[pltpu.VMEM((A_WIDTH, A_WIDTH), F32)],
        compiler_params=_cparams("parallel", "arbitrary"),
        name="hgrn2",
    )(za, lb, nw, tri, eones, bd)


def _ret_kernel(z_ref, cos_ref, sin_ref, dstack_ref, qdec_ref, kdec_ref, cg_ref, bd_ref, nw_ref,
                o_ref, st_ref):
    @pl.when(pl.program_id(1) == 0)
    def _():
        st_ref[...] = jnp.zeros_like(st_ref)

    H = B_QW // 2
    lane_head = (lax.broadcasted_iota(jnp.int32, (1, B_QW), 1) % H) // B_QSLOT
    head_masks = [lane_head == h for h in range(B_HEADS)]
    lane_real = lax.broadcasted_iota(jnp.int32, (1, B_VSLOT), 1) < B_DV
    dstack = dstack_ref[...]
    qdec, kdec = qdec_ref[...], kdec_ref[...]
    cg, bd = cg_ref[...], bd_ref[...]

    n_chunks = z_ref.shape[0] // CHUNK
    outs = []
    for c in range(n_chunks):
        rows = slice(c * CHUNK, (c + 1) * CHUNK)
        cos, sin = cos_ref[rows, :], sin_ref[rows, :]

        def rot(off):
            t1 = z_ref[rows, off:off + H].astype(F32)
            t2 = z_ref[rows, off + H:off + 2 * H].astype(F32)
            return jnp.concatenate([t1 * cos - t2 * sin, t1 * sin + t2 * cos], axis=1)

        qr = rot(0)
        kr = rot(B_QW) * (B_DK ** -0.5)
        vv = z_ref[rows, 2 * B_QW:2 * B_QW + B_VW]

        lhs = jnp.concatenate([jnp.where(m, qr, 0.0) for m in head_masks], axis=0)
        sc = _dot_nt(lhs.astype(BF16), kr.astype(BF16)) * dstack
        pv = _dot(sc.astype(BF16), vv)
        intra = jnp.concatenate(
            [pv[h * CHUNK:(h + 1) * CHUNK, h * B_VSLOT:(h + 1) * B_VSLOT] for h in range(B_HEADS)],
            axis=1)
        st = st_ref[...]
        inter = _dot_nt((qr * qdec).astype(BF16), st.astype(BF16))
        ut = _dot_tn(vv, (kr * kdec).astype(BF16))
        st_ref[...] = cg * st + ut * bd
        outs.append(intra + inter)

    o = jnp.concatenate(outs, axis=0)
    normed = []
    for h in range(B_HEADS):
        oh = o[:, h * B_VSLOT:(h + 1) * B_VSLOT]
        mu = jnp.sum(oh, axis=-1, keepdims=True) * (1.0 / B_DV)
        d = jnp.where(lane_real, oh - mu, 0.0)
        var = jnp.sum(d * d, axis=-1, keepdims=True) * (1.0 / B_DV)
        normed.append(d * lax.rsqrt(var + NORM_EPS))
    y = jnp.concatenate(normed, axis=1)
    g = z_ref[:, 2 * B_QW + B_VW:].astype(F32)
    o_ref[...] = (y * nw_ref[...] * _silu(g)).astype(BF16)


def _retention(zb, cos, sin, consts, nw, B, S):
    nt = S // TS_B
    fixed = lambda b, t: (0, 0)
    dstack, qdec, kdec, cg, bd = consts
    return pl.pallas_call(
        _ret_kernel,
        grid=(B, nt),
        in_specs=[pl.BlockSpec((TS_B, ZB_W), lambda b, t: (b * nt + t, 0)),
                  pl.BlockSpec((TS_B, B_QW // 2), lambda b, t: (t, 0)),
                  pl.BlockSpec((TS_B, B_QW // 2), lambda b, t: (t, 0)),
                  pl.BlockSpec(dstack.shape, fixed),
                  pl.BlockSpec(qdec.shape, fixed),
                  pl.BlockSpec(kdec.shape, fixed),
                  pl.BlockSpec(cg.shape, fixed),
                  pl.BlockSpec(bd.shape, fixed),
                  pl.BlockSpec((1, B_VW), fixed)],
        out_specs=pl.BlockSpec((TS_B, B_VW), lambda b, t: (b * nt + t, 0)),
        out_shape=jax.ShapeDtypeStruct((B * S, B_VW), BF16),
        scratch_shapes=[pltpu.VMEM((B_VW, B_QW), F32)],
        compiler_params=_cparams("parallel", "arbitrary"),
        name="retention",
    )(zb, cos, sin, dstack, qdec, kdec, cg, bd, nw)


def _attn_kernel(q_ref, kp_ref, kc_ref, vp_ref, vc_ref, bias_ref, o_ref, kwin_ref, vwin_ref):
    t = pl.program_id(1)
    kwin_ref[0:TQ_C, :] = kp_ref[...]
    kwin_ref[TQ_C:, :] = kc_ref[...]
    vwin_ref[0:TQ_C, :] = vp_ref[...]
    vwin_ref[TQ_C:, :] = vc_ref[...]
    jcol = lax.broadcasted_iota(jnp.int32, (1, C_BWIN), 1)
    first_head = lax.broadcasted_iota(jnp.int32, (1, 2 * C_DH), 1) < C_DH
    zero = jnp.zeros((), BF16)

    def block(i, carry):
        r0 = pl.multiple_of(i * C_QB, C_QB)
        pen = jnp.where((jcol + (r0 + (t - 1) * TQ_C)) >= 0, 0.0, MASK_VALUE)
        for p in range(C_HEADS // 2):
            lanes = slice(2 * p * C_DH, 2 * (p + 1) * C_DH)
            q = q_ref[pl.ds(r0, C_QB), lanes] * (C_DH ** -0.5)
            lhs = jnp.concatenate([jnp.where(first_head, q, zero), jnp.where(first_head, zero, q)], axis=0)

            def scores(j):
                k = kwin_ref[pl.ds(r0 + j * C_KT, C_KT), lanes]
                return (_dot_nt(lhs, k) + bias_ref[p, :, j * C_KT:(j + 1) * C_KT]
                        + pen[:, j * C_KT:(j + 1) * C_KT])

            m = scores(0)
            for j in range(1, C_BWIN // C_KT):
                m = jnp.maximum(m, scores(j))
            m = jnp.max(m, axis=-1, keepdims=True)
            acc = l = None
            for j in range(C_BWIN // C_KT):
                e = jnp.exp(scores(j) - m)
                pvj = _dot(e.astype(BF16), vwin_ref[pl.ds(r0 + j * C_KT, C_KT), lanes])
                acc = pvj if acc is None else acc + pvj
                l = e if l is None else l + e
            pv = acc / jnp.sum(l, axis=-1, keepdims=True)
            o = jnp.where(first_head, pv[:C_QB], pv[C_QB:])
            o_ref[pl.ds(r0, C_QB), lanes] = o.astype(BF16)
        return carry

    lax.fori_loop(0, TQ_C // C_QB, block, 0, unroll=True)


def _attention(zc, bias, B, S):
    nt = S // TQ_C
    cur = lambda col: (lambda b, t: (b * nt + t, col))
    prev = lambda col: (lambda b, t: (b * nt + jnp.maximum(t - 1, 0), col))
    blk = (TQ_C, C_WIDTH)
    return pl.pallas_call(
        _attn_kernel,
        grid=(B, nt),
        in_specs=[pl.BlockSpec(blk, cur(0)),
                  pl.BlockSpec(blk, prev(1)), pl.BlockSpec(blk, cur(1)),
                  pl.BlockSpec(blk, prev(2)), pl.BlockSpec(blk, cur(2)),
                  pl.BlockSpec(bias.shape, lambda b, t: (0, 0, 0))],
        out_specs=pl.BlockSpec(blk, cur(0)),
        out_shape=jax.ShapeDtypeStruct((B * S, C_WIDTH), BF16),
        scratch_shapes=[pltpu.VMEM((2 * TQ_C, C_WIDTH), BF16),
                        pltpu.VMEM((2 * TQ_C, C_WIDTH), BF16)],
        compiler_params=_cparams("parallel", "arbitrary"),
        name="chunk_attention",
    )(zc, zc, zc, zc, zc, bias)


def _route(logits):
    lane = lax.broadcasted_iota(jnp.int32, logits.shape, 1).astype(F32)
    big = float(1 << 20)
    neg = -jnp.inf

    def first_argmax(vals):
        m = jnp.max(vals, axis=-1, keepdims=True)
        idx = jnp.min(jnp.where(vals == m, lane, big), axis=-1, keepdims=True)
        return m, idx

    gl = jnp.where(lane < N_GROUPS, logits, neg)
    gm, grp = first_argmax(gl)
    p_grp = 1.0 / jnp.sum(jnp.exp(gl - gm), axis=-1, keepdims=True)
    lo = N_GROUPS + grp * EXPERTS_PER_GROUP
    el = jnp.where((lane >= lo) & (lane < lo + EXPERTS_PER_GROUP), logits, neg)
    v1, i1 = first_argmax(el)
    v2, i2 = first_argmax(jnp.where(lane == i1, neg, el))
    e2 = jnp.exp(v2 - v1)
    g1 = p_grp / (1.0 + e2)
    g2 = p_grp * e2 / (1.0 + e2)
    out = jnp.where(lane == 0, i1 - N_GROUPS, 0.0)
    out = jnp.where(lane == 1, i2 - N_GROUPS, out)
    out = jnp.where(lane == 2, g1, out)
    out = jnp.where(lane == 3, g2, out)
    return out


def _outproj_kernel(x_ref, ya_ref, yb_ref, yc_ref, w_ref, nw_ref, wr_ref, xo_ref, r_ref):
    a0, a1 = A_WIDTH, A_WIDTH + B_VW
    x = x_ref[...]
    x = x + _dot(ya_ref[...], w_ref[0:a0, :])
    x = x + _dot(yb_ref[...], w_ref[a0:a1, :])
    x = x + _dot(yc_ref[...], w_ref[a1:, :])
    xo_ref[...] = x
    h = _rms(x, nw_ref[...])
    h_hi = h.astype(BF16)
    h_lo = (h - h_hi.astype(F32)).astype(BF16)
    d_hi = _dot(h_hi, wr_ref[...])
    logits = d_hi + pltpu.roll(d_hi, ROUTE_W // 2, 1) + _dot(h_lo, wr_ref[...])
    r_ref[...] = _route(logits)


def _outproj(x2d, ya, yb, yc, w, nw, wr3):
    T = x2d.shape[0]
    row = lambda i: (i, 0)
    fixed = lambda i: (0, 0)
    return pl.pallas_call(
        _outproj_kernel,
        grid=(T // TM,),
        in_specs=[pl.BlockSpec((TM, D_MODEL), row),
                  pl.BlockSpec((TM, A_WIDTH), row),
                  pl.BlockSpec((TM, B_VW), row),
                  pl.BlockSpec((TM, C_WIDTH), row),
                  pl.BlockSpec((Y_W, D_MODEL), fixed),
                  pl.BlockSpec((1, D_MODEL), fixed),
                  pl.BlockSpec((D_MODEL, ROUTE_W), fixed)],
        out_specs=[pl.BlockSpec((TM, D_MODEL), row),
                   pl.BlockSpec((TM, ROUTE_W), row)],
        out_shape=[jax.ShapeDtypeStruct((T, D_MODEL), F32),
                   jax.ShapeDtypeStruct((T, ROUTE_W), F32)],
        compiler_params=_cparams("parallel"),
        name="outproj_router",
    )(x2d, ya, yb, yc, w, nw, wr3)


PAD_PIECES = tuple(MOE_BM >> (k + 1) for k in range(MOE_BM.bit_length() - 1))
DMA_GROUP = 8


def _tile_rows(ref, first_tile_row, n_rows=1):
    return ref.at[pl.ds(pl.multiple_of(first_tile_row, ROW_TILE), n_rows * ROW_TILE)]


def _rows_done(src_hbm, dst, sem, n_rows):
    pltpu.make_async_copy(_tile_rows(src_hbm, 0, n_rows), _tile_rows(dst, 0, n_rows), sem).wait()


def _dispatch_kernel(cnt_ref, pstart_ref, pos_ref, x_ref, nw_ref, xs_hbm, hbuf, zbuf, sem, zsem):
    i = pl.program_id(0)
    n = pl.num_programs(0)
    tm = TM_DISPATCH
    cur = lax.rem(i, 2)
    stage = hbuf.at[cur]
    _store_row_tiles(stage, _rms(x_ref[...], nw_ref[...]))

    def pad_pieces(e):
        cnt = cnt_ref[e]
        first = pstart_ref[e] + cnt
        n_pad = (-cnt) & (MOE_BM - 1)
        for p in PAD_PIECES:
            yield (n_pad & p) != 0, first + (n_pad & (MOE_BM - 2 * p)), p

    def spare_blocks():
        last = N_EXPERTS - 1
        n_used = (pstart_ref[last] + cnt_ref[last] + (MOE_BM - 1)) // MOE_BM
        n_blocks = xs_hbm.shape[0] // (MOE_BM * ROW_TILE)
        for b in range(n_blocks - N_EXPERTS, n_blocks):
            for first in range(0, MOE_BM, PAD_PIECES[0]):
                yield b >= n_used, b * MOE_BM + first, PAD_PIECES[0]

    @pl.when(i == 0)
    def _():
        zbuf[...] = jnp.zeros_like(zbuf)
        for phase in ("start", "wait"):
            for e in range(N_EXPERTS + 1):
                for fire, slot, p in (pad_pieces(e) if e < N_EXPERTS else spare_blocks()):
                    @pl.when(fire)
                    def _():
                        cp = pltpu.make_async_copy(_tile_rows(zbuf, 0, p),
                                                   _tile_rows(xs_hbm, slot * ROW_TILE, p), zsem)
                        cp.start() if phase == "start" else cp.wait()

    def body(g, carry):
        rows = [g * DMA_GROUP + k for k in range(DMA_GROUP)]
        slots = [(pos_ref[0, 0, r], pos_ref[0, 0, tm + r]) for r in rows]
        for r, (s0, s1) in zip(rows, slots):
            src = _tile_rows(stage, r * ROW_TILE)
            pltpu.make_async_copy(src, _tile_rows(xs_hbm, s0), sem.at[cur]).start(priority=0)
            pltpu.make_async_copy(src, _tile_rows(xs_hbm, s1), sem.at[cur]).start(priority=1)
        return carry
    lax.fori_loop(0, tm // DMA_GROUP, body, 0)

    @pl.when(i > 0)
    def _():
        _rows_done(stage, xs_hbm, sem.at[1 - cur], 2 * tm)

    @pl.when(i == n - 1)
    def _():
        _rows_done(stage, xs_hbm, sem.at[cur], 2 * tm)


def _dispatch(x2d, nw, pos, counts, pad_start, n_slots):
    T = x2d.shape[0]
    tm = TM_DISPATCH
    grid_spec = pltpu.PrefetchScalarGridSpec(
        num_scalar_prefetch=2,
        grid=(T // tm,),
        in_specs=[pl.BlockSpec((1, 1, 2 * tm), lambda i, c, s: (i, 0, 0), memory_space=pltpu.SMEM),
                  pl.BlockSpec((tm, D_MODEL), lambda i, c, s: (i, 0)),
                  pl.BlockSpec((1, D_MODEL), lambda i, c, s: (0, 0))],
        out_specs=pl.BlockSpec(memory_space=pl.ANY),
        scratch_shapes=[pltpu.VMEM((2, tm * ROW_TILE, LANES), F32),
                        pltpu.VMEM((PAD_PIECES[0] * ROW_TILE, LANES), F32),
                        pltpu.SemaphoreType.DMA((2,)), pltpu.SemaphoreType.DMA(())],
    )
    return pl.pallas_call(
        _dispatch_kernel,
        grid_spec=grid_spec,
        out_shape=jax.ShapeDtypeStruct((n_slots * ROW_TILE, LANES), F32),
        compiler_params=_cparams("arbitrary"),
        name="dispatch",
    )(counts, pad_start, pos, x2d, nw)


def _expert_kernel(blk_exp_ref, nused_ref, x_ref, wg_ref, wu_ref, wd_ref, y_ref, wg_bf, wu_bf, wd_bf):
    i = pl.program_id(0)
    used = i < nused_ref[0]

    @pl.when((i == 0) | (blk_exp_ref[i] != blk_exp_ref[jnp.maximum(i - 1, 0)]))
    def _():
        wg_bf[...] = wg_ref[0, 0].astype(BF16)
        wu_bf[...] = wu_ref[0, 0].astype(BF16)
        wd_bf[...] = wd_ref[0, 0].astype(BF16)

    @pl.when(used)
    def _():
        xb = _load_row_tiles(x_ref, 0, MOE_BM).astype(BF16)
        act = _silu(_dot(xb, wg_bf[...])) * _dot(xb, wu_bf[...])
        _store_row_tiles(y_ref, _dot(act.astype(BF16), wd_bf[...]))

    @pl.when(jnp.logical_not(used))
    def _():
        y_ref[...] = jnp.zeros_like(y_ref)


def _experts(xs, blk_exp, nused, layer, w_gate, w_up, w_down):
    nblk = blk_exp.shape[0]
    last_used = lambda i, nu: jnp.minimum(i, nu[0] - 1)
    expert = lambda i, be, nu: (layer, be[i], 0, 0)
    grid_spec = pltpu.PrefetchScalarGridSpec(
        num_scalar_prefetch=2,
        grid=(nblk,),
        in_specs=[pl.BlockSpec((MOE_BM * ROW_TILE, LANES), lambda i, be, nu: (last_used(i, nu), 0)),
                  pl.BlockSpec((1, 1, D_MODEL, D_EXPERT), expert),
                  pl.BlockSpec((1, 1, D_MODEL, D_EXPERT), expert),
                  pl.BlockSpec((1, 1, D_EXPERT, D_MODEL), expert)],
        out_specs=pl.BlockSpec((MOE_BM * ROW_TILE, LANES), lambda i, be, nu: (i, 0)),
        scratch_shapes=[pltpu.VMEM((D_MODEL, D_EXPERT), BF16), pltpu.VMEM((D_MODEL, D_EXPERT), BF16),
                        pltpu.VMEM((D_EXPERT, D_MODEL), BF16)],
    )
    return pl.pallas_call(
        _expert_kernel,
        grid_spec=grid_spec,
        out_shape=jax.ShapeDtypeStruct((nblk * MOE_BM * ROW_TILE, LANES), F32),
        compiler_params=_cparams("arbitrary"),
        name="experts",
    )(blk_exp, nused, xs, w_gate, w_up, w_down)


def _combine_rows(pos_ref, pos_next_ref, x_ref, r_ref, y_hbm, ybuf, sem, request_next_first):
    i = pl.program_id(0)
    n_rows = 2 * TM_COMBINE
    slot = lax.rem(i, 2)

    @pl.when(i == 0)
    def _():
        _request_rows(pos_ref, y_hbm, ybuf, sem, 0, unrolled=False)

    if request_next_first:
        _request_next(pos_next_ref, y_hbm, ybuf, sem, unrolled=False)
    _rows_done(y_hbm, ybuf.at[slot], sem.at[slot], n_rows)
    r = r_ref[...]
    g0, g1 = r[:, 2:3], r[:, 3:4]
    y0 = _load_row_tiles(ybuf.at[slot], 0, TM_COMBINE)
    y1 = _load_row_tiles(ybuf.at[slot], TM_COMBINE, TM_COMBINE)
    return x_ref[...] + (g0 * y0 + g1 * y1)


def _request_rows(idx_ref, y_hbm, ybuf, sem, s, unrolled):
    n_rows = 2 * TM_COMBINE
    group = 2 * DMA_GROUP

    def body(g, carry):
        rows = [g * group + k for k in range(group)]
        slots = [idx_ref[0, 0, r] for r in rows]
        for k, src_row in enumerate(slots):
            pltpu.make_async_copy(_tile_rows(y_hbm, src_row),
                                  _tile_rows(ybuf.at[s], rows[k] * ROW_TILE), sem.at[s]).start(priority=k % 2)
        return carry

    if unrolled:
        for g in range(n_rows // group):
            body(g, 0)
    else:
        lax.fori_loop(0, n_rows // group, body, 0)


def _request_next(pos_next_ref, y_hbm, ybuf, sem, unrolled):
    _request_rows(pos_next_ref, y_hbm, ybuf, sem, 1 - lax.rem(pl.program_id(0), 2), unrolled)


def _drain_last(y_hbm, ybuf, sem):
    i = pl.program_id(0)

    @pl.when(i == pl.num_programs(0) - 1)
    def _():
        other = 1 - lax.rem(i, 2)
        _rows_done(y_hbm, ybuf.at[other], sem.at[other], 2 * TM_COMBINE)


def _combine_inproj_kernel(pos_ref, pos_next_ref, x_ref, r_ref, y_hbm, nw_ref, w_ref,
                           xo_ref, za_ref, zb_ref, zc_ref, ybuf, sem):
    x = _combine_rows(pos_ref, pos_next_ref, x_ref, r_ref, y_hbm, ybuf, sem, request_next_first=False)
    xo_ref[...] = x
    _request_next(pos_next_ref, y_hbm, ybuf, sem, unrolled=True)
    _project(_rms(x, nw_ref[...]), w_ref, za_ref, zb_ref, zc_ref)
    _drain_last(y_hbm, ybuf, sem)


def _combine_final_kernel(pos_ref, pos_next_ref, x_ref, r_ref, y_hbm, nw_ref, o_ref, ybuf, sem):
    x = _combine_rows(pos_ref, pos_next_ref, x_ref, r_ref, y_hbm, ybuf, sem, request_next_first=True)
    o_ref[...] = _rms(x, nw_ref[...])
    _drain_last(y_hbm, ybuf, sem)


def _combine(x2d, route, ys, pos, nw, w=None):
    T = x2d.shape[0]
    tm = TM_COMBINE
    row = lambda i: (i, 0)
    fixed = lambda i: (0, 0)
    n_tiles = T // tm
    in_specs = [pl.BlockSpec((1, 1, 2 * tm), lambda i: (i, 0, 0), memory_space=pltpu.SMEM),
                pl.BlockSpec((1, 1, 2 * tm), lambda i: (jnp.minimum(i + 1, n_tiles - 1), 0, 0),
                             memory_space=pltpu.SMEM),
                pl.BlockSpec((tm, D_MODEL), row),
                pl.BlockSpec((tm, ROUTE_W), row),
                pl.BlockSpec(memory_space=pl.ANY),
                pl.BlockSpec((1, D_MODEL), fixed)]
    scratch = [pltpu.VMEM((2, 2 * tm * ROW_TILE, LANES), F32), pltpu.SemaphoreType.DMA((2,))]
    if w is None:
        return pl.pallas_call(
            _combine_final_kernel,
            grid=(T // tm,),
            in_specs=in_specs,
            out_specs=pl.BlockSpec((tm, D_MODEL), row),
            out_shape=jax.ShapeDtypeStruct((T, D_MODEL), F32),
            scratch_shapes=scratch,
            compiler_params=_cparams("arbitrary"),
            name="combine_final",
        )(pos, pos, x2d, route, ys, nw)
    return pl.pallas_call(
        _combine_inproj_kernel,
        grid=(T // tm,),
        in_specs=in_specs + [pl.BlockSpec((D_MODEL, ZA_W + ZB_W + ZC_W), fixed)],
        out_specs=[pl.BlockSpec((tm, D_MODEL), row),
                   pl.BlockSpec((tm, ZA_W), row),
                   pl.BlockSpec((tm, ZB_W), row),
                   pl.BlockSpec((tm, ZC_W), row)],
        out_shape=[jax.ShapeDtypeStruct((T, D_MODEL), F32),
                   jax.ShapeDtypeStruct((T, ZA_W), BF16),
                   jax.ShapeDtypeStruct((T, ZB_W), BF16),
                   jax.ShapeDtypeStruct((T, ZC_W), BF16)],
        scratch_shapes=scratch,
        compiler_params=_cparams("arbitrary"),
        name="combine_inproj",
    )(pos, pos, x2d, route, ys, nw, w)


def _pad_value_heads(t, axis):
    shape = t.shape
    t = t.reshape(shape[:axis] + (B_HEADS, B_DV) + shape[axis + 1:])
    pad = [(0, 0)] * t.ndim
    pad[axis + 1] = (0, B_VSLOT - B_DV)
    return jnp.pad(t, pad).reshape(shape[:axis] + (B_VW,) + shape[axis + 1:])


def _layout_in_weights(w):
    D = w.shape[0]
    nqk, nv = B_HEADS * B_DK, B_HEADS * B_DV
    o = ZA_W

    def qk(block):
        t = block.reshape(D, B_HEADS, 2, B_HALF).transpose(0, 2, 1, 3)
        t = jnp.pad(t, ((0, 0), (0, 0), (0, 0), (0, B_QSLOT - B_HALF)))
        return t.reshape(D, B_QW)

    return jnp.concatenate([
        w[:, :o], qk(w[:, o:o + nqk]), qk(w[:, o + nqk:o + 2 * nqk]),
        _pad_value_heads(w[:, o + 2 * nqk:o + 2 * nqk + nv], 1),
        _pad_value_heads(w[:, o + 2 * nqk + nv:o + 2 * nqk + 2 * nv], 1),
        w[:, o + 2 * nqk + 2 * nv:]], axis=1)


def _layout_out_weights(w):
    nv = B_HEADS * B_DV
    return jnp.concatenate([w[:A_WIDTH], _pad_value_heads(w[A_WIDTH:A_WIDTH + nv], 0),
                            w[A_WIDTH + nv:]], axis=0)


def _retention_tables(S):
    log_gamma = jnp.log1p(-jnp.exp2(-5.0 - jnp.arange(B_HEADS, dtype=F32)))
    idx = jnp.arange(CHUNK, dtype=F32)
    rel = idx[:, None] - idx[None, :]
    decay = jnp.where(rel >= 0, jnp.exp(log_gamma[:, None, None] * jnp.maximum(rel, 0.0)), 0.0)
    dstack = decay.reshape(B_HEADS * CHUNK, CHUNK)
    k_decay = jnp.exp(log_gamma[:, None] * (CHUNK - 1.0 - idx)[None, :])
    q_decay = jnp.exp(log_gamma[:, None] * (idx + 1.0)[None, :])
    chunk_gamma = jnp.exp(log_gamma * CHUNK)

    def lanes(per_head):
        t = jnp.repeat(per_head[..., None], B_QSLOT, axis=-1)
        t = t.reshape(per_head.shape[:-1] + (B_HEADS * B_QSLOT,))
        return jnp.concatenate([t, t], axis=-1)

    qdec = lanes(q_decay.T)
    kdec = lanes(k_decay.T)
    cg = lanes(chunk_gamma[None, :])
    row_head = np.arange(B_VW) // B_VSLOT
    lane_head = (np.arange(B_QW) % (B_QW // 2)) // B_QSLOT
    bd = jnp.asarray((row_head[:, None] == lane_head[None, :]).astype(np.float32))

    half = B_HALF
    inv_freq = ROPE_BASE ** (-jnp.arange(half, dtype=F32) / half)
    ang = jnp.arange(S).astype(F32)[:, None] * inv_freq[None, :]
    pad = ((0, 0), (0, B_QSLOT - half))
    cos = jnp.tile(jnp.pad(jnp.cos(ang), pad), (1, B_HEADS))
    sin = jnp.tile(jnp.pad(jnp.sin(ang), pad), (1, B_HEADS))
    return cos, sin, (dstack, qdec, kdec, cg, bd)


def _attention_bias(rel_bias):
    qi = np.arange(C_QB)[:, None]
    km = np.arange(C_BWIN)[None, :]
    lag = (qi // CHUNK + N_PREV_CHUNKS) - km // CHUNK
    in_band = jnp.asarray((lag >= 0) & (lag <= N_PREV_CHUNKS))
    P = C_QB + C_BWIN
    t = np.arange(P)
    t = np.where(t < C_BWIN, t, t - P)
    dist = N_PREV_CHUNKS * CHUNK - t
    row = rel_bias.astype(F32)[:, np.clip(dist, -MAX_REL, MAX_REL) + MAX_REL]
    b = jnp.tile(row, (1, C_QB))[:, :C_QB * (P - 1)].reshape(C_HEADS, C_QB, P - 1)[:, :, :C_BWIN]
    b = jnp.where(in_band[None], b, MASK_VALUE)
    return b.reshape(C_HEADS // 2, 2 * C_QB, C_BWIN)


def _hgrn_tables():
    tri = jnp.asarray(np.tril(np.ones((CHUNK, CHUNK), np.float32))).astype(BF16)
    head = np.arange(A_WIDTH) // A_DK
    same = (head[:, None] == head[None, :]).astype(np.float32)
    return tri, jnp.asarray(same).astype(BF16), jnp.asarray(same)


def _moe_plan(route, T):
    A = 2 * T
    nblk = A // MOE_BM + N_EXPERTS
    eid = route[:, 0:2].astype(jnp.int32)
    flat_e = eid.reshape(A)
    onehot = (flat_e[:, None] == jnp.arange(N_EXPERTS, dtype=jnp.int32)[None, :]).astype(jnp.int32)
    csum = jnp.cumsum(onehot, axis=0)
    counts = csum[-1]
    rank = jnp.sum((csum - onehot) * onehot, axis=1)
    padded = (counts + MOE_BM - 1) // MOE_BM * MOE_BM
    pad_end = jnp.cumsum(padded)
    pad_start = pad_end - padded
    dest = jnp.sum(onehot * pad_start[None, :], axis=1) + rank
    nused = (pad_end[-1] // MOE_BM).astype(jnp.int32)
    blk_start = jnp.arange(nblk, dtype=jnp.int32) * MOE_BM
    blk_exp = jnp.sum((pad_end[None, :] <= blk_start[:, None]).astype(jnp.int32), axis=1)
    last = jnp.take(blk_exp, jnp.maximum(nused - 1, 0))
    blk_exp = jnp.where(jnp.arange(nblk) < nused, jnp.minimum(blk_exp, N_EXPERTS - 1), last)
    tm = TM_COMBINE
    pos = (dest * ROW_TILE).reshape(T // tm, tm, 2).transpose(0, 2, 1).reshape(T // tm, 1, 2 * tm)
    return pos, counts.astype(jnp.int32), pad_start.astype(jnp.int32), blk_exp, nused.reshape(1)


def kernel(x, w_in, w_out, norm_mix, norm_ffn, norm_final, hgrn_lb, hgrn_norm, ret_norm, rel_bias,
           router_group, router_expert, expert_w_gate, expert_w_up, expert_w_down):
    B, S, D = x.shape
    T = B * S
    depth = w_in.shape[0]

    p = jax.nn.softmax(hgrn_lb.astype(F32), axis=0)
    lower_bounds = jnp.clip(jnp.cumsum(p, axis=0) - p[0], 0.0, 1.0)

    cos, sin, ret_consts = _retention_tables(S)
    bias = _attention_bias(rel_bias)
    tri, eones, bd_a = _hgrn_tables()

    x2d = x.reshape(T, D).astype(F32)
    route = ys = pos = None
    out = None
    for l in range(depth):
        w_in_l = _layout_in_weights(w_in[l]).astype(BF16)
        w_out_l = _layout_out_weights(w_out[l]).astype(BF16)
        ret_nw = _pad_value_heads(ret_norm[l].astype(F32), 0).reshape(1, B_VW)
        nw_mix = norm_mix[l].astype(F32).reshape(1, D)
        if l == 0:
            za, zb, zc = _inproj(x2d, nw_mix, w_in_l)
        else:
            x2d, za, zb, zc = _combine(x2d, route, ys, pos, nw_mix, w_in_l)

        ya = _hgrn(za, lower_bounds[l].reshape(1, A_WIDTH), hgrn_norm[l].astype(F32).reshape(1, A_WIDTH),
                   tri, eones, bd_a, B, S)
        yb = _retention(zb, cos, sin, ret_consts, ret_nw, B, S)
        yc = _attention(zc, bias, B, S)

        wr = jnp.concatenate([router_group[l], router_expert[l]], axis=1).astype(F32)
        wr_hi = wr.astype(BF16)
        wr_lo = (wr - wr_hi.astype(F32)).astype(BF16)
        gap = jnp.zeros((D, ROUTE_W // 2 - wr.shape[1]), BF16)
        wr2 = jnp.concatenate([wr_hi, gap, wr_lo, gap], axis=1)
        nw_ffn = norm_ffn[l].astype(F32).reshape(1, D)
        x2d, route = _outproj(x2d, ya, yb, yc, w_out_l, nw_ffn, wr2)

        pos, counts, pad_start, blk_exp, nused = _moe_plan(route, T)
        xs = _dispatch(x2d, nw_ffn, pos, counts, pad_start, blk_exp.shape[0] * MOE_BM)
        ys = _experts(xs, blk_exp, nused, l, expert_w_gate, expert_w_up, expert_w_down)

    out = _combine(x2d, route, ys, pos, norm_final.astype(F32).reshape(1, D))
    return out.reshape(B, S, D).astype(x.dtype)
```

```python
import functools

import numpy as np
import jax
import jax.numpy as jnp
from jax import lax
from jax.experimental import pallas as pl
from jax.experimental.pallas import tpu as pltpu

F32 = jnp.float32
BF16 = jnp.bfloat16

D_MODEL = 1024
CHUNK = 64
NORM_EPS = 1e-6
MASK_VALUE = -1e30
MIN_FORGET = 1e-30
LOG2_E = 1.4426950408889634

A_HEADS, A_DK, A_DV = 4, 64, 64
A_WIDTH = A_HEADS * A_DV
B_HEADS, B_DK, B_DV = 4, 48, 96
B_HALF = B_DK // 2
B_QSLOT = 32
B_VSLOT = 128
B_QW = 2 * B_HEADS * B_QSLOT
B_VW = B_HEADS * B_VSLOT
ROPE_BASE = 10000.0
C_HEADS, C_DH = 6, 64
C_WIDTH = C_HEADS * C_DH
N_PREV_CHUNKS = 8
MAX_REL = 128
C_QB = 2 * CHUNK
C_BWIN = C_QB + N_PREV_CHUNKS * CHUNK
C_KT = 128

ZA_W = 4 * A_WIDTH
ZB_W = 2 * B_QW + 2 * B_VW
ZC_W = 3 * C_WIDTH
Y_W = A_WIDTH + B_VW + C_WIDTH

N_GROUPS, EXPERTS_PER_GROUP = 4, 8
N_EXPERTS = N_GROUPS * EXPERTS_PER_GROUP
D_EXPERT = 512
ROUTE_W = 128

SUB = 8
VMEM_LIMIT = 56 * 1024 * 1024

TM = 512
TS_A = 256
TS_B = 256
TQ_C = 512
MOE_BM = 512
TM_COMBINE = 512
TM_DISPATCH = TM_COMBINE


def _cparams(*sem):
    return pltpu.CompilerParams(dimension_semantics=sem, vmem_limit_bytes=VMEM_LIMIT)


def _sigmoid(x):
    return 1.0 / (1.0 + jnp.exp(-x))


def _silu(x):
    return x * _sigmoid(x)


def _split3(x):
    hi = x.astype(BF16)
    r1 = x - hi.astype(F32)
    mid = r1.astype(BF16)
    lo = (r1 - mid.astype(F32)).astype(BF16)
    return hi, mid, lo


def _dot(a, b):
    return jnp.dot(a, b, preferred_element_type=F32)


def _dot_nt(a, b):
    return lax.dot_general(a, b, (((1,), (1,)), ((), ())), preferred_element_type=F32)


def _dot_tn(a, b):
    return lax.dot_general(a, b, (((0,), (0,)), ((), ())), preferred_element_type=F32)


LANES = 128
ROW_TILE = D_MODEL // LANES


def _store_row_tiles(ref, val):
    n = val.shape[0]
    for c in range(ROW_TILE):
        ref[pl.ds(c, n, stride=ROW_TILE), :] = val[:, c * LANES:(c + 1) * LANES]


def _load_row_tiles(ref, first_row, n):
    return jnp.concatenate(
        [ref[pl.ds(first_row * ROW_TILE + c, n, stride=ROW_TILE), :] for c in range(ROW_TILE)], axis=1)


def _rms(x, w):
    ms = jnp.mean(x * x, axis=-1, keepdims=True)
    return x * lax.rsqrt(ms + NORM_EPS) * w


def _project(h, w_ref, za_ref, zb_ref, zc_ref):
    hb = h.astype(BF16)
    za_ref[...] = _dot(hb, w_ref[:, 0:ZA_W]).astype(BF16)
    zb_ref[...] = _dot(hb, w_ref[:, ZA_W:ZA_W + ZB_W]).astype(BF16)
    zc_ref[...] = _dot(hb, w_ref[:, ZA_W + ZB_W:]).astype(BF16)


def _inproj_kernel(x_ref, nw_ref, w_ref, za_ref, zb_ref, zc_ref):
    _project(_rms(x_ref[...], nw_ref[...]), w_ref, za_ref, zb_ref, zc_ref)


def _inproj(x2d, nw, w):
    T = x2d.shape[0]
    row = lambda i: (i, 0)
    fixed = lambda i: (0, 0)
    return pl.pallas_call(
        _inproj_kernel,
        grid=(T // TM,),
        in_specs=[pl.BlockSpec((TM, D_MODEL), row),
                  pl.BlockSpec((1, D_MODEL), fixed),
                  pl.BlockSpec((D_MODEL, ZA_W + ZB_W + ZC_W), fixed)],
        out_specs=[pl.BlockSpec((TM, ZA_W), row),
                   pl.BlockSpec((TM, ZB_W), row),
                   pl.BlockSpec((TM, ZC_W), row)],
        out_shape=[jax.ShapeDtypeStruct((T, ZA_W), BF16),
                   jax.ShapeDtypeStruct((T, ZB_W), BF16),
                   jax.ShapeDtypeStruct((T, ZC_W), BF16)],
        compiler_params=_cparams("parallel"),
        name="inproj",
    )(x2d, nw, w)


def _hgrn_kernel(z_ref, lb_ref, nw_ref, tri_ref, eones_ref, bd_ref, o_ref, st_ref):
    @pl.when(pl.program_id(1) == 0)
    def _():
        st_ref[...] = jnp.zeros_like(st_ref)

    W = A_WIDTH
    lb = lb_ref[...]
    lane_head = lax.broadcasted_iota(jnp.int32, (1, W), 1) // A_DK
    head_masks = [lane_head == h for h in range(A_HEADS)]
    tri = tri_ref[...]
    eones = eones_ref[...]
    bd = bd_ref[...]
    jj = lax.broadcasted_iota(jnp.int32, (SUB, SUB, 1), 0)
    ii = lax.broadcasted_iota(jnp.int32, (SUB, SUB, 1), 1)
    causal = jj <= ii
    col = lax.broadcasted_iota(jnp.int32, (1, CHUNK), 1)

    n_chunks = z_ref.shape[0] // CHUNK
    n_sub = CHUNK // SUB
    outs = []
    for c in range(n_chunks):
        r0 = c * CHUNK
        zq = z_ref[r0:r0 + CHUNK, 0:W].astype(F32)
        zf = z_ref[r0:r0 + CHUNK, W:2 * W].astype(F32)
        vv = z_ref[r0:r0 + CHUNK, 2 * W:3 * W]
        vf = vv.astype(F32)
        f = lb + (1.0 - lb) * _sigmoid(zf)
        lf = jnp.log(jnp.maximum(f, MIN_FORGET))
        kk = (1.0 - lb) * _sigmoid(-zf)
        qf = _silu(zq) * (A_DK ** -0.5)
        hi, mid, lo = _split3(lf)
        G = (_dot(tri, hi) + _dot(tri, mid) + _dot(tri, lo)) * LOG2_E
        g_last = G[CHUNK - 1:CHUNK, :]

        st = st_ref[...]
        inter = _dot_nt((qf * jnp.exp2(G)).astype(BF16), st.astype(BF16))
        khat = kk * jnp.exp2(g_last - G)
        ut = _dot_tn(vv, khat.astype(BF16))
        st_ref[...] = jnp.exp2(g_last) * st + ut * bd

        p_rows = []
        for s in range(1, n_sub):
            b = G[s * SUB - 1:s * SUB, :]
            qt = qf[s * SUB:(s + 1) * SUB, :] * jnp.exp2(G[s * SUB:(s + 1) * SUB, :] - b)
            kt = kk * jnp.exp2(jnp.minimum(b - G, 0.0))
            lhs = jnp.concatenate([jnp.where(m, qt, 0.0) for m in head_masks], axis=0)
            p = _dot_nt(lhs.astype(BF16), kt.astype(BF16))
            p_rows.append(jnp.where(col < s * SUB, p, 0.0))
        pall = jnp.concatenate(p_rows, axis=0)
        pv = _dot(pall.astype(BF16), vv)

        pieces = []
        for s in range(n_sub):
            sl = slice(s * SUB, (s + 1) * SUB)
            gb, qb, kb, vb = G[sl], qf[sl], kk[sl], vf[sl]
            dec = jnp.exp2(jnp.minimum(gb[None, :, :] - gb[:, None, :], 0.0))
            a = (qb[None, :, :] * kb[:, None, :]) * dec
            r = _dot(a.reshape(SUB * SUB, W).astype(BF16), eones).reshape(SUB, SUB, W)
            diag = jnp.sum(jnp.where(causal, r, 0.0) * vb[:, None, :], axis=0)
            piece = inter[sl] + diag
            if s > 0:
                base = (s - 1) * A_HEADS * SUB
                for h in range(A_HEADS):
                    blk = pv[base + h * SUB:base + (h + 1) * SUB, :]
                    piece = piece + jnp.where(head_masks[h], blk, 0.0)
            pieces.append(piece)
        outs.append(jnp.concatenate(pieces, axis=0))

    o = jnp.concatenate(outs, axis=0)
    g = z_ref[:, 3 * W:4 * W].astype(F32)
    o_ref[...] = (_rms(o, nw_ref[...]) * _silu(g)).astype(BF16)


def _hgrn(za, lb, nw, tri, eones, bd, B, S):
    nt = S // TS_A
    fixed = lambda b, t: (0, 0)
    return pl.pallas_call(
        _hgrn_kernel,
        grid=(B, nt),
        in_specs=[pl.BlockSpec((TS_A, ZA_W), lambda b, t: (b * nt + t, 0)),
                  pl.BlockSpec((1, A_WIDTH), fixed),
                  pl.BlockSpec((1, A_WIDTH), fixed),
                  pl.BlockSpec((CHUNK, CHUNK), fixed),
                  pl.BlockSpec((A_WIDTH, A_WIDTH), fixed),
                  pl.BlockSpec((A_WIDTH, A_WIDTH), fixed)],
        out_specs=pl.BlockSpec((TS_A, A_WIDTH), lambda b, t: (b * nt + t, 0)),
        out_shape=jax.ShapeDtypeStruct((B * S, A_WIDTH), BF16),
        scratch_shapes=[pltpu.VMEM((A_WIDTH, A_WIDTH), F32)],
        compiler_params=_cparams("parallel", "arbitrary"),
        name="hgrn2",
    )(za, lb, nw, tri, eones, bd)


def _ret_kernel(z_ref, cos_ref, sin_ref, dstack_ref, qdec_ref, kdec_ref, cg_ref, bd_ref, nw_ref,
                o_ref, st_ref):
    @pl.when(pl.program_id(1) == 0)
    def _():
        st_ref[...] = jnp.zeros_like(st_ref)

    H = B_QW // 2
    lane_head = (lax.broadcasted_iota(jnp.int32, (1, B_QW), 1) % H) // B_QSLOT
    head_masks = [lane_head == h for h in range(B_HEADS)]
    lane_real = lax.broadcasted_iota(jnp.int32, (1, B_VSLOT), 1) < B_DV
    dstack = dstack_ref[...]
    qdec, kdec = qdec_ref[...], kdec_ref[...]
    cg, bd = cg_ref[...], bd_ref[...]

    n_chunks = z_ref.shape[0] // CHUNK
    outs = []
    for c in range(n_chunks):
        rows = slice(c * CHUNK, (c + 1) * CHUNK)
        cos, sin = cos_ref[rows, :], sin_ref[rows, :]

        def rot(off):
            t1 = z_ref[rows, off:off + H].astype(F32)
            t2 = z_ref[rows, off + H:off + 2 * H].astype(F32)
            return jnp.concatenate([t1 * cos - t2 * sin, t1 * sin + t2 * cos], axis=1)

        qr = rot(0)
        kr = rot(B_QW) * (B_DK ** -0.5)
        vv = z_ref[rows, 2 * B_QW:2 * B_QW + B_VW]

        lhs = jnp.concatenate([jnp.where(m, qr, 0.0) for m in head_masks], axis=0)
        sc = _dot_nt(lhs.astype(BF16), kr.astype(BF16)) * dstack
        pv = _dot(sc.astype(BF16), vv)
        intra = jnp.concatenate(
            [pv[h * CHUNK:(h + 1) * CHUNK, h * B_VSLOT:(h + 1) * B_VSLOT] for h in range(B_HEADS)],
            axis=1)
        st = st_ref[...]
        inter = _dot_nt((qr * qdec).astype(BF16), st.astype(BF16))
        ut = _dot_tn(vv, (kr * kdec).astype(BF16))
        st_ref[...] = cg * st + ut * bd
        outs.append(intra + inter)

    o = jnp.concatenate(outs, axis=0)
    normed = []
    for h in range(B_HEADS):
        oh = o[:, h * B_VSLOT:(h + 1) * B_VSLOT]
        mu = jnp.sum(oh, axis=-1, keepdims=True) * (1.0 / B_DV)
        d = jnp.where(lane_real, oh - mu, 0.0)
        var = jnp.sum(d * d, axis=-1, keepdims=True) * (1.0 / B_DV)
        normed.append(d * lax.rsqrt(var + NORM_EPS))
    y = jnp.concatenate(normed, axis=1)
    g = z_ref[:, 2 * B_QW + B_VW:].astype(F32)
    o_ref[...] = (y * nw_ref[...] * _silu(g)).astype(BF16)


def _retention(zb, cos, sin, consts, nw, B, S):
    nt = S // TS_B
    fixed = lambda b, t: (0, 0)
    dstack, qdec, kdec, cg, bd = consts
    return pl.pallas_call(
        _ret_kernel,
        grid=(B, nt),
        in_specs=[pl.BlockSpec((TS_B, ZB_W), lambda b, t: (b * nt + t, 0)),
                  pl.BlockSpec((TS_B, B_QW // 2), lambda b, t: (t, 0)),
                  pl.BlockSpec((TS_B, B_QW // 2), lambda b, t: (t, 0)),
                  pl.BlockSpec(dstack.shape, fixed),
                  pl.BlockSpec(qdec.shape, fixed),
                  pl.BlockSpec(kdec.shape, fixed),
                  pl.BlockSpec(cg.shape, fixed),
                  pl.BlockSpec(bd.shape, fixed),
                  pl.BlockSpec((1, B_VW), fixed)],
        out_specs=pl.BlockSpec((TS_B, B_VW), lambda b, t: (b * nt + t, 0)),
        out_shape=jax.ShapeDtypeStruct((B * S, B_VW), BF16),
        scratch_shapes=[pltpu.VMEM((B_VW, B_QW), F32)],
        compiler_params=_cparams("parallel", "arbitrary"),
        name="retention",
    )(zb, cos, sin, dstack, qdec, kdec, cg, bd, nw)


def _attn_kernel(q_ref, kp_ref, kc_ref, vp_ref, vc_ref, bias_ref, o_ref, kwin_ref, vwin_ref):
    t = pl.program_id(1)
    kwin_ref[0:TQ_C, :] = kp_ref[...]
    kwin_ref[TQ_C:, :] = kc_ref[...]
    vwin_ref[0:TQ_C, :] = vp_ref[...]
    vwin_ref[TQ_C:, :] = vc_ref[...]
    jcol = lax.broadcasted_iota(jnp.int32, (1, C_BWIN), 1)
    first_head = lax.broadcasted_iota(jnp.int32, (1, 2 * C_DH), 1) < C_DH
    zero = jnp.zeros((), BF16)

    def block(i, carry):
        r0 = pl.multiple_of(i * C_QB, C_QB)
        pen = jnp.where((jcol + (r0 + (t - 1) * TQ_C)) >= 0, 0.0, MASK_VALUE)
        for p in range(C_HEADS // 2):
            lanes = slice(2 * p * C_DH, 2 * (p + 1) * C_DH)
            q = q_ref[pl.ds(r0, C_QB), lanes] * (C_DH ** -0.5)
            lhs = jnp.concatenate([jnp.where(first_head, q, zero), jnp.where(first_head, zero, q)], axis=0)

            def scores(j):
                k = kwin_ref[pl.ds(r0 + j * C_KT, C_KT), lanes]
                return (_dot_nt(lhs, k) + bias_ref[p, :, j * C_KT:(j + 1) * C_KT]
                        + pen[:, j * C_KT:(j + 1) * C_KT])

            m = scores(0)
            for j in range(1, C_BWIN // C_KT):
                m = jnp.maximum(m, scores(j))
            m = jnp.max(m, axis=-1, keepdims=True)
            acc = l = None
            for j in range(C_BWIN // C_KT):
                e = jnp.exp(scores(j) - m)
                pvj = _dot(e.astype(BF16), vwin_ref[pl.ds(r0 + j * C_KT, C_KT), lanes])
                acc = pvj if acc is None else acc + pvj
                l = e if l is None else l + e
            pv = acc / jnp.sum(l, axis=-1, keepdims=True)
            o = jnp.where(first_head, pv[:C_QB], pv[C_QB:])
            o_ref[pl.ds(r0, C_QB), lanes] = o.astype(BF16)
        return carry

    lax.fori_loop(0, TQ_C // C_QB, block, 0, unroll=True)


def _attention(zc, bias, B, S):
    nt = S // TQ_C
    cur = lambda col: (lambda b, t: (b * nt + t, col))
    prev = lambda col: (lambda b, t: (b * nt + jnp.maximum(t - 1, 0), col))
    blk = (TQ_C, C_WIDTH)
    return pl.pallas_call(
        _attn_kernel,
        grid=(B, nt),
        in_specs=[pl.BlockSpec(blk, cur(0)),
                  pl.BlockSpec(blk, prev(1)), pl.BlockSpec(blk, cur(1)),
                  pl.BlockSpec(blk, prev(2)), pl.BlockSpec(blk, cur(2)),
                  pl.BlockSpec(bias.shape, lambda b, t: (0, 0, 0))],
        out_specs=pl.BlockSpec(blk, cur(0)),
        out_shape=jax.ShapeDtypeStruct((B * S, C_WIDTH), BF16),
        scratch_shapes=[pltpu.VMEM((2 * TQ_C, C_WIDTH), BF16),
                        pltpu.VMEM((2 * TQ_C, C_WIDTH), BF16)],
        compiler_params=_cparams("parallel", "arbitrary"),
        name="chunk_attention",
    )(zc, zc, zc, zc, zc, bias)


def _route(logits):
    lane = lax.broadcasted_iota(jnp.int32, logits.shape, 1).astype(F32)
    big = float(1 << 20)
    neg = -jnp.inf

    def first_argmax(vals):
        m = jnp.max(vals, axis=-1, keepdims=True)
        idx = jnp.min(jnp.where(vals == m, lane, big), axis=-1, keepdims=True)
        return m, idx

    gl = jnp.where(lane < N_GROUPS, logits, neg)
    gm, grp = first_argmax(gl)
    p_grp = 1.0 / jnp.sum(jnp.exp(gl - gm), axis=-1, keepdims=True)
    lo = N_GROUPS + grp * EXPERTS_PER_GROUP
    el = jnp.where((lane >= lo) & (lane < lo + EXPERTS_PER_GROUP), logits, neg)
    v1, i1 = first_argmax(el)
    v2, i2 = first_argmax(jnp.where(lane == i1, neg, el))
    e2 = jnp.exp(v2 - v1)
    g1 = p_grp / (1.0 + e2)
    g2 = p_grp * e2 / (1.0 + e2)
    out = jnp.where(lane == 0, i1 - N_GROUPS, 0.0)
    out = jnp.where(lane == 1, i2 - N_GROUPS, out)
    out = jnp.where(lane == 2, g1, out)
    out = jnp.where(lane == 3, g2, out)
    return out


def _outproj_kernel(x_ref, ya_ref, yb_ref, yc_ref, w_ref, nw_ref, wr_ref, xo_ref, r_ref):
    a0, a1 = A_WIDTH, A_WIDTH + B_VW
    x = x_ref[...]
    x = x + _dot(ya_ref[...], w_ref[0:a0, :])
    x = x + _dot(yb_ref[...], w_ref[a0:a1, :])
    x = x + _dot(yc_ref[...], w_ref[a1:, :])
    xo_ref[...] = x
    h = _rms(x, nw_ref[...])
    h_hi = h.astype(BF16)
    h_lo = (h - h_hi.astype(F32)).astype(BF16)
    d_hi = _dot(h_hi, wr_ref[...])
    logits = d_hi + pltpu.roll(d_hi, ROUTE_W // 2, 1) + _dot(h_lo, wr_ref[...])
    r_ref[...] = _route(logits)


def _outproj(x2d, ya, yb, yc, w, nw, wr3):
    T = x2d.shape[0]
    row = lambda i: (i, 0)
    fixed = lambda i: (0, 0)
    return pl.pallas_call(
        _outproj_kernel,
        grid=(T // TM,),
        in_specs=[pl.BlockSpec((TM, D_MODEL), row),
                  pl.BlockSpec((TM, A_WIDTH), row),
                  pl.BlockSpec((TM, B_VW), row),
                  pl.BlockSpec((TM, C_WIDTH), row),
                  pl.BlockSpec((Y_W, D_MODEL), fixed),
                  pl.BlockSpec((1, D_MODEL), fixed),
                  pl.BlockSpec((D_MODEL, ROUTE_W), fixed)],
        out_specs=[pl.BlockSpec((TM, D_MODEL), row),
                   pl.BlockSpec((TM, ROUTE_W), row)],
        out_shape=[jax.ShapeDtypeStruct((T, D_MODEL), F32),
                   jax.ShapeDtypeStruct((T, ROUTE_W), F32)],
        compiler_params=_cparams("parallel"),
        name="outproj_router",
    )(x2d, ya, yb, yc, w, nw, wr3)


PAD_PIECES = tuple(MOE_BM >> (k + 1) for k in range(MOE_BM.bit_length() - 1))
DMA_GROUP = 8


def _tile_rows(ref, first_tile_row, n_rows=1):
    return ref.at[pl.ds(pl.multiple_of(first_tile_row, ROW_TILE), n_rows * ROW_TILE)]


def _rows_done(src_hbm, dst, sem, n_rows):
    pltpu.make_async_copy(_tile_rows(src_hbm, 0, n_rows), _tile_rows(dst, 0, n_rows), sem).wait()


def _dispatch_kernel(cnt_ref, pstart_ref, pos_ref, x_ref, nw_ref, xs_hbm, hbuf, zbuf, sem, zsem):
    i = pl.program_id(0)
    n = pl.num_programs(0)
    tm = TM_DISPATCH
    cur = lax.rem(i, 2)
    stage = hbuf.at[cur]
    _store_row_tiles(stage, _rms(x_ref[...], nw_ref[...]))

    def pad_pieces(e):
        cnt = cnt_ref[e]
        first = pstart_ref[e] + cnt
        n_pad = (-cnt) & (MOE_BM - 1)
        for p in PAD_PIECES:
            yield (n_pad & p) != 0, first + (n_pad & (MOE_BM - 2 * p)), p

    def spare_blocks():
        last = N_EXPERTS - 1
        n_used = (pstart_ref[last] + cnt_ref[last] + (MOE_BM - 1)) // MOE_BM
        n_blocks = xs_hbm.shape[0] // (MOE_BM * ROW_TILE)
        for b in range(n_blocks - N_EXPERTS, n_blocks):
            for first in range(0, MOE_BM, PAD_PIECES[0]):
                yield b >= n_used, b * MOE_BM + first, PAD_PIECES[0]

    @pl.when(i == 0)
    def _():
        zbuf[...] = jnp.zeros_like(zbuf)
        for phase in ("start", "wait"):
            for e in range(N_EXPERTS + 1):
                for fire, slot, p in (pad_pieces(e) if e < N_EXPERTS else spare_blocks()):
                    @pl.when(fire)
                    def _():
                        cp = pltpu.make_async_copy(_tile_rows(zbuf, 0, p),
                                                   _tile_rows(xs_hbm, slot * ROW_TILE, p), zsem)
                        cp.start() if phase == "start" else cp.wait()

    def body(g, carry):
        rows = [g * DMA_GROUP + k for k in range(DMA_GROUP)]
        slots = [(pos_ref[0, 0, r], pos_ref[0, 0, tm + r]) for r in rows]
        for r, (s0, s1) in zip(rows, slots):
            src = _tile_rows(stage, r * ROW_TILE)
            pltpu.make_async_copy(src, _tile_rows(xs_hbm, s0), sem.at[cur]).start(priority=0)
            pltpu.make_async_copy(src, _tile_rows(xs_hbm, s1), sem.at[cur]).start(priority=1)
        return carry
    lax.fori_loop(0, tm // DMA_GROUP, body, 0)

    @pl.when(i > 0)
    def _():
        _rows_done(stage, xs_hbm, sem.at[1 - cur], 2 * tm)

    @pl.when(i == n - 1)
    def _():
        _rows_done(stage, xs_hbm, sem.at[cur], 2 * tm)


def _dispatch(x2d, nw, pos, counts, pad_start, n_slots):
    T = x2d.shape[0]
    tm = TM_DISPATCH
    grid_spec = pltpu.PrefetchScalarGridSpec(
        num_scalar_prefetch=2,
        grid=(T // tm,),
        in_specs=[pl.BlockSpec((1, 1, 2 * tm), lambda i, c, s: (i, 0, 0), memory_space=pltpu.SMEM),
                  pl.BlockSpec((tm, D_MODEL), lambda i, c, s: (i, 0)),
                  pl.BlockSpec((1, D_MODEL), lambda i, c, s: (0, 0))],
        out_specs=pl.BlockSpec(memory_space=pl.ANY),
        scratch_shapes=[pltpu.VMEM((2, tm * ROW_TILE, LANES), F32),
                        pltpu.VMEM((PAD_PIECES[0] * ROW_TILE, LANES), F32),
                        pltpu.SemaphoreType.DMA((2,)), pltpu.SemaphoreType.DMA(())],
    )
    return pl.pallas_call(
        _dispatch_kernel,
        grid_spec=grid_spec,
        out_shape=jax.ShapeDtypeStruct((n_slots * ROW_TILE, LANES), F32),
        compiler_params=_cparams("arbitrary"),
        name="dispatch",
    )(counts, pad_start, pos, x2d, nw)


def _expert_kernel(blk_exp_ref, next_exp_ref, nused_ref, x_ref, wg_hbm, wu_hbm, wd_hbm, ys_hbm,
                   wg_f, wu_f, wd_f, wg_bf, wu_bf, wd_bf, ybuf, wsem, ysem, *, layer):
    i = pl.program_id(0)
    n = pl.num_programs(0)
    used = i < nused_ref[0]
    slot = lax.rem(i, 2)
    block_rows = MOE_BM * ROW_TILE

    def weight_copies(e):
        return (pltpu.make_async_copy(wg_hbm.at[layer, e], wg_f, wsem.at[0]),
                pltpu.make_async_copy(wu_hbm.at[layer, e], wu_f, wsem.at[1]),
                pltpu.make_async_copy(wd_hbm.at[layer, e], wd_f, wsem.at[2]))

    def result_copy(s, block):
        dst = ys_hbm.at[pl.ds(pl.multiple_of(block * block_rows, block_rows), block_rows)]
        return pltpu.make_async_copy(ybuf.at[s], dst, ysem.at[s])

    @pl.when(i == 0)
    def _():
        for cp in weight_copies(blk_exp_ref[0]):
            cp.start(priority=1)

    @pl.when((i == 0) | (blk_exp_ref[i] != blk_exp_ref[jnp.maximum(i - 1, 0)]))
    def _():
        for cp in weight_copies(blk_exp_ref[i]):
            cp.wait()
        wg_bf[...] = wg_f[...].astype(BF16)
        wu_bf[...] = wu_f[...].astype(BF16)
        wd_bf[...] = wd_f[...].astype(BF16)

        @pl.when(next_exp_ref[i] >= 0)
        def _():
            for cp in weight_copies(next_exp_ref[i]):
                cp.start(priority=1)

    @pl.when(i >= 2)
    def _():
        result_copy(slot, i - 2).wait()

    @pl.when(used)
    def _():
        xb = _load_row_tiles(x_ref, 0, MOE_BM).astype(BF16)
        act = _silu(_dot(xb, wg_bf[...])) * _dot(xb, wu_bf[...])
        _store_row_tiles(ybuf.at[slot], _dot(act.astype(BF16), wd_bf[...]))

    @pl.when(jnp.logical_not(used))
    def _():
        ybuf[slot] = jnp.zeros((block_rows, LANES), F32)

    result_copy(slot, i).start(priority=1)

    @pl.when(i == n - 1)
    def _():
        result_copy(1 - slot, i - 1).wait()
        result_copy(slot, i).wait()


def _experts(xs, blk_exp, next_exp, nused, layer, w_gate, w_up, w_down):
    nblk = blk_exp.shape[0]
    assert nblk >= 2
    last_used = lambda i, nu: jnp.minimum(i, nu[0] - 1)
    grid_spec = pltpu.PrefetchScalarGridSpec(
        num_scalar_prefetch=3,
        grid=(nblk,),
        in_specs=[pl.BlockSpec((MOE_BM * ROW_TILE, LANES), lambda i, be, nx, nu: (last_used(i, nu), 0)),
                  pl.BlockSpec(memory_space=pl.ANY),
                  pl.BlockSpec(memory_space=pl.ANY),
                  pl.BlockSpec(memory_space=pl.ANY)],
        out_specs=pl.BlockSpec(memory_space=pl.ANY),
        scratch_shapes=[pltpu.VMEM((D_MODEL, D_EXPERT), F32), pltpu.VMEM((D_MODEL, D_EXPERT), F32),
                        pltpu.VMEM((D_EXPERT, D_MODEL), F32),
                        pltpu.VMEM((D_MODEL, D_EXPERT), BF16), pltpu.VMEM((D_MODEL, D_EXPERT), BF16),
                        pltpu.VMEM((D_EXPERT, D_MODEL), BF16),
                        pltpu.VMEM((2, MOE_BM * ROW_TILE, LANES), F32),
                        pltpu.SemaphoreType.DMA((3,)), pltpu.SemaphoreType.DMA((2,))],
    )
    return pl.pallas_call(
        functools.partial(_expert_kernel, layer=layer),
        grid_spec=grid_spec,
        out_shape=jax.ShapeDtypeStruct((nblk * MOE_BM * ROW_TILE, LANES), F32),
        compiler_params=_cparams("arbitrary"),
        name="experts",
    )(blk_exp, next_exp, nused, xs, w_gate, w_up, w_down)


def _combine_rows(pos_ref, pos_next_ref, x_ref, r_ref, y_hbm, ybuf, sem, request_next_first):
    i = pl.program_id(0)
    n_rows = 2 * TM_COMBINE
    slot = lax.rem(i, 2)

    @pl.when(i == 0)
    def _():
        _request_rows(pos_ref, y_hbm, ybuf, sem, 0, unrolled=False)

    if request_next_first:
        _request_next(pos_next_ref, y_hbm, ybuf, sem, unrolled=False)
    _rows_done(y_hbm, ybuf.at[slot], sem.at[slot], n_rows)
    r = r_ref[...]
    g0, g1 = r[:, 2:3], r[:, 3:4]
    y0 = _load_row_tiles(ybuf.at[slot], 0, TM_COMBINE)
    y1 = _load_row_tiles(ybuf.at[slot], TM_COMBINE, TM_COMBINE)
    return x_ref[...] + (g0 * y0 + g1 * y1)


def _request_rows(idx_ref, y_hbm, ybuf, sem, s, unrolled):
    n_rows = 2 * TM_COMBINE
    group = 2 * DMA_GROUP

    def body(g, carry):
        rows = [g * group + k for k in range(group)]
        slots = [idx_ref[0, 0, r] for r in rows]
        for k, src_row in enumerate(slots):
            pltpu.make_async_copy(_tile_rows(y_hbm, src_row),
                                  _tile_rows(ybuf.at[s], rows[k] * ROW_TILE), sem.at[s]).start(priority=k % 2)
        return carry

    if unrolled:
        for g in range(n_rows // group):
            body(g, 0)
    else:
        lax.fori_loop(0, n_rows // group, body, 0)


def _request_next(pos_next_ref, y_hbm, ybuf, sem, unrolled):
    _request_rows(pos_next_ref, y_hbm, ybuf, sem, 1 - lax.rem(pl.program_id(0), 2), unrolled)


def _drain_last(y_hbm, ybuf, sem):
    i = pl.program_id(0)

    @pl.when(i == pl.num_programs(0) - 1)
    def _():
        other = 1 - lax.rem(i, 2)
        _rows_done(y_hbm, ybuf.at[other], sem.at[other], 2 * TM_COMBINE)


def _combine_inproj_kernel(pos_ref, pos_next_ref, x_ref, r_ref, y_hbm, nw_ref, w_ref,
                           xo_ref, za_ref, zb_ref, zc_ref, ybuf, sem):
    x = _combine_rows(pos_ref, pos_next_ref, x_ref, r_ref, y_hbm, ybuf, sem, request_next_first=False)
    xo_ref[...] = x
    _request_next(pos_next_ref, y_hbm, ybuf, sem, unrolled=True)
    _project(_rms(x, nw_ref[...]), w_ref, za_ref, zb_ref, zc_ref)
    _drain_last(y_hbm, ybuf, sem)


def _combine_final_kernel(pos_ref, pos_next_ref, x_ref, r_ref, y_hbm, nw_ref, o_ref, ybuf, sem):
    x = _combine_rows(pos_ref, pos_next_ref, x_ref, r_ref, y_hbm, ybuf, sem, request_next_first=True)
    o_ref[...] = _rms(x, nw_ref[...])
    _drain_last(y_hbm, ybuf, sem)


def _combine(x2d, route, ys, pos, nw, w=None):
    T = x2d.shape[0]
    tm = TM_COMBINE
    row = lambda i: (i, 0)
    fixed = lambda i: (0, 0)
    n_tiles = T // tm
    in_specs = [pl.BlockSpec((1, 1, 2 * tm), lambda i: (i, 0, 0), memory_space=pltpu.SMEM),
                pl.BlockSpec((1, 1, 2 * tm), lambda i: (jnp.minimum(i + 1, n_tiles - 1), 0, 0),
                             memory_space=pltpu.SMEM),
                pl.BlockSpec((tm, D_MODEL), row),
                pl.BlockSpec((tm, ROUTE_W), row),
                pl.BlockSpec(memory_space=pl.ANY),
                pl.BlockSpec((1, D_MODEL), fixed)]
    scratch = [pltpu.VMEM((2, 2 * tm * ROW_TILE, LANES), F32), pltpu.SemaphoreType.DMA((2,))]
    if w is None:
        return pl.pallas_call(
            _combine_final_kernel,
            grid=(T // tm,),
            in_specs=in_specs,
            out_specs=pl.BlockSpec((tm, D_MODEL), row),
            out_shape=jax.ShapeDtypeStruct((T, D_MODEL), F32),
            scratch_shapes=scratch,
            compiler_params=_cparams("arbitrary"),
            name="combine_final",
        )(pos, pos, x2d, route, ys, nw)
    return pl.pallas_call(
        _combine_inproj_kernel,
        grid=(T // tm,),
        in_specs=in_specs + [pl.BlockSpec((D_MODEL, ZA_W + ZB_W + ZC_W), fixed)],
        out_specs=[pl.BlockSpec((tm, D_MODEL), row),
                   pl.BlockSpec((tm, ZA_W), row),
                   pl.BlockSpec((tm, ZB_W), row),
                   pl.BlockSpec((tm, ZC_W), row)],
        out_shape=[jax.ShapeDtypeStruct((T, D_MODEL), F32),
                   jax.ShapeDtypeStruct((T, ZA_W), BF16),
                   jax.ShapeDtypeStruct((T, ZB_W), BF16),
                   jax.ShapeDtypeStruct((T, ZC_W), BF16)],
        scratch_shapes=scratch,
        compiler_params=_cparams("arbitrary"),
        name="combine_inproj",
    )(pos, pos, x2d, route, ys, nw, w)


def _pad_value_heads(t, axis):
    shape = t.shape
    t = t.reshape(shape[:axis] + (B_HEADS, B_DV) + shape[axis + 1:])
    pad = [(0, 0)] * t.ndim
    pad[axis + 1] = (0, B_VSLOT - B_DV)
    return jnp.pad(t, pad).reshape(shape[:axis] + (B_VW,) + shape[axis + 1:])


def _layout_in_weights(w):
    D = w.shape[0]
    nqk, nv = B_HEADS * B_DK, B_HEADS * B_DV
    o = ZA_W

    def qk(block):
        t = block.reshape(D, B_HEADS, 2, B_HALF).transpose(0, 2, 1, 3)
        t = jnp.pad(t, ((0, 0), (0, 0), (0, 0), (0, B_QSLOT - B_HALF)))
        return t.reshape(D, B_QW)

    return jnp.concatenate([
        w[:, :o], qk(w[:, o:o + nqk]), qk(w[:, o + nqk:o + 2 * nqk]),
        _pad_value_heads(w[:, o + 2 * nqk:o + 2 * nqk + nv], 1),
        _pad_value_heads(w[:, o + 2 * nqk + nv:o + 2 * nqk + 2 * nv], 1),
        w[:, o + 2 * nqk + 2 * nv:]], axis=1)


def _layout_out_weights(w):
    nv = B_HEADS * B_DV
    return jnp.concatenate([w[:A_WIDTH], _pad_value_heads(w[A_WIDTH:A_WIDTH + nv], 0),
                            w[A_WIDTH + nv:]], axis=0)


def _retention_tables(S):
    log_gamma = jnp.log1p(-jnp.exp2(-5.0 - jnp.arange(B_HEADS, dtype=F32)))
    idx = jnp.arange(CHUNK, dtype=F32)
    rel = idx[:, None] - idx[None, :]
    decay = jnp.where(rel >= 0, jnp.exp(log_gamma[:, None, None] * jnp.maximum(rel, 0.0)), 0.0)
    dstack = decay.reshape(B_HEADS * CHUNK, CHUNK)
    k_decay = jnp.exp(log_gamma[:, None] * (CHUNK - 1.0 - idx)[None, :])
    q_decay = jnp.exp(log_gamma[:, None] * (idx + 1.0)[None, :])
    chunk_gamma = jnp.exp(log_gamma * CHUNK)

    def lanes(per_head):
        t = jnp.repeat(per_head[..., None], B_QSLOT, axis=-1)
        t = t.reshape(per_head.shape[:-1] + (B_HEADS * B_QSLOT,))
        return jnp.concatenate([t, t], axis=-1)

    qdec = lanes(q_decay.T)
    kdec = lanes(k_decay.T)
    cg = lanes(chunk_gamma[None, :])
    row_head = np.arange(B_VW) // B_VSLOT
    lane_head = (np.arange(B_QW) % (B_QW // 2)) // B_QSLOT
    bd = jnp.asarray((row_head[:, None] == lane_head[None, :]).astype(np.float32))

    half = B_HALF
    inv_freq = ROPE_BASE ** (-jnp.arange(half, dtype=F32) / half)
    ang = jnp.arange(S).astype(F32)[:, None] * inv_freq[None, :]
    pad = ((0, 0), (0, B_QSLOT - half))
    cos = jnp.tile(jnp.pad(jnp.cos(ang), pad), (1, B_HEADS))
    sin = jnp.tile(jnp.pad(jnp.sin(ang), pad), (1, B_HEADS))
    return cos, sin, (dstack, qdec, kdec, cg, bd)


def _attention_bias(rel_bias):
    qi = np.arange(C_QB)[:, None]
    km = np.arange(C_BWIN)[None, :]
    lag = (qi // CHUNK + N_PREV_CHUNKS) - km // CHUNK
    in_band = jnp.asarray((lag >= 0) & (lag <= N_PREV_CHUNKS))
    P = C_QB + C_BWIN
    t = np.arange(P)
    t = np.where(t < C_BWIN, t, t - P)
    dist = N_PREV_CHUNKS * CHUNK - t
    row = rel_bias.astype(F32)[:, np.clip(dist, -MAX_REL, MAX_REL) + MAX_REL]
    b = jnp.tile(row, (1, C_QB))[:, :C_QB * (P - 1)].reshape(C_HEADS, C_QB, P - 1)[:, :, :C_BWIN]
    b = jnp.where(in_band[None], b, MASK_VALUE)
    return b.reshape(C_HEADS // 2, 2 * C_QB, C_BWIN)


def _hgrn_tables():
    tri = jnp.asarray(np.tril(np.ones((CHUNK, CHUNK), np.float32))).astype(BF16)
    head = np.arange(A_WIDTH) // A_DK
    same = (head[:, None] == head[None, :]).astype(np.float32)
    return tri, jnp.asarray(same).astype(BF16), jnp.asarray(same)


def _moe_plan(route, T):
    A = 2 * T
    nblk = A // MOE_BM + N_EXPERTS
    eid = route[:, 0:2].astype(jnp.int32)
    flat_e = eid.reshape(A)
    onehot = (flat_e[:, None] == jnp.arange(N_EXPERTS, dtype=jnp.int32)[None, :]).astype(jnp.int32)
    csum = jnp.cumsum(onehot, axis=0)
    counts = csum[-1]
    rank = jnp.sum((csum - onehot) * onehot, axis=1)
    padded = (counts + MOE_BM - 1) // MOE_BM * MOE_BM
    pad_end = jnp.cumsum(padded)
    pad_start = pad_end - padded
    dest = jnp.sum(onehot * pad_start[None, :], axis=1) + rank
    nused = (pad_end[-1] // MOE_BM).astype(jnp.int32)
    blk_start = jnp.arange(nblk, dtype=jnp.int32) * MOE_BM
    blk_exp = jnp.sum((pad_end[None, :] <= blk_start[:, None]).astype(jnp.int32), axis=1)
    last = jnp.take(blk_exp, jnp.maximum(nused - 1, 0))
    blk_used = jnp.arange(nblk) < nused
    blk_exp = jnp.where(blk_used, jnp.minimum(blk_exp, N_EXPERTS - 1), last)
    later = blk_used[None, :] & (blk_exp[None, :] > blk_exp[:, None])
    next_exp = jnp.min(jnp.where(later, blk_exp[None, :], N_EXPERTS), axis=1)
    next_exp = jnp.where(next_exp < N_EXPERTS, next_exp, -1).astype(jnp.int32)
    tm = TM_COMBINE
    pos = (dest * ROW_TILE).reshape(T // tm, tm, 2).transpose(0, 2, 1).reshape(T // tm, 1, 2 * tm)
    return pos, counts.astype(jnp.int32), pad_start.astype(jnp.int32), blk_exp, next_exp, nused.reshape(1)


def kernel(x, w_in, w_out, norm_mix, norm_ffn, norm_final, hgrn_lb, hgrn_norm, ret_norm, rel_bias,
           router_group, router_expert, expert_w_gate, expert_w_up, expert_w_down):
    B, S, D = x.shape
    T = B * S
    depth = w_in.shape[0]

    p = jax.nn.softmax(hgrn_lb.astype(F32), axis=0)
    lower_bounds = jnp.clip(jnp.cumsum(p, axis=0) - p[0], 0.0, 1.0)

    cos, sin, ret_consts = _retention_tables(S)
    bias = _attention_bias(rel_bias)
    tri, eones, bd_a = _hgrn_tables()

    x2d = x.reshape(T, D).astype(F32)
    route = ys = pos = None
    out = None
    for l in range(depth):
        w_in_l = _layout_in_weights(w_in[l]).astype(BF16)
        w_out_l = _layout_out_weights(w_out[l]).astype(BF16)
        ret_nw = _pad_value_heads(ret_norm[l].astype(F32), 0).reshape(1, B_VW)
        nw_mix = norm_mix[l].astype(F32).reshape(1, D)
        if l == 0:
            za, zb, zc = _inproj(x2d, nw_mix, w_in_l)
        else:
            x2d, za, zb, zc = _combine(x2d, route, ys, pos, nw_mix, w_in_l)

        ya = _hgrn(za, lower_bounds[l].reshape(1, A_WIDTH), hgrn_norm[l].astype(F32).reshape(1, A_WIDTH),
                   tri, eones, bd_a, B, S)
        yb = _retention(zb, cos, sin, ret_consts, ret_nw, B, S)
        yc = _attention(zc, bias, B, S)

        wr = jnp.concatenate([router_group[l], router_expert[l]], axis=1).astype(F32)
        wr_hi = wr.astype(BF16)
        wr_lo = (wr - wr_hi.astype(F32)).astype(BF16)
        gap = jnp.zeros((D, ROUTE_W // 2 - wr.shape[1]), BF16)
        wr2 = jnp.concatenate([wr_hi, gap, wr_lo, gap], axis=1)
        nw_ffn = norm_ffn[l].astype(F32).reshape(1, D)
        x2d, route = _outproj(x2d, ya, yb, yc, w_out_l, nw_ffn, wr2)

        pos, counts, pad_start, blk_exp, next_exp, nused = _moe_plan(route, T)
        xs = _dispatch(x2d, nw_ffn, pos, counts, pad_start, blk_exp.shape[0] * MOE_BM)
        ys = _experts(xs, blk_exp, next_exp, nused, l, expert_w_gate, expert_w_up, expert_w_down)

    out = _combine(x2d, route, ys, pos, norm_final.astype(F32).reshape(1, D))
    return out.reshape(B, S, D).astype(x.dtype)
```

```python
import functools

import numpy as np
import jax
import jax.numpy as jnp
from jax import lax
from jax.experimental import pallas as pl
from jax.experimental.pallas import tpu as pltpu

F32 = jnp.float32
BF16 = jnp.bfloat16

D_MODEL = 1024
CHUNK = 64
NORM_EPS = 1e-6
MASK_VALUE = -1e30
MIN_FORGET = 1e-30
LOG2_E = 1.4426950408889634

A_HEADS, A_DK, A_DV = 4, 64, 64
A_WIDTH = A_HEADS * A_DV
B_HEADS, B_DK, B_DV = 4, 48, 96
B_HALF = B_DK // 2
B_QSLOT = 32
B_VSLOT = 128
B_QW = 2 * B_HEADS * B_QSLOT
B_VW = B_HEADS * B_VSLOT
ROPE_BASE = 10000.0
C_HEADS, C_DH = 6, 64
C_WIDTH = C_HEADS * C_DH
N_PREV_CHUNKS = 8
MAX_REL = 128
C_QB = 2 * CHUNK
C_BWIN = C_QB + N_PREV_CHUNKS * CHUNK
C_KT = 128

ZA_W = 4 * A_WIDTH
ZB_W = 2 * B_QW + 2 * B_VW
ZC_W = 3 * C_WIDTH
Y_W = A_WIDTH + B_VW + C_WIDTH

N_GROUPS, EXPERTS_PER_GROUP = 4, 8
N_EXPERTS = N_GROUPS * EXPERTS_PER_GROUP
D_EXPERT = 512
ROUTE_W = 128

SUB = 8
VMEM_LIMIT = 56 * 1024 * 1024

TM = 512
TS_A = 512
TS_B = 512
TQ_C = 512
MOE_BM = 512
TM_COMBINE = 512
TM_DISPATCH = TM_COMBINE


def _cparams(*sem):
    return pltpu.CompilerParams(dimension_semantics=sem, vmem_limit_bytes=VMEM_LIMIT)


def _sigmoid(x):
    return 1.0 / (1.0 + jnp.exp(-x))


def _silu(x):
    return x * _sigmoid(x)


def _split3(x):
    hi = x.astype(BF16)
    r1 = x - hi.astype(F32)
    mid = r1.astype(BF16)
    lo = (r1 - mid.astype(F32)).astype(BF16)
    return hi, mid, lo


def _dot(a, b):
    return jnp.dot(a, b, preferred_element_type=F32)


def _dot_nt(a, b):
    return lax.dot_general(a, b, (((1,), (1,)), ((), ())), preferred_element_type=F32)


def _dot_tn(a, b):
    return lax.dot_general(a, b, (((0,), (0,)), ((), ())), preferred_element_type=F32)


LANES = 128
ROW_TILE = D_MODEL // LANES


def _store_row_tiles(ref, val):
    n = val.shape[0]
    for c in range(ROW_TILE):
        ref[pl.ds(c, n, stride=ROW_TILE), :] = val[:, c * LANES:(c + 1) * LANES]


def _load_row_tiles(ref, first_row, n):
    return jnp.concatenate(
        [ref[pl.ds(first_row * ROW_TILE + c, n, stride=ROW_TILE), :] for c in range(ROW_TILE)], axis=1)


def _rms(x, w):
    ms = jnp.mean(x * x, axis=-1, keepdims=True)
    return x * lax.rsqrt(ms + NORM_EPS) * w


def _project(h, w_ref, za_ref, zb_ref, zc_ref):
    hb = h.astype(BF16)
    za_ref[...] = _dot(hb, w_ref[:, 0:ZA_W]).astype(BF16)
    zb_ref[...] = _dot(hb, w_ref[:, ZA_W:ZA_W + ZB_W]).astype(BF16)
    zc_ref[...] = _dot(hb, w_ref[:, ZA_W + ZB_W:]).astype(BF16)


def _inproj_kernel(x_ref, nw_ref, w_ref, za_ref, zb_ref, zc_ref):
    _project(_rms(x_ref[...], nw_ref[...]), w_ref, za_ref, zb_ref, zc_ref)


def _inproj(x2d, nw, w):
    T = x2d.shape[0]
    row = lambda i: (i, 0)
    fixed = lambda i: (0, 0)
    return pl.pallas_call(
        _inproj_kernel,
        grid=(T // TM,),
        in_specs=[pl.BlockSpec((TM, D_MODEL), row),
                  pl.BlockSpec((1, D_MODEL), fixed),
                  pl.BlockSpec((D_MODEL, ZA_W + ZB_W + ZC_W), fixed)],
        out_specs=[pl.BlockSpec((TM, ZA_W), row),
                   pl.BlockSpec((TM, ZB_W), row),
                   pl.BlockSpec((TM, ZC_W), row)],
        out_shape=[jax.ShapeDtypeStruct((T, ZA_W), BF16),
                   jax.ShapeDtypeStruct((T, ZB_W), BF16),
                   jax.ShapeDtypeStruct((T, ZC_W), BF16)],
        compiler_params=_cparams("parallel"),
        name="inproj",
    )(x2d, nw, w)


def _hgrn_kernel(z_ref, lb_ref, nw_ref, tri_ref, eones_ref, bd_ref, o_ref, st_ref):
    @pl.when(pl.program_id(1) == 0)
    def _():
        st_ref[...] = jnp.zeros_like(st_ref)

    W = A_WIDTH
    lb = lb_ref[...]
    lane_head = lax.broadcasted_iota(jnp.int32, (1, W), 1) // A_DK
    head_masks = [lane_head == h for h in range(A_HEADS)]
    tri = tri_ref[...]
    eones = eones_ref[...]
    bd = bd_ref[...]
    jj = lax.broadcasted_iota(jnp.int32, (SUB, SUB, 1), 0)
    ii = lax.broadcasted_iota(jnp.int32, (SUB, SUB, 1), 1)
    causal = jj <= ii

    n_chunks = z_ref.shape[0] // CHUNK
    n_sub = CHUNK // SUB
    outs = []
    for c in range(n_chunks):
        r0 = c * CHUNK
        zq = z_ref[r0:r0 + CHUNK, 0:W].astype(F32)
        zf = z_ref[r0:r0 + CHUNK, W:2 * W].astype(F32)
        vv = z_ref[r0:r0 + CHUNK, 2 * W:3 * W]
        vf = vv.astype(F32)
        f = lb + (1.0 - lb) * _sigmoid(zf)
        lf = jnp.log(jnp.maximum(f, MIN_FORGET))
        kk = (1.0 - lb) * _sigmoid(-zf)
        qf = _silu(zq) * (A_DK ** -0.5)
        hi, mid, lo = _split3(lf)
        G = (_dot(tri, hi) + _dot(tri, mid) + _dot(tri, lo)) * LOG2_E
        g_last = G[CHUNK - 1:CHUNK, :]

        st = st_ref[...]
        inter = _dot_nt((qf * jnp.exp2(G)).astype(BF16), st.astype(BF16))
        khat = kk * jnp.exp2(g_last - G)
        ut = _dot_tn(vv, khat.astype(BF16))
        st_ref[...] = jnp.exp2(g_last) * st + ut * bd

        p_rows = []
        for s in range(1, n_sub):
            b = G[s * SUB - 1:s * SUB, :]
            qt = qf[s * SUB:(s + 1) * SUB, :] * jnp.exp2(G[s * SUB:(s + 1) * SUB, :] - b)
            kt = (kk[:s * SUB] * jnp.exp2(jnp.minimum(b - G[:s * SUB], 0.0))).astype(BF16)
            kt = jnp.concatenate([kt, jnp.zeros((CHUNK - s * SUB, W), BF16)], axis=0)
            lhs = jnp.concatenate([jnp.where(m, qt, 0.0) for m in head_masks], axis=0)
            p_rows.append(_dot_nt(lhs.astype(BF16), kt))
        pall = jnp.concatenate(p_rows, axis=0)
        pv = _dot(pall.astype(BF16), vv)

        pieces = []
        for s in range(n_sub):
            sl = slice(s * SUB, (s + 1) * SUB)
            gb, qb, kb, vb = G[sl], qf[sl], kk[sl], vf[sl]
            dec = jnp.exp2(jnp.minimum(gb[None, :, :] - gb[:, None, :], 0.0))
            a = (qb[None, :, :] * kb[:, None, :]) * dec
            r = _dot(a.reshape(SUB * SUB, W).astype(BF16), eones).reshape(SUB, SUB, W)
            diag = jnp.sum(jnp.where(causal, r, 0.0) * vb[:, None, :], axis=0)
            piece = inter[sl] + diag
            if s > 0:
                base = (s - 1) * A_HEADS * SUB
                for h in range(A_HEADS):
                    blk = pv[base + h * SUB:base + (h + 1) * SUB, :]
                    piece = piece + jnp.where(head_masks[h], blk, 0.0)
            pieces.append(piece)
        outs.append(jnp.concatenate(pieces, axis=0))

    o = jnp.concatenate(outs, axis=0)
    g = z_ref[:, 3 * W:4 * W].astype(F32)
    o_ref[...] = (_rms(o, nw_ref[...]) * _silu(g)).astype(BF16)


def _hgrn(za, lb, nw, tri, eones, bd, B, S):
    nt = S // TS_A
    fixed = lambda b, t: (0, 0)
    return pl.pallas_call(
        _hgrn_kernel,
        grid=(B, nt),
        in_specs=[pl.BlockSpec((TS_A, ZA_W), lambda b, t: (b * nt + t, 0)),
                  pl.BlockSpec((1, A_WIDTH), fixed),
                  pl.BlockSpec((1, A_WIDTH), fixed),
                  pl.BlockSpec((CHUNK, CHUNK), fixed),
                  pl.BlockSpec((A_WIDTH, A_WIDTH), fixed),
                  pl.BlockSpec((A_WIDTH, A_WIDTH), fixed)],
        out_specs=pl.BlockSpec((TS_A, A_WIDTH), lambda b, t: (b * nt + t, 0)),
        out_shape=jax.ShapeDtypeStruct((B * S, A_WIDTH), BF16),
        scratch_shapes=[pltpu.VMEM((A_WIDTH, A_WIDTH), F32)],
        compiler_params=_cparams("parallel", "arbitrary"),
        name="hgrn2",
    )(za, lb, nw, tri, eones, bd)


def _ret_kernel(z_ref, cos_ref, sin_ref, dstack_ref, qdec_ref, kdec_ref, cg_ref, bd_ref, nw_ref,
                o_ref, st_ref):
    @pl.when(pl.program_id(1) == 0)
    def _():
        st_ref[...] = jnp.zeros_like(st_ref)

    H = B_QW // 2
    lane_head = (lax.broadcasted_iota(jnp.int32, (1, B_QW), 1) % H) // B_QSLOT
    head_masks = [lane_head == h for h in range(B_HEADS)]
    lane_real = lax.broadcasted_iota(jnp.int32, (1, B_VSLOT), 1) < B_DV
    dstack = dstack_ref[...]
    qdec, kdec = qdec_ref[...], kdec_ref[...]
    cg, bd = cg_ref[...], bd_ref[...]

    n_chunks = z_ref.shape[0] // CHUNK
    outs = []
    for c in range(n_chunks):
        rows = slice(c * CHUNK, (c + 1) * CHUNK)
        cos, sin = cos_ref[rows, :], sin_ref[rows, :]

        def rot(off):
            t1 = z_ref[rows, off:off + H].astype(F32)
            t2 = z_ref[rows, off + H:off + 2 * H].astype(F32)
            return jnp.concatenate([t1 * cos - t2 * sin, t1 * sin + t2 * cos], axis=1)

        qr = rot(0)
        kr = rot(B_QW) * (B_DK ** -0.5)
        vv = z_ref[rows, 2 * B_QW:2 * B_QW + B_VW]

        lhs = jnp.concatenate([jnp.where(m, qr, 0.0) for m in head_masks], axis=0)
        sc = _dot_nt(lhs.astype(BF16), kr.astype(BF16)) * dstack
        pv = _dot(sc.astype(BF16), vv)
        intra = jnp.concatenate(
            [pv[h * CHUNK:(h + 1) * CHUNK, h * B_VSLOT:(h + 1) * B_VSLOT] for h in range(B_HEADS)],
            axis=1)
        st = st_ref[...]
        inter = _dot_nt((qr * qdec).astype(BF16), st.astype(BF16))
        ut = _dot_tn(vv, (kr * kdec).astype(BF16))
        st_ref[...] = cg * st + ut * bd
        outs.append(intra + inter)

    o = jnp.concatenate(outs, axis=0)
    normed = []
    for h in range(B_HEADS):
        oh = o[:, h * B_VSLOT:(h + 1) * B_VSLOT]
        mu = jnp.sum(oh, axis=-1, keepdims=True) * (1.0 / B_DV)
        d = jnp.where(lane_real, oh - mu, 0.0)
        var = jnp.sum(d * d, axis=-1, keepdims=True) * (1.0 / B_DV)
        normed.append(d * lax.rsqrt(var + NORM_EPS))
    y = jnp.concatenate(normed, axis=1)
    g = z_ref[:, 2 * B_QW + B_VW:].astype(F32)
    o_ref[...] = (y * nw_ref[...] * _silu(g)).astype(BF16)


def _retention(zb, cos, sin, consts, nw, B, S):
    nt = S // TS_B
    fixed = lambda b, t: (0, 0)
    dstack, qdec, kdec, cg, bd = consts
    return pl.pallas_call(
        _ret_kernel,
        grid=(B, nt),
        in_specs=[pl.BlockSpec((TS_B, ZB_W), lambda b, t: (b * nt + t, 0)),
                  pl.BlockSpec((TS_B, B_QW // 2), lambda b, t: (t, 0)),
                  pl.BlockSpec((TS_B, B_QW // 2), lambda b, t: (t, 0)),
                  pl.BlockSpec(dstack.shape, fixed),
                  pl.BlockSpec(qdec.shape, fixed),
                  pl.BlockSpec(kdec.shape, fixed),
                  pl.BlockSpec(cg.shape, fixed),
                  pl.BlockSpec(bd.shape, fixed),
                  pl.BlockSpec((1, B_VW), fixed)],
        out_specs=pl.BlockSpec((TS_B, B_VW), lambda b, t: (b * nt + t, 0)),
        out_shape=jax.ShapeDtypeStruct((B * S, B_VW), BF16),
        scratch_shapes=[pltpu.VMEM((B_VW, B_QW), F32)],
        compiler_params=_cparams("parallel", "arbitrary"),
        name="retention",
    )(zb, cos, sin, dstack, qdec, kdec, cg, bd, nw)


def _attn_kernel(q_ref, kp_ref, kc_ref, vp_ref, vc_ref, bias_ref, o_ref, kwin_ref, vwin_ref):
    t = pl.program_id(1)
    kwin_ref[0:TQ_C, :] = kp_ref[...]
    kwin_ref[TQ_C:, :] = kc_ref[...]
    vwin_ref[0:TQ_C, :] = vp_ref[...]
    vwin_ref[TQ_C:, :] = vc_ref[...]
    jcol = lax.broadcasted_iota(jnp.int32, (1, C_BWIN), 1)
    first_head = lax.broadcasted_iota(jnp.int32, (1, 2 * C_DH), 1) < C_DH
    zero = jnp.zeros((), BF16)

    def block(i, carry):
        r0 = pl.multiple_of(i * C_QB, C_QB)
        pen = jnp.where((jcol + (r0 + (t - 1) * TQ_C)) >= 0, 0.0, MASK_VALUE)
        for p in range(C_HEADS // 2):
            lanes = slice(2 * p * C_DH, 2 * (p + 1) * C_DH)
            q = q_ref[pl.ds(r0, C_QB), lanes] * (C_DH ** -0.5)
            lhs = jnp.concatenate([jnp.where(first_head, q, zero), jnp.where(first_head, zero, q)], axis=0)

            def scores(j):
                k = kwin_ref[pl.ds(r0 + j * C_KT, C_KT), lanes]
                return (_dot_nt(lhs, k) + bias_ref[p, :, j * C_KT:(j + 1) * C_KT]
                        + pen[:, j * C_KT:(j + 1) * C_KT])

            m = scores(0)
            for j in range(1, C_BWIN // C_KT):
                m = jnp.maximum(m, scores(j))
            m = jnp.max(m, axis=-1, keepdims=True)
            acc = l = None
            for j in range(C_BWIN // C_KT):
                e = jnp.exp(scores(j) - m)
                pvj = _dot(e.astype(BF16), vwin_ref[pl.ds(r0 + j * C_KT, C_KT), lanes])
                acc = pvj if acc is None else acc + pvj
                l = e if l is None else l + e
            pv = acc / jnp.sum(l, axis=-1, keepdims=True)
            o = jnp.where(first_head, pv[:C_QB], pv[C_QB:])
            o_ref[pl.ds(r0, C_QB), lanes] = o.astype(BF16)
        return carry

    lax.fori_loop(0, TQ_C // C_QB, block, 0, unroll=True)


def _attention(zc, bias, B, S):
    nt = S // TQ_C
    cur = lambda col: (lambda b, t: (b * nt + t, col))
    prev = lambda col: (lambda b, t: (b * nt + jnp.maximum(t - 1, 0), col))
    blk = (TQ_C, C_WIDTH)
    return pl.pallas_call(
        _attn_kernel,
        grid=(B, nt),
        in_specs=[pl.BlockSpec(blk, cur(0)),
                  pl.BlockSpec(blk, prev(1)), pl.BlockSpec(blk, cur(1)),
                  pl.BlockSpec(blk, prev(2)), pl.BlockSpec(blk, cur(2)),
                  pl.BlockSpec(bias.shape, lambda b, t: (0, 0, 0))],
        out_specs=pl.BlockSpec(blk, cur(0)),
        out_shape=jax.ShapeDtypeStruct((B * S, C_WIDTH), BF16),
        scratch_shapes=[pltpu.VMEM((2 * TQ_C, C_WIDTH), BF16),
                        pltpu.VMEM((2 * TQ_C, C_WIDTH), BF16)],
        compiler_params=_cparams("parallel", "arbitrary"),
        name="chunk_attention",
    )(zc, zc, zc, zc, zc, bias)


def _route(logits):
    lane = lax.broadcasted_iota(jnp.int32, logits.shape, 1).astype(F32)
    big = float(1 << 20)
    neg = -jnp.inf

    def first_argmax(vals):
        m = jnp.max(vals, axis=-1, keepdims=True)
        idx = jnp.min(jnp.where(vals == m, lane, big), axis=-1, keepdims=True)
        return m, idx

    gl = jnp.where(lane < N_GROUPS, logits, neg)
    gm, grp = first_argmax(gl)
    p_grp = 1.0 / jnp.sum(jnp.exp(gl - gm), axis=-1, keepdims=True)
    lo = N_GROUPS + grp * EXPERTS_PER_GROUP
    el = jnp.where((lane >= lo) & (lane < lo + EXPERTS_PER_GROUP), logits, neg)
    v1, i1 = first_argmax(el)
    v2, i2 = first_argmax(jnp.where(lane == i1, neg, el))
    e2 = jnp.exp(v2 - v1)
    g1 = p_grp / (1.0 + e2)
    g2 = p_grp * e2 / (1.0 + e2)
    out = jnp.where(lane == 0, i1 - N_GROUPS, 0.0)
    out = jnp.where(lane == 1, i2 - N_GROUPS, out)
    out = jnp.where(lane == 2, g1, out)
    out = jnp.where(lane == 3, g2, out)
    return out


def _outproj_kernel(x_ref, ya_ref, yb_ref, yc_ref, w_ref, nw_ref, wr_ref, xo_ref, r_ref):
    a0, a1 = A_WIDTH, A_WIDTH + B_VW
    x = x_ref[...]
    x = x + _dot(ya_ref[...], w_ref[0:a0, :])
    x = x + _dot(yb_ref[...], w_ref[a0:a1, :])
    x = x + _dot(yc_ref[...], w_ref[a1:, :])
    xo_ref[...] = x
    h = _rms(x, nw_ref[...])
    h_hi = h.astype(BF16)
    h_lo = (h - h_hi.astype(F32)).astype(BF16)
    d_hi = _dot(h_hi, wr_ref[...])
    logits = d_hi + pltpu.roll(d_hi, ROUTE_W // 2, 1) + _dot(h_lo, wr_ref[...])
    r_ref[...] = _route(logits)


def _outproj(x2d, ya, yb, yc, w, nw, wr3):
    T = x2d.shape[0]
    row = lambda i: (i, 0)
    fixed = lambda i: (0, 0)
    return pl.pallas_call(
        _outproj_kernel,
        grid=(T // TM,),
        in_specs=[pl.BlockSpec((TM, D_MODEL), row),
                  pl.BlockSpec((TM, A_WIDTH), row),
                  pl.BlockSpec((TM, B_VW), row),
                  pl.BlockSpec((TM, C_WIDTH), row),
                  pl.BlockSpec((Y_W, D_MODEL), fixed),
                  pl.BlockSpec((1, D_MODEL), fixed),
                  pl.BlockSpec((D_MODEL, ROUTE_W), fixed)],
        out_specs=[pl.BlockSpec((TM, D_MODEL), row),
                   pl.BlockSpec((TM, ROUTE_W), row)],
        out_shape=[jax.ShapeDtypeStruct((T, D_MODEL), F32),
                   jax.ShapeDtypeStruct((T, ROUTE_W), F32)],
        compiler_params=_cparams("parallel"),
        name="outproj_router",
    )(x2d, ya, yb, yc, w, nw, wr3)


PAD_PIECES = tuple(MOE_BM >> (k + 1) for k in range(MOE_BM.bit_length() - 1))
DMA_GROUP = 8


def _tile_rows(ref, first_tile_row, n_rows=1):
    return ref.at[pl.ds(pl.multiple_of(first_tile_row, ROW_TILE), n_rows * ROW_TILE)]


def _rows_done(src_hbm, dst, sem, n_rows):
    pltpu.make_async_copy(_tile_rows(src_hbm, 0, n_rows), _tile_rows(dst, 0, n_rows), sem).wait()


def _dispatch_kernel(cnt_ref, pstart_ref, pos_ref, x_ref, nw_ref, xs_hbm, hbuf, zbuf, sem, zsem):
    i = pl.program_id(0)
    n = pl.num_programs(0)
    tm = TM_DISPATCH
    cur = lax.rem(i, 2)
    stage = hbuf.at[cur]
    _store_row_tiles(stage, _rms(x_ref[...], nw_ref[...]))

    def pad_pieces(e):
        cnt = cnt_ref[e]
        first = pstart_ref[e] + cnt
        n_pad = (-cnt) & (MOE_BM - 1)
        for p in PAD_PIECES:
            yield (n_pad & p) != 0, first + (n_pad & (MOE_BM - 2 * p)), p

    def spare_blocks():
        last = N_EXPERTS - 1
        n_used = (pstart_ref[last] + cnt_ref[last] + (MOE_BM - 1)) // MOE_BM
        n_blocks = xs_hbm.shape[0] // (MOE_BM * ROW_TILE)
        for b in range(n_blocks - N_EXPERTS, n_blocks):
            for first in range(0, MOE_BM, PAD_PIECES[0]):
                yield b >= n_used, b * MOE_BM + first, PAD_PIECES[0]

    @pl.when(i == 0)
    def _():
        zbuf[...] = jnp.zeros_like(zbuf)
        for phase in ("start", "wait"):
            for e in range(N_EXPERTS + 1):
                for fire, slot, p in (pad_pieces(e) if e < N_EXPERTS else spare_blocks()):
                    @pl.when(fire)
                    def _():
                        cp = pltpu.make_async_copy(_tile_rows(zbuf, 0, p),
                                                   _tile_rows(xs_hbm, slot * ROW_TILE, p), zsem)
                        cp.start() if phase == "start" else cp.wait()

    def body(g, carry):
        rows = [g * DMA_GROUP + k for k in range(DMA_GROUP)]
        slots = [(pos_ref[0, 0, r], pos_ref[0, 0, tm + r]) for r in rows]
        for r, (s0, s1) in zip(rows, slots):
            src = _tile_rows(stage, r * ROW_TILE)
            pltpu.make_async_copy(src, _tile_rows(xs_hbm, s0), sem.at[cur]).start(priority=0)
            pltpu.make_async_copy(src, _tile_rows(xs_hbm, s1), sem.at[cur]).start(priority=1)
        return carry
    lax.fori_loop(0, tm // DMA_GROUP, body, 0)

    @pl.when(i > 0)
    def _():
        _rows_done(stage, xs_hbm, sem.at[1 - cur], 2 * tm)

    @pl.when(i == n - 1)
    def _():
        _rows_done(stage, xs_hbm, sem.at[cur], 2 * tm)


def _dispatch(x2d, nw, pos, counts, pad_start, n_slots):
    T = x2d.shape[0]
    tm = TM_DISPATCH
    grid_spec = pltpu.PrefetchScalarGridSpec(
        num_scalar_prefetch=2,
        grid=(T // tm,),
        in_specs=[pl.BlockSpec((1, 1, 2 * tm), lambda i, c, s: (i, 0, 0), memory_space=pltpu.SMEM),
                  pl.BlockSpec((tm, D_MODEL), lambda i, c, s: (i, 0)),
                  pl.BlockSpec((1, D_MODEL), lambda i, c, s: (0, 0))],
        out_specs=pl.BlockSpec(memory_space=pl.ANY),
        scratch_shapes=[pltpu.VMEM((2, tm * ROW_TILE, LANES), F32),
                        pltpu.VMEM((PAD_PIECES[0] * ROW_TILE, LANES), F32),
                        pltpu.SemaphoreType.DMA((2,)), pltpu.SemaphoreType.DMA(())],
    )
    return pl.pallas_call(
        _dispatch_kernel,
        grid_spec=grid_spec,
        out_shape=jax.ShapeDtypeStruct((n_slots * ROW_TILE, LANES), F32),
        compiler_params=_cparams("arbitrary"),
        name="dispatch",
    )(counts, pad_start, pos, x2d, nw)


def _expert_kernel(blk_exp_ref, next_exp_ref, nused_ref, x_ref, wg_hbm, wu_hbm, wd_hbm, ys_hbm,
                   wg_f, wu_f, wd_f, wg_bf, wu_bf, wd_bf, ybuf, wsem, ysem, *, layer):
    i = pl.program_id(0)
    n = pl.num_programs(0)
    used = i < nused_ref[0]
    slot = lax.rem(i, 2)
    block_rows = MOE_BM * ROW_TILE

    def weight_copies(e):
        return (pltpu.make_async_copy(wg_hbm.at[layer, e], wg_f, wsem.at[0]),
                pltpu.make_async_copy(wu_hbm.at[layer, e], wu_f, wsem.at[1]),
                pltpu.make_async_copy(wd_hbm.at[layer, e], wd_f, wsem.at[2]))

    def result_copy(s, block):
        dst = ys_hbm.at[pl.ds(pl.multiple_of(block * block_rows, block_rows), block_rows)]
        return pltpu.make_async_copy(ybuf.at[s], dst, ysem.at[s])

    @pl.when(i == 0)
    def _():
        for cp in weight_copies(blk_exp_ref[0]):
            cp.start(priority=1)

    @pl.when((i == 0) | (blk_exp_ref[i] != blk_exp_ref[jnp.maximum(i - 1, 0)]))
    def _():
        for cp in weight_copies(blk_exp_ref[i]):
            cp.wait()
        wg_bf[...] = wg_f[...].astype(BF16)
        wu_bf[...] = wu_f[...].astype(BF16)
        wd_bf[...] = wd_f[...].astype(BF16)

        @pl.when(next_exp_ref[i] >= 0)
        def _():
            for cp in weight_copies(next_exp_ref[i]):
                cp.start(priority=1)

    @pl.when(i >= 2)
    def _():
        result_copy(slot, i - 2).wait()

    @pl.when(used)
    def _():
        xb = _load_row_tiles(x_ref, 0, MOE_BM).astype(BF16)
        act = _silu(_dot(xb, wg_bf[...])) * _dot(xb, wu_bf[...])
        _store_row_tiles(ybuf.at[slot], _dot(act.astype(BF16), wd_bf[...]))

    @pl.when(jnp.logical_not(used))
    def _():
        ybuf[slot] = jnp.zeros((block_rows, LANES), F32)

    result_copy(slot, i).start(priority=1)

    @pl.when(i == n - 1)
    def _():
        result_copy(1 - slot, i - 1).wait()
        result_copy(slot, i).wait()


def _experts(xs, blk_exp, next_exp, nused, layer, w_gate, w_up, w_down):
    nblk = blk_exp.shape[0]
    assert nblk >= 2
    last_used = lambda i, nu: jnp.minimum(i, nu[0] - 1)
    grid_spec = pltpu.PrefetchScalarGridSpec(
        num_scalar_prefetch=3,
        grid=(nblk,),
        in_specs=[pl.BlockSpec((MOE_BM * ROW_TILE, LANES), lambda i, be, nx, nu: (last_used(i, nu), 0)),
                  pl.BlockSpec(memory_space=pl.ANY),
                  pl.BlockSpec(memory_space=pl.ANY),
                  pl.BlockSpec(memory_space=pl.ANY)],
        out_specs=pl.BlockSpec(memory_space=pl.ANY),
        scratch_shapes=[pltpu.VMEM((D_MODEL, D_EXPERT), F32), pltpu.VMEM((D_MODEL, D_EXPERT), F32),
                        pltpu.VMEM((D_EXPERT, D_MODEL), F32),
                        pltpu.VMEM((D_MODEL, D_EXPERT), BF16), pltpu.VMEM((D_MODEL, D_EXPERT), BF16),
                        pltpu.VMEM((D_EXPERT, D_MODEL), BF16),
                        pltpu.VMEM((2, MOE_BM * ROW_TILE, LANES), F32),
                        pltpu.SemaphoreType.DMA((3,)), pltpu.SemaphoreType.DMA((2,))],
    )
    return pl.pallas_call(
        functools.partial(_expert_kernel, layer=layer),
        grid_spec=grid_spec,
        out_shape=jax.ShapeDtypeStruct((nblk * MOE_BM * ROW_TILE, LANES), F32),
        compiler_params=_cparams("arbitrary"),
        name="experts",
    )(blk_exp, next_exp, nused, xs, w_gate, w_up, w_down)


def _combine_rows(pos_ref, pos_next_ref, x_ref, r_ref, y_hbm, ybuf, sem, request_next_first):
    i = pl.program_id(0)
    n_rows = 2 * TM_COMBINE
    slot = lax.rem(i, 2)

    @pl.when(i == 0)
    def _():
        _request_rows(pos_ref, y_hbm, ybuf, sem, 0, unrolled=False)

    if request_next_first:
        _request_next(pos_next_ref, y_hbm, ybuf, sem, unrolled=False)
    _rows_done(y_hbm, ybuf.at[slot], sem.at[slot], n_rows)
    r = r_ref[...]
    g0, g1 = r[:, 2:3], r[:, 3:4]
    y0 = _load_row_tiles(ybuf.at[slot], 0, TM_COMBINE)
    y1 = _load_row_tiles(ybuf.at[slot], TM_COMBINE, TM_COMBINE)
    return x_ref[...] + (g0 * y0 + g1 * y1)


def _request_rows(idx_ref, y_hbm, ybuf, sem, s, unrolled):
    n_rows = 2 * TM_COMBINE
    group = 2 * DMA_GROUP

    def body(g, carry):
        rows = [g * group + k for k in range(group)]
        slots = [idx_ref[0, 0, r] for r in rows]
        for k, src_row in enumerate(slots):
            pltpu.make_async_copy(_tile_rows(y_hbm, src_row),
                                  _tile_rows(ybuf.at[s], rows[k] * ROW_TILE), sem.at[s]).start(priority=k % 2)
        return carry

    if unrolled:
        for g in range(n_rows // group):
            body(g, 0)
    else:
        lax.fori_loop(0, n_rows // group, body, 0)


def _request_next(pos_next_ref, y_hbm, ybuf, sem, unrolled):
    _request_rows(pos_next_ref, y_hbm, ybuf, sem, 1 - lax.rem(pl.program_id(0), 2), unrolled)


def _drain_last(y_hbm, ybuf, sem):
    i = pl.program_id(0)

    @pl.when(i == pl.num_programs(0) - 1)
    def _():
        other = 1 - lax.rem(i, 2)
        _rows_done(y_hbm, ybuf.at[other], sem.at[other], 2 * TM_COMBINE)


def _combine_inproj_kernel(pos_ref, pos_next_ref, x_ref, r_ref, y_hbm, nw_ref, w_ref,
                           xo_ref, za_ref, zb_ref, zc_ref, ybuf, sem):
    x = _combine_rows(pos_ref, pos_next_ref, x_ref, r_ref, y_hbm, ybuf, sem, request_next_first=False)
    xo_ref[...] = x
    _request_next(pos_next_ref, y_hbm, ybuf, sem, unrolled=True)
    _project(_rms(x, nw_ref[...]), w_ref, za_ref, zb_ref, zc_ref)
    _drain_last(y_hbm, ybuf, sem)


def _combine_final_kernel(pos_ref, pos_next_ref, x_ref, r_ref, y_hbm, nw_ref, o_ref, ybuf, sem):
    x = _combine_rows(pos_ref, pos_next_ref, x_ref, r_ref, y_hbm, ybuf, sem, request_next_first=True)
    o_ref[...] = _rms(x, nw_ref[...])
    _drain_last(y_hbm, ybuf, sem)


def _combine(x2d, route, ys, pos, nw, w=None):
    T = x2d.shape[0]
    tm = TM_COMBINE
    row = lambda i: (i, 0)
    fixed = lambda i: (0, 0)
    n_tiles = T // tm
    in_specs = [pl.BlockSpec((1, 1, 2 * tm), lambda i: (i, 0, 0), memory_space=pltpu.SMEM),
                pl.BlockSpec((1, 1, 2 * tm), lambda i: (jnp.minimum(i + 1, n_tiles - 1), 0, 0),
                             memory_space=pltpu.SMEM),
                pl.BlockSpec((tm, D_MODEL), row),
                pl.BlockSpec((tm, ROUTE_W), row),
                pl.BlockSpec(memory_space=pl.ANY),
                pl.BlockSpec((1, D_MODEL), fixed)]
    scratch = [pltpu.VMEM((2, 2 * tm * ROW_TILE, LANES), F32), pltpu.SemaphoreType.DMA((2,))]
    if w is None:
        return pl.pallas_call(
            _combine_final_kernel,
            grid=(T // tm,),
            in_specs=in_specs,
            out_specs=pl.BlockSpec((tm, D_MODEL), row),
            out_shape=jax.ShapeDtypeStruct((T, D_MODEL), F32),
            scratch_shapes=scratch,
            compiler_params=_cparams("arbitrary"),
            name="combine_final",
        )(pos, pos, x2d, route, ys, nw)
    return pl.pallas_call(
        _combine_inproj_kernel,
        grid=(T // tm,),
        in_specs=in_specs + [pl.BlockSpec((D_MODEL, ZA_W + ZB_W + ZC_W), fixed)],
        out_specs=[pl.BlockSpec((tm, D_MODEL), row),
                   pl.BlockSpec((tm, ZA_W), row),
                   pl.BlockSpec((tm, ZB_W), row),
                   pl.BlockSpec((tm, ZC_W), row)],
        out_shape=[jax.ShapeDtypeStruct((T, D_MODEL), F32),
                   jax.ShapeDtypeStruct((T, ZA_W), BF16),
                   jax.ShapeDtypeStruct((T, ZB_W), BF16),
                   jax.ShapeDtypeStruct((T, ZC_W), BF16)],
        scratch_shapes=scratch,
        compiler_params=_cparams("arbitrary"),
        name="combine_inproj",
    )(pos, pos, x2d, route, ys, nw, w)


def _pad_value_heads(t, axis):
    shape = t.shape
    t = t.reshape(shape[:axis] + (B_HEADS, B_DV) + shape[axis + 1:])
    pad = [(0, 0)] * t.ndim
    pad[axis + 1] = (0, B_VSLOT - B_DV)
    return jnp.pad(t, pad).reshape(shape[:axis] + (B_VW,) + shape[axis + 1:])


def _layout_in_weights(w):
    D = w.shape[0]
    nqk, nv = B_HEADS * B_DK, B_HEADS * B_DV
    o = ZA_W

    def qk(block):
        t = block.reshape(D, B_HEADS, 2, B_HALF).transpose(0, 2, 1, 3)
        t = jnp.pad(t, ((0, 0), (0, 0), (0, 0), (0, B_QSLOT - B_HALF)))
        return t.reshape(D, B_QW)

    return jnp.concatenate([
        w[:, :o], qk(w[:, o:o + nqk]), qk(w[:, o + nqk:o + 2 * nqk]),
        _pad_value_heads(w[:, o + 2 * nqk:o + 2 * nqk + nv], 1),
        _pad_value_heads(w[:, o + 2 * nqk + nv:o + 2 * nqk + 2 * nv], 1),
        w[:, o + 2 * nqk + 2 * nv:]], axis=1)


def _layout_out_weights(w):
    nv = B_HEADS * B_DV
    return jnp.concatenate([w[:A_WIDTH], _pad_value_heads(w[A_WIDTH:A_WIDTH + nv], 0),
                            w[A_WIDTH + nv:]], axis=0)


def _retention_tables(S):
    log_gamma = jnp.log1p(-jnp.exp2(-5.0 - jnp.arange(B_HEADS, dtype=F32)))
    idx = jnp.arange(CHUNK, dtype=F32)
    rel = idx[:, None] - idx[None, :]
    decay = jnp.where(rel >= 0, jnp.exp(log_gamma[:, None, None] * jnp.maximum(rel, 0.0)), 0.0)
    dstack = decay.reshape(B_HEADS * CHUNK, CHUNK)
    k_decay = jnp.exp(log_gamma[:, None] * (CHUNK - 1.0 - idx)[None, :])
    q_decay = jnp.exp(log_gamma[:, None] * (idx + 1.0)[None, :])
    chunk_gamma = jnp.exp(log_gamma * CHUNK)

    def lanes(per_head):
        t = jnp.repeat(per_head[..., None], B_QSLOT, axis=-1)
        t = t.reshape(per_head.shape[:-1] + (B_HEADS * B_QSLOT,))
        return jnp.concatenate([t, t], axis=-1)

    qdec = lanes(q_decay.T)
    kdec = lanes(k_decay.T)
    cg = lanes(chunk_gamma[None, :])
    row_head = np.arange(B_VW) // B_VSLOT
    lane_head = (np.arange(B_QW) % (B_QW // 2)) // B_QSLOT
    bd = jnp.asarray((row_head[:, None] == lane_head[None, :]).astype(np.float32))

    half = B_HALF
    inv_freq = ROPE_BASE ** (-jnp.arange(half, dtype=F32) / half)
    ang = jnp.arange(S).astype(F32)[:, None] * inv_freq[None, :]
    pad = ((0, 0), (0, B_QSLOT - half))
    cos = jnp.tile(jnp.pad(jnp.cos(ang), pad), (1, B_HEADS))
    sin = jnp.tile(jnp.pad(jnp.sin(ang), pad), (1, B_HEADS))
    return cos, sin, (dstack, qdec, kdec, cg, bd)


def _attention_bias(rel_bias):
    qi = np.arange(C_QB)[:, None]
    km = np.arange(C_BWIN)[None, :]
    lag = (qi // CHUNK + N_PREV_CHUNKS) - km // CHUNK
    in_band = jnp.asarray((lag >= 0) & (lag <= N_PREV_CHUNKS))
    P = C_QB + C_BWIN
    t = np.arange(P)
    t = np.where(t < C_BWIN, t, t - P)
    dist = N_PREV_CHUNKS * CHUNK - t
    row = rel_bias.astype(F32)[:, np.clip(dist, -MAX_REL, MAX_REL) + MAX_REL]
    b = jnp.tile(row, (1, C_QB))[:, :C_QB * (P - 1)].reshape(C_HEADS, C_QB, P - 1)[:, :, :C_BWIN]
    b = jnp.where(in_band[None], b, MASK_VALUE)
    return b.reshape(C_HEADS // 2, 2 * C_QB, C_BWIN)


def _hgrn_tables():
    tri = jnp.asarray(np.tril(np.ones((CHUNK, CHUNK), np.float32))).astype(BF16)
    head = np.arange(A_WIDTH) // A_DK
    same = (head[:, None] == head[None, :]).astype(np.float32)
    return tri, jnp.asarray(same).astype(BF16), jnp.asarray(same)


def _moe_plan(route, T):
    A = 2 * T
    nblk = A // MOE_BM + N_EXPERTS
    eid = route[:, 0:2].astype(jnp.int32)
    flat_e = eid.reshape(A)
    onehot = (flat_e[:, None] == jnp.arange(N_EXPERTS, dtype=jnp.int32)[None, :]).astype(jnp.int32)
    csum = jnp.cumsum(onehot, axis=0)
    counts = csum[-1]
    rank = jnp.sum((csum - onehot) * onehot, axis=1)
    padded = (counts + MOE_BM - 1) // MOE_BM * MOE_BM
    pad_end = jnp.cumsum(padded)
    pad_start = pad_end - padded
    dest = jnp.sum(onehot * pad_start[None, :], axis=1) + rank
    nused = (pad_end[-1] // MOE_BM).astype(jnp.int32)
    blk_start = jnp.arange(nblk, dtype=jnp.int32) * MOE_BM
    blk_exp = jnp.sum((pad_end[None, :] <= blk_start[:, None]).astype(jnp.int32), axis=1)
    last = jnp.take(blk_exp, jnp.maximum(nused - 1, 0))
    blk_used = jnp.arange(nblk) < nused
    blk_exp = jnp.where(blk_used, jnp.minimum(blk_exp, N_EXPERTS - 1), last)
    later = blk_used[None, :] & (blk_exp[None, :] > blk_exp[:, None])
    next_exp = jnp.min(jnp.where(later, blk_exp[None, :], N_EXPERTS), axis=1)
    next_exp = jnp.where(next_exp < N_EXPERTS, next_exp, -1).astype(jnp.int32)
    tm = TM_COMBINE
    pos = (dest * ROW_TILE).reshape(T // tm, tm, 2).transpose(0, 2, 1).reshape(T // tm, 1, 2 * tm)
    return pos, counts.astype(jnp.int32), pad_start.astype(jnp.int32), blk_exp, next_exp, nused.reshape(1)


def kernel(x, w_in, w_out, norm_mix, norm_ffn, norm_final, hgrn_lb, hgrn_norm, ret_norm, rel_bias,
           router_group, router_expert, expert_w_gate, expert_w_up, expert_w_down):
    B, S, D = x.shape
    T = B * S
    depth = w_in.shape[0]

    p = jax.nn.softmax(hgrn_lb.astype(F32), axis=0)
    lower_bounds = jnp.clip(jnp.cumsum(p, axis=0) - p[0], 0.0, 1.0)

    cos, sin, ret_consts = _retention_tables(S)
    bias = _attention_bias(rel_bias)
    tri, eones, bd_a = _hgrn_tables()

    x2d = x.reshape(T, D).astype(F32)
    route = ys = pos = None
    out = None
    for l in range(depth):
        w_in_l = _layout_in_weights(w_in[l]).astype(BF16)
        w_out_l = _layout_out_weights(w_out[l]).astype(BF16)
        ret_nw = _pad_value_heads(ret_norm[l].astype(F32), 0).reshape(1, B_VW)
        nw_mix = norm_mix[l].astype(F32).reshape(1, D)
        if l == 0:
            za, zb, zc = _inproj(x2d, nw_mix, w_in_l)
        else:
            x2d, za, zb, zc = _combine(x2d, route, ys, pos, nw_mix, w_in_l)

        ya = _hgrn(za, lower_bounds[l].reshape(1, A_WIDTH), hgrn_norm[l].astype(F32).reshape(1, A_WIDTH),
                   tri, eones, bd_a, B, S)
        yb = _retention(zb, cos, sin, ret_consts, ret_nw, B, S)
        yc = _attention(zc, bias, B, S)

        wr = jnp.concatenate([router_group[l], router_expert[l]], axis=1).astype(F32)
        wr_hi = wr.astype(BF16)
        wr_lo = (wr - wr_hi.astype(F32)).astype(BF16)
        gap = jnp.zeros((D, ROUTE_W // 2 - wr.shape[1]), BF16)
        wr2 = jnp.concatenate([wr_hi, gap, wr_lo, gap], axis=1)
        nw_ffn = norm_ffn[l].astype(F32).reshape(1, D)
        x2d, route = _outproj(x2d, ya, yb, yc, w_out_l, nw_ffn, wr2)

        pos, counts, pad_start, blk_exp, next_exp, nused = _moe_plan(route, T)
        xs = _dispatch(x2d, nw_ffn, pos, counts, pad_start, blk_exp.shape[0] * MOE_BM)
        ys = _experts(xs, blk_exp, next_exp, nused, l, expert_w_gate, expert_w_up, expert_w_down)

    out = _combine(x2d, route, ys, pos, norm_final.astype(F32).reshape(1, D))
    return out.reshape(B, S, D).astype(x.dtype)
```

```python
import functools

import numpy as np
import jax
import jax.numpy as jnp
from jax import lax
from jax.experimental import pallas as pl
from jax.experimental.pallas import tpu as pltpu

F32 = jnp.float32
BF16 = jnp.bfloat16

D_MODEL = 1024
CHUNK = 64
NORM_EPS = 1e-6
MASK_VALUE = -1e30
MIN_FORGET = 1e-30
LOG2_E = 1.4426950408889634

A_HEADS, A_DK, A_DV = 4, 64, 64
A_WIDTH = A_HEADS * A_DV
B_HEADS, B_DK, B_DV = 4, 48, 96
B_HALF = B_DK // 2
B_QSLOT = 32
B_VSLOT = 128
B_QW = 2 * B_HEADS * B_QSLOT
B_VW = B_HEADS * B_VSLOT
ROPE_BASE = 10000.0
C_HEADS, C_DH = 6, 64
C_WIDTH = C_HEADS * C_DH
N_PREV_CHUNKS = 8
MAX_REL = 128
C_QB = 2 * CHUNK
C_BWIN = C_QB + N_PREV_CHUNKS * CHUNK
C_KT = 128

ZA_W = 4 * A_WIDTH
ZB_W = 2 * B_QW + 2 * B_VW
ZC_W = 3 * C_WIDTH
Y_W = A_WIDTH + B_VW + C_WIDTH

N_GROUPS, EXPERTS_PER_GROUP = 4, 8
N_EXPERTS = N_GROUPS * EXPERTS_PER_GROUP
D_EXPERT = 512
ROUTE_W = 128

SUB = 8
VMEM_LIMIT = 56 * 1024 * 1024

TM = 1024
TS_A = 512
TS_B = 512
TQ_C = 512
MOE_BM = 512
TM_COMBINE = 512
TM_DISPATCH = TM_COMBINE


def _cparams(*sem):
    return pltpu.CompilerParams(dimension_semantics=sem, vmem_limit_bytes=VMEM_LIMIT)


def _sigmoid(x):
    return 1.0 / (1.0 + jnp.exp(-x))


def _silu(x):
    return x * _sigmoid(x)


def _split3(x):
    hi = x.astype(BF16)
    r1 = x - hi.astype(F32)
    mid = r1.astype(BF16)
    lo = (r1 - mid.astype(F32)).astype(BF16)
    return hi, mid, lo


def _dot(a, b):
    return jnp.dot(a, b, preferred_element_type=F32)


def _dot_nt(a, b):
    return lax.dot_general(a, b, (((1,), (1,)), ((), ())), preferred_element_type=F32)


def _dot_tn(a, b):
    return lax.dot_general(a, b, (((0,), (0,)), ((), ())), preferred_element_type=F32)


LANES = 128
ROW_TILE = D_MODEL // LANES


def _store_row_tiles(ref, val):
    n = val.shape[0]
    for c in range(ROW_TILE):
        ref[pl.ds(c, n, stride=ROW_TILE), :] = val[:, c * LANES:(c + 1) * LANES]


def _load_row_tiles(ref, first_row, n):
    return jnp.concatenate(
        [ref[pl.ds(first_row * ROW_TILE + c, n, stride=ROW_TILE), :] for c in range(ROW_TILE)], axis=1)


def _rms(x, w):
    ms = jnp.mean(x * x, axis=-1, keepdims=True)
    return x * lax.rsqrt(ms + NORM_EPS) * w


def _project(h, w_ref, za_ref, zb_ref, zc_ref):
    hb = h.astype(BF16)
    za_ref[...] = _dot(hb, w_ref[:, 0:ZA_W]).astype(BF16)
    zb_ref[...] = _dot(hb, w_ref[:, ZA_W:ZA_W + ZB_W]).astype(BF16)
    zc_ref[...] = _dot(hb, w_ref[:, ZA_W + ZB_W:]).astype(BF16)


def _inproj_kernel(x_ref, nw_ref, w_ref, za_ref, zb_ref, zc_ref):
    _project(_rms(x_ref[...], nw_ref[...]), w_ref, za_ref, zb_ref, zc_ref)


def _inproj(x2d, nw, w):
    T = x2d.shape[0]
    row = lambda i: (i, 0)
    fixed = lambda i: (0, 0)
    return pl.pallas_call(
        _inproj_kernel,
        grid=(T // TM,),
        in_specs=[pl.BlockSpec((TM, D_MODEL), row),
                  pl.BlockSpec((1, D_MODEL), fixed),
                  pl.BlockSpec((D_MODEL, ZA_W + ZB_W + ZC_W), fixed)],
        out_specs=[pl.BlockSpec((TM, ZA_W), row),
                   pl.BlockSpec((TM, ZB_W), row),
                   pl.BlockSpec((TM, ZC_W), row)],
        out_shape=[jax.ShapeDtypeStruct((T, ZA_W), BF16),
                   jax.ShapeDtypeStruct((T, ZB_W), BF16),
                   jax.ShapeDtypeStruct((T, ZC_W), BF16)],
        compiler_params=_cparams("parallel"),
        name="inproj",
    )(x2d, nw, w)


def _hgrn_kernel(z_ref, lb_ref, nw_ref, tri_ref, eones_ref, bd_ref, o_ref, st_ref):
    @pl.when(pl.program_id(1) == 0)
    def _():
        st_ref[...] = jnp.zeros_like(st_ref)

    W = A_WIDTH
    lb = lb_ref[...]
    lane_head = lax.broadcasted_iota(jnp.int32, (1, W), 1) // A_DK
    head_masks = [lane_head == h for h in range(A_HEADS)]
    tri = tri_ref[...]
    eones = eones_ref[...]
    bd = bd_ref[...]
    jj = lax.broadcasted_iota(jnp.int32, (SUB, SUB, 1), 0)
    ii = lax.broadcasted_iota(jnp.int32, (SUB, SUB, 1), 1)
    causal = jj <= ii

    n_chunks = z_ref.shape[0] // CHUNK
    n_sub = CHUNK // SUB
    outs = []
    for c in range(n_chunks):
        r0 = c * CHUNK
        zq = z_ref[r0:r0 + CHUNK, 0:W].astype(F32)
        zf = z_ref[r0:r0 + CHUNK, W:2 * W].astype(F32)
        vv = z_ref[r0:r0 + CHUNK, 2 * W:3 * W]
        vf = vv.astype(F32)
        f = lb + (1.0 - lb) * _sigmoid(zf)
        lf = jnp.log(jnp.maximum(f, MIN_FORGET))
        kk = (1.0 - lb) * _sigmoid(-zf)
        qf = _silu(zq) * (A_DK ** -0.5)
        hi, mid, lo = _split3(lf)
        G = (_dot(tri, hi) + _dot(tri, mid) + _dot(tri, lo)) * LOG2_E
        g_last = G[CHUNK - 1:CHUNK, :]

        st = st_ref[...]
        inter = _dot_nt((qf * jnp.exp2(G)).astype(BF16), st.astype(BF16))
        khat = kk * jnp.exp2(g_last - G)
        ut = _dot_tn(vv, khat.astype(BF16))
        st_ref[...] = jnp.exp2(g_last) * st + ut * bd

        p_rows = []
        for s in range(1, n_sub):
            b = G[s * SUB - 1:s * SUB, :]
            qt = qf[s * SUB:(s + 1) * SUB, :] * jnp.exp2(G[s * SUB:(s + 1) * SUB, :] - b)
            kt = (kk[:s * SUB] * jnp.exp2(jnp.minimum(b - G[:s * SUB], 0.0))).astype(BF16)
            kt = jnp.concatenate([kt, jnp.zeros((CHUNK - s * SUB, W), BF16)], axis=0)
            lhs = jnp.concatenate([jnp.where(m, qt, 0.0) for m in head_masks], axis=0)
            p_rows.append(_dot_nt(lhs.astype(BF16), kt))
        pall = jnp.concatenate(p_rows, axis=0)
        pv = _dot(pall.astype(BF16), vv)

        pieces = []
        for s in range(n_sub):
            sl = slice(s * SUB, (s + 1) * SUB)
            gb, qb, kb, vb = G[sl], qf[sl], kk[sl], vf[sl]
            dec = jnp.exp2(jnp.minimum(gb[None, :, :] - gb[:, None, :], 0.0))
            a = (qb[None, :, :] * kb[:, None, :]) * dec
            r = _dot(a.reshape(SUB * SUB, W).astype(BF16), eones).reshape(SUB, SUB, W)
            diag = jnp.sum(jnp.where(causal, r, 0.0) * vb[:, None, :], axis=0)
            piece = inter[sl] + diag
            if s > 0:
                base = (s - 1) * A_HEADS * SUB
                for h in range(A_HEADS):
                    blk = pv[base + h * SUB:base + (h + 1) * SUB, :]
                    piece = piece + jnp.where(head_masks[h], blk, 0.0)
            pieces.append(piece)
        outs.append(jnp.concatenate(pieces, axis=0))

    o = jnp.concatenate(outs, axis=0)
    g = z_ref[:, 3 * W:4 * W].astype(F32)
    o_ref[...] = (_rms(o, nw_ref[...]) * _silu(g)).astype(BF16)


def _hgrn(za, lb, nw, tri, eones, bd, B, S):
    nt = S // TS_A
    fixed = lambda b, t: (0, 0)
    return pl.pallas_call(
        _hgrn_kernel,
        grid=(B, nt),
        in_specs=[pl.BlockSpec((TS_A, ZA_W), lambda b, t: (b * nt + t, 0)),
                  pl.BlockSpec((1, A_WIDTH), fixed),
                  pl.BlockSpec((1, A_WIDTH), fixed),
                  pl.BlockSpec((CHUNK, CHUNK), fixed),
                  pl.BlockSpec((A_WIDTH, A_WIDTH), fixed),
                  pl.BlockSpec((A_WIDTH, A_WIDTH), fixed)],
        out_specs=pl.BlockSpec((TS_A, A_WIDTH), lambda b, t: (b * nt + t, 0)),
        out_shape=jax.ShapeDtypeStruct((B * S, A_WIDTH), BF16),
        scratch_shapes=[pltpu.VMEM((A_WIDTH, A_WIDTH), F32)],
        compiler_params=_cparams("parallel", "arbitrary"),
        name="hgrn2",
    )(za, lb, nw, tri, eones, bd)


def _ret_kernel(z_ref, cos_ref, sin_ref, dstack_ref, qdec_ref, kdec_ref, cg_ref, bd_ref, nw_ref,
                o_ref, st_ref):
    @pl.when(pl.program_id(1) == 0)
    def _():
        st_ref[...] = jnp.zeros_like(st_ref)

    H = B_QW // 2
    lane_head = (lax.broadcasted_iota(jnp.int32, (1, B_QW), 1) % H) // B_QSLOT
    head_masks = [lane_head == h for h in range(B_HEADS)]
    lane_real = lax.broadcasted_iota(jnp.int32, (1, B_VSLOT), 1) < B_DV
    dstack = dstack_ref[...]
    qdec, kdec = qdec_ref[...], kdec_ref[...]
    cg, bd = cg_ref[...], bd_ref[...]

    n_chunks = z_ref.shape[0] // CHUNK
    outs = []
    for c in range(n_chunks):
        rows = slice(c * CHUNK, (c + 1) * CHUNK)
        cos, sin = cos_ref[rows, :], sin_ref[rows, :]

        def rot(off):
            t1 = z_ref[rows, off:off + H].astype(F32)
            t2 = z_ref[rows, off + H:off + 2 * H].astype(F32)
            return jnp.concatenate([t1 * cos - t2 * sin, t1 * sin + t2 * cos], axis=1)

        qr = rot(0)
        kr = rot(B_QW) * (B_DK ** -0.5)
        vv = z_ref[rows, 2 * B_QW:2 * B_QW + B_VW]

        lhs = jnp.concatenate([jnp.where(m, qr, 0.0) for m in head_masks], axis=0)
        sc = _dot_nt(lhs.astype(BF16), kr.astype(BF16)) * dstack
        pv = _dot(sc.astype(BF16), vv)
        intra = jnp.concatenate(
            [pv[h * CHUNK:(h + 1) * CHUNK, h * B_VSLOT:(h + 1) * B_VSLOT] for h in range(B_HEADS)],
            axis=1)
        st = st_ref[...]
        inter = _dot_nt((qr * qdec).astype(BF16), st.astype(BF16))
        ut = _dot_tn(vv, (kr * kdec).astype(BF16))
        st_ref[...] = cg * st + ut * bd
        outs.append(intra + inter)

    o = jnp.concatenate(outs, axis=0)
    normed = []
    for h in range(B_HEADS):
        oh = o[:, h * B_VSLOT:(h + 1) * B_VSLOT]
        mu = jnp.sum(oh, axis=-1, keepdims=True) * (1.0 / B_DV)
        d = jnp.where(lane_real, oh - mu, 0.0)
        var = jnp.sum(d * d, axis=-1, keepdims=True) * (1.0 / B_DV)
        normed.append(d * lax.rsqrt(var + NORM_EPS))
    y = jnp.concatenate(normed, axis=1)
    g = z_ref[:, 2 * B_QW + B_VW:].astype(F32)
    o_ref[...] = (y * nw_ref[...] * _silu(g)).astype(BF16)


def _retention(zb, cos, sin, consts, nw, B, S):
    nt = S // TS_B
    fixed = lambda b, t: (0, 0)
    dstack, qdec, kdec, cg, bd = consts
    return pl.pallas_call(
        _ret_kernel,
        grid=(B, nt),
        in_specs=[pl.BlockSpec((TS_B, ZB_W), lambda b, t: (b * nt + t, 0)),
                  pl.BlockSpec((TS_B, B_QW // 2), lambda b, t: (t, 0)),
                  pl.BlockSpec((TS_B, B_QW // 2), lambda b, t: (t, 0)),
                  pl.BlockSpec(dstack.shape, fixed),
                  pl.BlockSpec(qdec.shape, fixed),
                  pl.BlockSpec(kdec.shape, fixed),
                  pl.BlockSpec(cg.shape, fixed),
                  pl.BlockSpec(bd.shape, fixed),
                  pl.BlockSpec((1, B_VW), fixed)],
        out_specs=pl.BlockSpec((TS_B, B_VW), lambda b, t: (b * nt + t, 0)),
        out_shape=jax.ShapeDtypeStruct((B * S, B_VW), BF16),
        scratch_shapes=[pltpu.VMEM((B_VW, B_QW), F32)],
        compiler_params=_cparams("parallel", "arbitrary"),
        name="retention",
    )(zb, cos, sin, dstack, qdec, kdec, cg, bd, nw)


def _attn_kernel(q_ref, kp_ref, kc_ref, vp_ref, vc_ref, bias_ref, o_ref, kwin_ref, vwin_ref):
    t = pl.program_id(1)
    kwin_ref[0:TQ_C, :] = kp_ref[...]
    kwin_ref[TQ_C:, :] = kc_ref[...]
    vwin_ref[0:TQ_C, :] = vp_ref[...]
    vwin_ref[TQ_C:, :] = vc_ref[...]
    jcol = lax.broadcasted_iota(jnp.int32, (1, C_BWIN), 1)
    first_head = lax.broadcasted_iota(jnp.int32, (1, 2 * C_DH), 1) < C_DH
    zero = jnp.zeros((), BF16)

    def block(i, carry):
        r0 = pl.multiple_of(i * C_QB, C_QB)
        pen = jnp.where((jcol + (r0 + (t - 1) * TQ_C)) >= 0, 0.0, MASK_VALUE)
        for p in range(C_HEADS // 2):
            lanes = slice(2 * p * C_DH, 2 * (p + 1) * C_DH)
            q = q_ref[pl.ds(r0, C_QB), lanes] * (C_DH ** -0.5)
            lhs = jnp.concatenate([jnp.where(first_head, q, zero), jnp.where(first_head, zero, q)], axis=0)

            def scores(j):
                k = kwin_ref[pl.ds(r0 + j * C_KT, C_KT), lanes]
                return (_dot_nt(lhs, k) + bias_ref[p, :, j * C_KT:(j + 1) * C_KT]
                        + pen[:, j * C_KT:(j + 1) * C_KT])

            m = scores(0)
            for j in range(1, C_BWIN // C_KT):
                m = jnp.maximum(m, scores(j))
            m = jnp.max(m, axis=-1, keepdims=True)
            acc = l = None
            for j in range(C_BWIN // C_KT):
                e = jnp.exp(scores(j) - m)
                pvj = _dot(e.astype(BF16), vwin_ref[pl.ds(r0 + j * C_KT, C_KT), lanes])
                acc = pvj if acc is None else acc + pvj
                l = e if l is None else l + e
            pv = acc / jnp.sum(l, axis=-1, keepdims=True)
            o = jnp.where(first_head, pv[:C_QB], pv[C_QB:])
            o_ref[pl.ds(r0, C_QB), lanes] = o.astype(BF16)
        return carry

    lax.fori_loop(0, TQ_C // C_QB, block, 0, unroll=True)


def _attention(zc, bias, B, S):
    nt = S // TQ_C
    cur = lambda col: (lambda b, t: (b * nt + t, col))
    prev = lambda col: (lambda b, t: (b * nt + jnp.maximum(t - 1, 0), col))
    blk = (TQ_C, C_WIDTH)
    return pl.pallas_call(
        _attn_kernel,
        grid=(B, nt),
        in_specs=[pl.BlockSpec(blk, cur(0)),
                  pl.BlockSpec(blk, prev(1)), pl.BlockSpec(blk, cur(1)),
                  pl.BlockSpec(blk, prev(2)), pl.BlockSpec(blk, cur(2)),
                  pl.BlockSpec(bias.shape, lambda b, t: (0, 0, 0))],
        out_specs=pl.BlockSpec(blk, cur(0)),
        out_shape=jax.ShapeDtypeStruct((B * S, C_WIDTH), BF16),
        scratch_shapes=[pltpu.VMEM((2 * TQ_C, C_WIDTH), BF16),
                        pltpu.VMEM((2 * TQ_C, C_WIDTH), BF16)],
        compiler_params=_cparams("parallel", "arbitrary"),
        name="chunk_attention",
    )(zc, zc, zc, zc, zc, bias)


def _route(logits):
    lane = lax.broadcasted_iota(jnp.int32, logits.shape, 1).astype(F32)
    big = float(1 << 20)
    neg = -jnp.inf

    def first_argmax(vals):
        m = jnp.max(vals, axis=-1, keepdims=True)
        idx = jnp.min(jnp.where(vals == m, lane, big), axis=-1, keepdims=True)
        return m, idx

    gl = jnp.where(lane < N_GROUPS, logits, neg)
    gm, grp = first_argmax(gl)
    p_grp = 1.0 / jnp.sum(jnp.exp(gl - gm), axis=-1, keepdims=True)
    lo = N_GROUPS + grp * EXPERTS_PER_GROUP
    el = jnp.where((lane >= lo) & (lane < lo + EXPERTS_PER_GROUP), logits, neg)
    v1, i1 = first_argmax(el)
    v2, i2 = first_argmax(jnp.where(lane == i1, neg, el))
    e2 = jnp.exp(v2 - v1)
    g1 = p_grp / (1.0 + e2)
    g2 = p_grp * e2 / (1.0 + e2)
    out = jnp.where(lane == 0, i1 - N_GROUPS, 0.0)
    out = jnp.where(lane == 1, i2 - N_GROUPS, out)
    out = jnp.where(lane == 2, g1, out)
    out = jnp.where(lane == 3, g2, out)
    return out


def _outproj_kernel(x_ref, ya_ref, yb_ref, yc_ref, w_ref, nw_ref, wr_ref, xo_ref, r_ref):
    a0, a1 = A_WIDTH, A_WIDTH + B_VW
    x = x_ref[...]
    x = x + _dot(ya_ref[...], w_ref[0:a0, :])
    x = x + _dot(yb_ref[...], w_ref[a0:a1, :])
    x = x + _dot(yc_ref[...], w_ref[a1:, :])
    xo_ref[...] = x
    h = _rms(x, nw_ref[...])
    h_hi = h.astype(BF16)
    h_lo = (h - h_hi.astype(F32)).astype(BF16)
    d_hi = _dot(h_hi, wr_ref[...])
    logits = d_hi + pltpu.roll(d_hi, ROUTE_W // 2, 1) + _dot(h_lo, wr_ref[...])
    r_ref[...] = _route(logits)


def _outproj(x2d, ya, yb, yc, w, nw, wr3):
    T = x2d.shape[0]
    row = lambda i: (i, 0)
    fixed = lambda i: (0, 0)
    return pl.pallas_call(
        _outproj_kernel,
        grid=(T // TM,),
        in_specs=[pl.BlockSpec((TM, D_MODEL), row),
                  pl.BlockSpec((TM, A_WIDTH), row),
                  pl.BlockSpec((TM, B_VW), row),
                  pl.BlockSpec((TM, C_WIDTH), row),
                  pl.BlockSpec((Y_W, D_MODEL), fixed),
                  pl.BlockSpec((1, D_MODEL), fixed),
                  pl.BlockSpec((D_MODEL, ROUTE_W), fixed)],
        out_specs=[pl.BlockSpec((TM, D_MODEL), row),
                   pl.BlockSpec((TM, ROUTE_W), row)],
        out_shape=[jax.ShapeDtypeStruct((T, D_MODEL), F32),
                   jax.ShapeDtypeStruct((T, ROUTE_W), F32)],
        compiler_params=_cparams("parallel"),
        name="outproj_router",
    )(x2d, ya, yb, yc, w, nw, wr3)


PAD_PIECES = tuple(MOE_BM >> (k + 1) for k in range(MOE_BM.bit_length() - 1))
DMA_GROUP = 8


def _tile_rows(ref, first_tile_row, n_rows=1):
    return ref.at[pl.ds(pl.multiple_of(first_tile_row, ROW_TILE), n_rows * ROW_TILE)]


def _rows_done(src_hbm, dst, sem, n_rows):
    pltpu.make_async_copy(_tile_rows(src_hbm, 0, n_rows), _tile_rows(dst, 0, n_rows), sem).wait()


def _dispatch_kernel(cnt_ref, pstart_ref, pos_ref, x_ref, nw_ref, xs_hbm, hbuf, zbuf, sem, zsem):
    i = pl.program_id(0)
    n = pl.num_programs(0)
    tm = TM_DISPATCH
    cur = lax.rem(i, 2)
    stage = hbuf.at[cur]
    _store_row_tiles(stage, _rms(x_ref[...], nw_ref[...]))

    def pad_pieces(e):
        cnt = cnt_ref[e]
        first = pstart_ref[e] + cnt
        n_pad = (-cnt) & (MOE_BM - 1)
        for p in PAD_PIECES:
            yield (n_pad & p) != 0, first + (n_pad & (MOE_BM - 2 * p)), p

    def spare_blocks():
        last = N_EXPERTS - 1
        n_used = (pstart_ref[last] + cnt_ref[last] + (MOE_BM - 1)) // MOE_BM
        n_blocks = xs_hbm.shape[0] // (MOE_BM * ROW_TILE)
        for b in range(n_blocks - N_EXPERTS, n_blocks):
            for first in range(0, MOE_BM, PAD_PIECES[0]):
                yield b >= n_used, b * MOE_BM + first, PAD_PIECES[0]

    @pl.when(i == 0)
    def _():
        zbuf[...] = jnp.zeros_like(zbuf)
        for phase in ("start", "wait"):
            for e in range(N_EXPERTS + 1):
                for fire, slot, p in (pad_pieces(e) if e < N_EXPERTS else spare_blocks()):
                    @pl.when(fire)
                    def _():
                        cp = pltpu.make_async_copy(_tile_rows(zbuf, 0, p),
                                                   _tile_rows(xs_hbm, slot * ROW_TILE, p), zsem)
                        cp.start() if phase == "start" else cp.wait()

    def body(g, carry):
        rows = [g * DMA_GROUP + k for k in range(DMA_GROUP)]
        slots = [(pos_ref[0, 0, r], pos_ref[0, 0, tm + r]) for r in rows]
        for r, (s0, s1) in zip(rows, slots):
            src = _tile_rows(stage, r * ROW_TILE)
            pltpu.make_async_copy(src, _tile_rows(xs_hbm, s0), sem.at[cur]).start(priority=0)
            pltpu.make_async_copy(src, _tile_rows(xs_hbm, s1), sem.at[cur]).start(priority=1)
        return carry
    lax.fori_loop(0, tm // DMA_GROUP, body, 0)

    @pl.when(i > 0)
    def _():
        _rows_done(stage, xs_hbm, sem.at[1 - cur], 2 * tm)

    @pl.when(i == n - 1)
    def _():
        _rows_done(stage, xs_hbm, sem.at[cur], 2 * tm)


def _dispatch(x2d, nw, pos, counts, pad_start, n_slots):
    T = x2d.shape[0]
    tm = TM_DISPATCH
    grid_spec = pltpu.PrefetchScalarGridSpec(
        num_scalar_prefetch=2,
        grid=(T // tm,),
        in_specs=[pl.BlockSpec((1, 1, 2 * tm), lambda i, c, s: (i, 0, 0), memory_space=pltpu.SMEM),
                  pl.BlockSpec((tm, D_MODEL), lambda i, c, s: (i, 0)),
                  pl.BlockSpec((1, D_MODEL), lambda i, c, s: (0, 0))],
        out_specs=pl.BlockSpec(memory_space=pl.ANY),
        scratch_shapes=[pltpu.VMEM((2, tm * ROW_TILE, LANES), F32),
                        pltpu.VMEM((PAD_PIECES[0] * ROW_TILE, LANES), F32),
                        pltpu.SemaphoreType.DMA((2,)), pltpu.SemaphoreType.DMA(())],
    )
    return pl.pallas_call(
        _dispatch_kernel,
        grid_spec=grid_spec,
        out_shape=jax.ShapeDtypeStruct((n_slots * ROW_TILE, LANES), F32),
        compiler_params=_cparams("arbitrary"),
        name="dispatch",
    )(counts, pad_start, pos, x2d, nw)


def _expert_kernel(blk_exp_ref, next_exp_ref, nused_ref, x_ref, wg_hbm, wu_hbm, wd_hbm, ys_hbm,
                   wg_f, wu_f, wd_f, wg_bf, wu_bf, wd_bf, ybuf, wsem, ysem, *, layer):
    i = pl.program_id(0)
    n = pl.num_programs(0)
    used = i < nused_ref[0]
    slot = lax.rem(i, 2)
    block_rows = MOE_BM * ROW_TILE

    def weight_copies(e):
        return (pltpu.make_async_copy(wg_hbm.at[layer, e], wg_f, wsem.at[0]),
                pltpu.make_async_copy(wu_hbm.at[layer, e], wu_f, wsem.at[1]),
                pltpu.make_async_copy(wd_hbm.at[layer, e], wd_f, wsem.at[2]))

    def result_copy(s, block):
        dst = ys_hbm.at[pl.ds(pl.multiple_of(block * block_rows, block_rows), block_rows)]
        return pltpu.make_async_copy(ybuf.at[s], dst, ysem.at[s])

    @pl.when(i == 0)
    def _():
        for cp in weight_copies(blk_exp_ref[0]):
            cp.start(priority=1)

    @pl.when((i == 0) | (blk_exp_ref[i] != blk_exp_ref[jnp.maximum(i - 1, 0)]))
    def _():
        for cp in weight_copies(blk_exp_ref[i]):
            cp.wait()
        wg_bf[...] = wg_f[...].astype(BF16)
        wu_bf[...] = wu_f[...].astype(BF16)
        wd_bf[...] = wd_f[...].astype(BF16)

        @pl.when(next_exp_ref[i] >= 0)
        def _():
            for cp in weight_copies(next_exp_ref[i]):
                cp.start(priority=1)

    @pl.when(i >= 2)
    def _():
        result_copy(slot, i - 2).wait()

    @pl.when(used)
    def _():
        xb = _load_row_tiles(x_ref, 0, MOE_BM).astype(BF16)
        act = _silu(_dot(xb, wg_bf[...])) * _dot(xb, wu_bf[...])
        _store_row_tiles(ybuf.at[slot], _dot(act.astype(BF16), wd_bf[...]))

    @pl.when(jnp.logical_not(used))
    def _():
        ybuf[slot] = jnp.zeros((block_rows, LANES), F32)

    result_copy(slot, i).start(priority=1)

    @pl.when(i == n - 1)
    def _():
        result_copy(1 - slot, i - 1).wait()
        result_copy(slot, i).wait()


def _experts(xs, blk_exp, next_exp, nused, layer, w_gate, w_up, w_down):
    nblk = blk_exp.shape[0]
    assert nblk >= 2
    last_used = lambda i, nu: jnp.minimum(i, nu[0] - 1)
    grid_spec = pltpu.PrefetchScalarGridSpec(
        num_scalar_prefetch=3,
        grid=(nblk,),
        in_specs=[pl.BlockSpec((MOE_BM * ROW_TILE, LANES), lambda i, be, nx, nu: (last_used(i, nu), 0)),
                  pl.BlockSpec(memory_space=pl.ANY),
                  pl.BlockSpec(memory_space=pl.ANY),
                  pl.BlockSpec(memory_space=pl.ANY)],
        out_specs=pl.BlockSpec(memory_space=pl.ANY),
        scratch_shapes=[pltpu.VMEM((D_MODEL, D_EXPERT), F32), pltpu.VMEM((D_MODEL, D_EXPERT), F32),
                        pltpu.VMEM((D_EXPERT, D_MODEL), F32),
                        pltpu.VMEM((D_MODEL, D_EXPERT), BF16), pltpu.VMEM((D_MODEL, D_EXPERT), BF16),
                        pltpu.VMEM((D_EXPERT, D_MODEL), BF16),
                        pltpu.VMEM((2, MOE_BM * ROW_TILE, LANES), F32),
                        pltpu.SemaphoreType.DMA((3,)), pltpu.SemaphoreType.DMA((2,))],
    )
    return pl.pallas_call(
        functools.partial(_expert_kernel, layer=layer),
        grid_spec=grid_spec,
        out_shape=jax.ShapeDtypeStruct((nblk * MOE_BM * ROW_TILE, LANES), F32),
        compiler_params=_cparams("arbitrary"),
        name="experts",
    )(blk_exp, next_exp, nused, xs, w_gate, w_up, w_down)


def _combine_rows(pos_ref, pos_next_ref, x_ref, r_ref, y_hbm, ybuf, sem, request_next_first):
    i = pl.program_id(0)
    n_rows = 2 * TM_COMBINE
    slot = lax.rem(i, 2)

    @pl.when(i == 0)
    def _():
        _request_rows(pos_ref, y_hbm, ybuf, sem, 0, unrolled=False)

    if request_next_first:
        _request_next(pos_next_ref, y_hbm, ybuf, sem, unrolled=False)
    _rows_done(y_hbm, ybuf.at[slot], sem.at[slot], n_rows)
    r = r_ref[...]
    g0, g1 = r[:, 2:3], r[:, 3:4]
    y0 = _load_row_tiles(ybuf.at[slot], 0, TM_COMBINE)
    y1 = _load_row_tiles(ybuf.at[slot], TM_COMBINE, TM_COMBINE)
    return x_ref[...] + (g0 * y0 + g1 * y1)


def _request_rows(idx_ref, y_hbm, ybuf, sem, s, unrolled):
    n_rows = 2 * TM_COMBINE
    group = 2 * DMA_GROUP

    def body(g, carry):
        rows = [g * group + k for k in range(group)]
        slots = [idx_ref[0, 0, r] for r in rows]
        for k, src_row in enumerate(slots):
            pltpu.make_async_copy(_tile_rows(y_hbm, src_row),
                                  _tile_rows(ybuf.at[s], rows[k] * ROW_TILE), sem.at[s]).start(priority=k % 2)
        return carry

    if unrolled:
        for g in range(n_rows // group):
            body(g, 0)
    else:
        lax.fori_loop(0, n_rows // group, body, 0)


def _request_next(pos_next_ref, y_hbm, ybuf, sem, unrolled):
    _request_rows(pos_next_ref, y_hbm, ybuf, sem, 1 - lax.rem(pl.program_id(0), 2), unrolled)


def _drain_last(y_hbm, ybuf, sem):
    i = pl.program_id(0)

    @pl.when(i == pl.num_programs(0) - 1)
    def _():
        other = 1 - lax.rem(i, 2)
        _rows_done(y_hbm, ybuf.at[other], sem.at[other], 2 * TM_COMBINE)


def _combine_inproj_kernel(pos_ref, pos_next_ref, x_ref, r_ref, y_hbm, nw_ref, w_ref,
                           xo_ref, za_ref, zb_ref, zc_ref, ybuf, sem):
    x = _combine_rows(pos_ref, pos_next_ref, x_ref, r_ref, y_hbm, ybuf, sem, request_next_first=False)
    xo_ref[...] = x
    _request_next(pos_next_ref, y_hbm, ybuf, sem, unrolled=True)
    _project(_rms(x, nw_ref[...]), w_ref, za_ref, zb_ref, zc_ref)
    _drain_last(y_hbm, ybuf, sem)


def _combine_final_kernel(pos_ref, pos_next_ref, x_ref, r_ref, y_hbm, nw_ref, o_ref, ybuf, sem):
    x = _combine_rows(pos_ref, pos_next_ref, x_ref, r_ref, y_hbm, ybuf, sem, request_next_first=True)
    o_ref[...] = _rms(x, nw_ref[...])
    _drain_last(y_hbm, ybuf, sem)


def _combine(x2d, route, ys, pos, nw, w=None):
    T = x2d.shape[0]
    tm = TM_COMBINE
    row = lambda i: (i, 0)
    fixed = lambda i: (0, 0)
    n_tiles = T // tm
    in_specs = [pl.BlockSpec((1, 1, 2 * tm), lambda i: (i, 0, 0), memory_space=pltpu.SMEM),
                pl.BlockSpec((1, 1, 2 * tm), lambda i: (jnp.minimum(i + 1, n_tiles - 1), 0, 0),
                             memory_space=pltpu.SMEM),
                pl.BlockSpec((tm, D_MODEL), row),
                pl.BlockSpec((tm, ROUTE_W), row),
                pl.BlockSpec(memory_space=pl.ANY),
                pl.BlockSpec((1, D_MODEL), fixed)]
    scratch = [pltpu.VMEM((2, 2 * tm * ROW_TILE, LANES), F32), pltpu.SemaphoreType.DMA((2,))]
    if w is None:
        return pl.pallas_call(
            _combine_final_kernel,
            grid=(T // tm,),
            in_specs=in_specs,
            out_specs=pl.BlockSpec((tm, D_MODEL), row),
            out_shape=jax.ShapeDtypeStruct((T, D_MODEL), F32),
            scratch_shapes=scratch,
            compiler_params=_cparams("arbitrary"),
            name="combine_final",
        )(pos, pos, x2d, route, ys, nw)
    return pl.pallas_call(
        _combine_inproj_kernel,
        grid=(T // tm,),
        in_specs=in_specs + [pl.BlockSpec((D_MODEL, ZA_W + ZB_W + ZC_W), fixed)],
        out_specs=[pl.BlockSpec((tm, D_MODEL), row),
                   pl.BlockSpec((tm, ZA_W), row),
                   pl.BlockSpec((tm, ZB_W), row),
                   pl.BlockSpec((tm, ZC_W), row)],
        out_shape=[jax.ShapeDtypeStruct((T, D_MODEL), F32),
                   jax.ShapeDtypeStruct((T, ZA_W), BF16),
                   jax.ShapeDtypeStruct((T, ZB_W), BF16),
                   jax.ShapeDtypeStruct((T, ZC_W), BF16)],
        scratch_shapes=scratch,
        compiler_params=_cparams("arbitrary"),
        name="combine_inproj",
    )(pos, pos, x2d, route, ys, nw, w)


def _pad_value_heads(t, axis):
    shape = t.shape
    t = t.reshape(shape[:axis] + (B_HEADS, B_DV) + shape[axis + 1:])
    pad = [(0, 0)] * t.ndim
    pad[axis + 1] = (0, B_VSLOT - B_DV)
    return jnp.pad(t, pad).reshape(shape[:axis] + (B_VW,) + shape[axis + 1:])


def _layout_in_weights(w):
    D = w.shape[0]
    nqk, nv = B_HEADS * B_DK, B_HEADS * B_DV
    o = ZA_W

    def qk(block):
        t = block.reshape(D, B_HEADS, 2, B_HALF).transpose(0, 2, 1, 3)
        t = jnp.pad(t, ((0, 0), (0, 0), (0, 0), (0, B_QSLOT - B_HALF)))
        return t.reshape(D, B_QW)

    return jnp.concatenate([
        w[:, :o], qk(w[:, o:o + nqk]), qk(w[:, o + nqk:o + 2 * nqk]),
        _pad_value_heads(w[:, o + 2 * nqk:o + 2 * nqk + nv], 1),
        _pad_value_heads(w[:, o + 2 * nqk + nv:o + 2 * nqk + 2 * nv], 1),
        w[:, o + 2 * nqk + 2 * nv:]], axis=1)


def _layout_out_weights(w):
    nv = B_HEADS * B_DV
    return jnp.concatenate([w[:A_WIDTH], _pad_value_heads(w[A_WIDTH:A_WIDTH + nv], 0),
                            w[A_WIDTH + nv:]], axis=0)


def _retention_tables(S):
    log_gamma = jnp.log1p(-jnp.exp2(-5.0 - jnp.arange(B_HEADS, dtype=F32)))
    idx = jnp.arange(CHUNK, dtype=F32)
    rel = idx[:, None] - idx[None, :]
    decay = jnp.where(rel >= 0, jnp.exp(log_gamma[:, None, None] * jnp.maximum(rel, 0.0)), 0.0)
    dstack = decay.reshape(B_HEADS * CHUNK, CHUNK)
    k_decay = jnp.exp(log_gamma[:, None] * (CHUNK - 1.0 - idx)[None, :])
    q_decay = jnp.exp(log_gamma[:, None] * (idx + 1.0)[None, :])
    chunk_gamma = jnp.exp(log_gamma * CHUNK)

    def lanes(per_head):
        t = jnp.repeat(per_head[..., None], B_QSLOT, axis=-1)
        t = t.reshape(per_head.shape[:-1] + (B_HEADS * B_QSLOT,))
        return jnp.concatenate([t, t], axis=-1)

    qdec = lanes(q_decay.T)
    kdec = lanes(k_decay.T)
    cg = lanes(chunk_gamma[None, :])
    row_head = np.arange(B_VW) // B_VSLOT
    lane_head = (np.arange(B_QW) % (B_QW // 2)) // B_QSLOT
    bd = jnp.asarray((row_head[:, None] == lane_head[None, :]).astype(np.float32))

    half = B_HALF
    inv_freq = ROPE_BASE ** (-jnp.arange(half, dtype=F32) / half)
    ang = jnp.arange(S).astype(F32)[:, None] * inv_freq[None, :]
    pad = ((0, 0), (0, B_QSLOT - half))
    cos = jnp.tile(jnp.pad(jnp.cos(ang), pad), (1, B_HEADS))
    sin = jnp.tile(jnp.pad(jnp.sin(ang), pad), (1, B_HEADS))
    return cos, sin, (dstack, qdec, kdec, cg, bd)


def _attention_bias(rel_bias):
    qi = np.arange(C_QB)[:, None]
    km = np.arange(C_BWIN)[None, :]
    lag = (qi // CHUNK + N_PREV_CHUNKS) - km // CHUNK
    in_band = jnp.asarray((lag >= 0) & (lag <= N_PREV_CHUNKS))
    P = C_QB + C_BWIN
    t = np.arange(P)
    t = np.where(t < C_BWIN, t, t - P)
    dist = N_PREV_CHUNKS * CHUNK - t
    row = rel_bias.astype(F32)[:, np.clip(dist, -MAX_REL, MAX_REL) + MAX_REL]
    b = jnp.tile(row, (1, C_QB))[:, :C_QB * (P - 1)].reshape(C_HEADS, C_QB, P - 1)[:, :, :C_BWIN]
    b = jnp.where(in_band[None], b, MASK_VALUE)
    return b.reshape(C_HEADS // 2, 2 * C_QB, C_BWIN)


def _hgrn_tables():
    tri = jnp.asarray(np.tril(np.ones((CHUNK, CHUNK), np.float32))).astype(BF16)
    head = np.arange(A_WIDTH) // A_DK
    same = (head[:, None] == head[None, :]).astype(np.float32)
    return tri, jnp.asarray(same).astype(BF16), jnp.asarray(same)


def _moe_plan(route, T):
    nblk = 2 * T // MOE_BM + N_EXPERTS
    experts = jnp.arange(N_EXPERTS, dtype=jnp.int32)[None, :]
    oh0 = (route[:, 0:1].astype(jnp.int32) == experts).astype(jnp.int32)
    oh1 = (route[:, 1:2].astype(jnp.int32) == experts).astype(jnp.int32)
    c0, c1 = jnp.cumsum(oh0, axis=0), jnp.cumsum(oh1, axis=0)
    first_choice = c0[-1]
    counts = first_choice + c1[-1]
    padded = (counts + MOE_BM - 1) // MOE_BM * MOE_BM
    pad_end = jnp.cumsum(padded)
    pad_start = pad_end - padded
    dest0 = jnp.sum(oh0 * (pad_start[None, :] + c0 - oh0), axis=1)
    dest1 = jnp.sum(oh1 * ((pad_start + first_choice)[None, :] + c1 - oh1), axis=1)
    nused = (pad_end[-1] // MOE_BM).astype(jnp.int32)
    blk_start = jnp.arange(nblk, dtype=jnp.int32) * MOE_BM
    blk_exp = jnp.sum((pad_end[None, :] <= blk_start[:, None]).astype(jnp.int32), axis=1)
    last = jnp.take(blk_exp, jnp.maximum(nused - 1, 0))
    blk_used = jnp.arange(nblk) < nused
    blk_exp = jnp.where(blk_used, jnp.minimum(blk_exp, N_EXPERTS - 1), last)
    later = blk_used[None, :] & (blk_exp[None, :] > blk_exp[:, None])
    next_exp = jnp.min(jnp.where(later, blk_exp[None, :], N_EXPERTS), axis=1)
    next_exp = jnp.where(next_exp < N_EXPERTS, next_exp, -1).astype(jnp.int32)
    tm = TM_COMBINE
    pos = jnp.concatenate([dest0.reshape(T // tm, 1, tm), dest1.reshape(T // tm, 1, tm)], axis=2) * ROW_TILE
    return pos, counts.astype(jnp.int32), pad_start.astype(jnp.int32), blk_exp, next_exp, nused.reshape(1)


def kernel(x, w_in, w_out, norm_mix, norm_ffn, norm_final, hgrn_lb, hgrn_norm, ret_norm, rel_bias,
           router_group, router_expert, expert_w_gate, expert_w_up, expert_w_down):
    B, S, D = x.shape
    T = B * S
    depth = w_in.shape[0]

    p = jax.nn.softmax(hgrn_lb.astype(F32), axis=0)
    lower_bounds = jnp.clip(jnp.cumsum(p, axis=0) - p[0], 0.0, 1.0)

    cos, sin, ret_consts = _retention_tables(S)
    bias = _attention_bias(rel_bias)
    tri, eones, bd_a = _hgrn_tables()

    x2d = x.reshape(T, D).astype(F32)
    route = ys = pos = None
    out = None
    for l in range(depth):
        w_in_l = _layout_in_weights(w_in[l]).astype(BF16)
        w_out_l = _layout_out_weights(w_out[l]).astype(BF16)
        ret_nw = _pad_value_heads(ret_norm[l].astype(F32), 0).reshape(1, B_VW)
        nw_mix = norm_mix[l].astype(F32).reshape(1, D)
        if l == 0:
            za, zb, zc = _inproj(x2d, nw_mix, w_in_l)
        else:
            x2d, za, zb, zc = _combine(x2d, route, ys, pos, nw_mix, w_in_l)

        ya = _hgrn(za, lower_bounds[l].reshape(1, A_WIDTH), hgrn_norm[l].astype(F32).reshape(1, A_WIDTH),
                   tri, eones, bd_a, B, S)
        yb = _retention(zb, cos, sin, ret_consts, ret_nw, B, S)
        yc = _attention(zc, bias, B, S)

        wr = jnp.concatenate([router_group[l], router_expert[l]], axis=1).astype(F32)
        wr_hi = wr.astype(BF16)
        wr_lo = (wr - wr_hi.astype(F32)).astype(BF16)
        gap = jnp.zeros((D, ROUTE_W // 2 - wr.shape[1]), BF16)
        wr2 = jnp.concatenate([wr_hi, gap, wr_lo, gap], axis=1)
        nw_ffn = norm_ffn[l].astype(F32).reshape(1, D)
        x2d, route = _outproj(x2d, ya, yb, yc, w_out_l, nw_ffn, wr2)

        pos, counts, pad_start, blk_exp, next_exp, nused = _moe_plan(route, T)
        xs = _dispatch(x2d, nw_ffn, pos, counts, pad_start, blk_exp.shape[0] * MOE_BM)
        ys = _experts(xs, blk_exp, next_exp, nused, l, expert_w_gate, expert_w_up, expert_w_down)

    out = _combine(x2d, route, ys, pos, norm_final.astype(F32).reshape(1, D))
    return out.reshape(B, S, D).astype(x.dtype)
```

```python
import functools

import numpy as np
import jax
import jax.numpy as jnp
from jax import lax
from jax.experimental import pallas as pl
from jax.experimental.pallas import tpu as pltpu

F32 = jnp.float32
BF16 = jnp.bfloat16

D_MODEL = 1024
CHUNK = 64
NORM_EPS = 1e-6
MASK_VALUE = -1e30
MIN_FORGET = 1e-30
LOG2_E = 1.4426950408889634

A_HEADS, A_DK, A_DV = 4, 64, 64
A_WIDTH = A_HEADS * A_DV
B_HEADS, B_DK, B_DV = 4, 48, 96
B_HALF = B_DK // 2
B_QSLOT = 32
B_VSLOT = 128
B_QW = 2 * B_HEADS * B_QSLOT
B_VW = B_HEADS * B_VSLOT
ROPE_BASE = 10000.0
C_HEADS, C_DH = 6, 64
C_WIDTH = C_HEADS * C_DH
N_PREV_CHUNKS = 8
MAX_REL = 128
C_QB = 2 * CHUNK
C_BWIN = C_QB + N_PREV_CHUNKS * CHUNK
C_KT = 128

ZA_W = 4 * A_WIDTH
ZB_W = 2 * B_QW + 2 * B_VW
ZC_W = 3 * C_WIDTH
Y_W = A_WIDTH + B_VW + C_WIDTH

N_GROUPS, EXPERTS_PER_GROUP = 4, 8
N_EXPERTS = N_GROUPS * EXPERTS_PER_GROUP
D_EXPERT = 512
ROUTE_W = 128

SUB = 8
V7X_VMEM_BYTES = 64 * 1024 * 1024
VMEM_LIMIT = V7X_VMEM_BYTES * 7 // 8

TM = 1024
TS_A = 512
TS_B = 512
TQ_C = 512
MOE_BM = 512
TM_COMBINE = 512
TM_DISPATCH = TM_COMBINE


def _cparams(*sem):
    return pltpu.CompilerParams(dimension_semantics=sem, vmem_limit_bytes=VMEM_LIMIT)


def _sigmoid(x):
    return 1.0 / (1.0 + jnp.exp(-x))


def _silu(x):
    return x * _sigmoid(x)


def _split3(x):
    hi = x.astype(BF16)
    r1 = x - hi.astype(F32)
    mid = r1.astype(BF16)
    lo = (r1 - mid.astype(F32)).astype(BF16)
    return hi, mid, lo


def _dot(a, b):
    return jnp.dot(a, b, preferred_element_type=F32)


def _dot_nt(a, b):
    return lax.dot_general(a, b, (((1,), (1,)), ((), ())), preferred_element_type=F32)


def _dot_tn(a, b):
    return lax.dot_general(a, b, (((0,), (0,)), ((), ())), preferred_element_type=F32)


LANES = 128
ROW_TILE = D_MODEL // LANES


def _store_row_tiles(ref, val):
    n = val.shape[0]
    for c in range(ROW_TILE):
        ref[pl.ds(c, n, stride=ROW_TILE), :] = val[:, c * LANES:(c + 1) * LANES]


def _load_row_tiles(ref, first_row, n):
    return jnp.concatenate(
        [ref[pl.ds(first_row * ROW_TILE + c, n, stride=ROW_TILE), :] for c in range(ROW_TILE)], axis=1)


def _rms(x, w):
    ms = jnp.mean(x * x, axis=-1, keepdims=True)
    return x * lax.rsqrt(ms + NORM_EPS) * w


def _project(h, w_ref, za_ref, zb_ref, zc_ref):
    hb = h.astype(BF16)
    za_ref[...] = _dot(hb, w_ref[:, 0:ZA_W]).astype(BF16)
    zb_ref[...] = _dot(hb, w_ref[:, ZA_W:ZA_W + ZB_W]).astype(BF16)
    zc_ref[...] = _dot(hb, w_ref[:, ZA_W + ZB_W:]).astype(BF16)


def _inproj_kernel(x_ref, nw_ref, w_ref, za_ref, zb_ref, zc_ref):
    _project(_rms(x_ref[...], nw_ref[...]), w_ref, za_ref, zb_ref, zc_ref)


def _inproj(x2d, nw, w):
    T = x2d.shape[0]
    row = lambda i: (i, 0)
    fixed = lambda i: (0, 0)
    return pl.pallas_call(
        _inproj_kernel,
        grid=(T // TM,),
        in_specs=[pl.BlockSpec((TM, D_MODEL), row),
                  pl.BlockSpec((1, D_MODEL), fixed),
                  pl.BlockSpec((D_MODEL, ZA_W + ZB_W + ZC_W), fixed)],
        out_specs=[pl.BlockSpec((TM, ZA_W), row),
                   pl.BlockSpec((TM, ZB_W), row),
                   pl.BlockSpec((TM, ZC_W), row)],
        out_shape=[jax.ShapeDtypeStruct((T, ZA_W), BF16),
                   jax.ShapeDtypeStruct((T, ZB_W), BF16),
                   jax.ShapeDtypeStruct((T, ZC_W), BF16)],
        compiler_params=_cparams("parallel"),
        name="inproj",
    )(x2d, nw, w)


def _hgrn_kernel(z_ref, lb_ref, nw_ref, tri_ref, eones_ref, bd_ref, o_ref, st_ref):
    @pl.when(pl.program_id(1) == 0)
    def _():
        st_ref[...] = jnp.zeros_like(st_ref)

    W = A_WIDTH
    lb = lb_ref[...]
    lane_head = lax.broadcasted_iota(jnp.int32, (1, W), 1) // A_DK
    head_masks = [lane_head == h for h in range(A_HEADS)]
    tri = tri_ref[...]
    eones = eones_ref[...]
    bd = bd_ref[...]
    jj = lax.broadcasted_iota(jnp.int32, (SUB, SUB, 1), 0)
    ii = lax.broadcasted_iota(jnp.int32, (SUB, SUB, 1), 1)
    causal = jj <= ii

    n_chunks = z_ref.shape[0] // CHUNK
    n_sub = CHUNK // SUB
    outs = []
    for c in range(n_chunks):
        r0 = c * CHUNK
        zq = z_ref[r0:r0 + CHUNK, 0:W].astype(F32)
        zf = z_ref[r0:r0 + CHUNK, W:2 * W].astype(F32)
        vv = z_ref[r0:r0 + CHUNK, 2 * W:3 * W]
        vf = vv.astype(F32)
        f = lb + (1.0 - lb) * _sigmoid(zf)
        lf = jnp.log(jnp.maximum(f, MIN_FORGET))
        kk = (1.0 - lb) * _sigmoid(-zf)
        qf = _silu(zq) * (A_DK ** -0.5)
        hi, mid, lo = _split3(lf)
        G = (_dot(tri, hi) + _dot(tri, mid) + _dot(tri, lo)) * LOG2_E
        g_last = G[CHUNK - 1:CHUNK, :]

        st = st_ref[...]
        inter = _dot_nt((qf * jnp.exp2(G)).astype(BF16), st.astype(BF16))
        khat = kk * jnp.exp2(g_last - G)
        ut = _dot_tn(vv, khat.astype(BF16))
        st_ref[...] = jnp.exp2(g_last) * st + ut * bd

        p_rows = []
        for s in range(1, n_sub):
            b = G[s * SUB - 1:s * SUB, :]
            qt = qf[s * SUB:(s + 1) * SUB, :] * jnp.exp2(G[s * SUB:(s + 1) * SUB, :] - b)
            kt = (kk[:s * SUB] * jnp.exp2(jnp.minimum(b - G[:s * SUB], 0.0))).astype(BF16)
            kt = jnp.concatenate([kt, jnp.zeros((CHUNK - s * SUB, W), BF16)], axis=0)
            lhs = jnp.concatenate([jnp.where(m, qt, 0.0) for m in head_masks], axis=0)
            p_rows.append(_dot_nt(lhs.astype(BF16), kt))
        pall = jnp.concatenate(p_rows, axis=0)
        pv = _dot(pall.astype(BF16), vv)

        pieces = []
        for s in range(n_sub):
            sl = slice(s * SUB, (s + 1) * SUB)
            gb, qb, kb, vb = G[sl], qf[sl], kk[sl], vf[sl]
            dec = jnp.exp2(jnp.minimum(gb[None, :, :] - gb[:, None, :], 0.0))
            a = (qb[None, :, :] * kb[:, None, :]) * dec
            r = _dot(a.reshape(SUB * SUB, W).astype(BF16), eones).reshape(SUB, SUB, W)
            diag = jnp.sum(jnp.where(causal, r, 0.0) * vb[:, None, :], axis=0)
            piece = inter[sl] + diag
            if s > 0:
                base = (s - 1) * A_HEADS * SUB
                for h in range(A_HEADS):
                    blk = pv[base + h * SUB:base + (h + 1) * SUB, :]
                    piece = piece + jnp.where(head_masks[h], blk, 0.0)
            pieces.append(piece)
        outs.append(jnp.concatenate(pieces, axis=0))

    o = jnp.concatenate(outs, axis=0)
    g = z_ref[:, 3 * W:4 * W].astype(F32)
    o_ref[...] = (_rms(o, nw_ref[...]) * _silu(g)).astype(BF16)


def _hgrn(za, lb, nw, tri, eones, bd, B, S):
    nt = S // TS_A
    fixed = lambda b, t: (0, 0)
    return pl.pallas_call(
        _hgrn_kernel,
        grid=(B, nt),
        in_specs=[pl.BlockSpec((TS_A, ZA_W), lambda b, t: (b * nt + t, 0)),
                  pl.BlockSpec((1, A_WIDTH), fixed),
                  pl.BlockSpec((1, A_WIDTH), fixed),
                  pl.BlockSpec((CHUNK, CHUNK), fixed),
                  pl.BlockSpec((A_WIDTH, A_WIDTH), fixed),
                  pl.BlockSpec((A_WIDTH, A_WIDTH), fixed)],
        out_specs=pl.BlockSpec((TS_A, A_WIDTH), lambda b, t: (b * nt + t, 0)),
        out_shape=jax.ShapeDtypeStruct((B * S, A_WIDTH), BF16),
        scratch_shapes=[pltpu.VMEM((A_WIDTH, A_WIDTH), F32)],
        compiler_params=_cparams("parallel", "arbitrary"),
        name="hgrn2",
    )(za, lb, nw, tri, eones, bd)


def _ret_kernel(z_ref, cos_ref, sin_ref, dstack_ref, qdec_ref, kdec_ref, cg_ref, bd_ref, nw_ref,
                o_ref, st_ref):
    @pl.when(pl.program_id(1) == 0)
    def _():
        st_ref[...] = jnp.zeros_like(st_ref)

    H = B_QW // 2
    lane_head = (lax.broadcasted_iota(jnp.int32, (1, B_QW), 1) % H) // B_QSLOT
    head_masks = [lane_head == h for h in range(B_HEADS)]
    lane_real = lax.broadcasted_iota(jnp.int32, (1, B_VSLOT), 1) < B_DV
    dstack = dstack_ref[...]
    qdec, kdec = qdec_ref[...], kdec_ref[...]
    cg, bd = cg_ref[...], bd_ref[...]

    n_chunks = z_ref.shape[0] // CHUNK
    outs = []
    for c in range(n_chunks):
        rows = slice(c * CHUNK, (c + 1) * CHUNK)
        cos, sin = cos_ref[rows, :], sin_ref[rows, :]

        def rot(off):
            t1 = z_ref[rows, off:off + H].astype(F32)
            t2 = z_ref[rows, off + H:off + 2 * H].astype(F32)
            return jnp.concatenate([t1 * cos - t2 * sin, t1 * sin + t2 * cos], axis=1)

        qr = rot(0)
        kr = rot(B_QW) * (B_DK ** -0.5)
        vv = z_ref[rows, 2 * B_QW:2 * B_QW + B_VW]

        lhs = jnp.concatenate([jnp.where(m, qr, 0.0) for m in head_masks], axis=0)
        sc = _dot_nt(lhs.astype(BF16), kr.astype(BF16)) * dstack
        pv = _dot(sc.astype(BF16), vv)
        intra = jnp.concatenate(
            [pv[h * CHUNK:(h + 1) * CHUNK, h * B_VSLOT:(h + 1) * B_VSLOT] for h in range(B_HEADS)],
            axis=1)
        st = st_ref[...]
        inter = _dot_nt((qr * qdec).astype(BF16), st.astype(BF16))
        ut = _dot_tn(vv, (kr * kdec).astype(BF16))
        st_ref[...] = cg * st + ut * bd
        outs.append(intra + inter)

    o = jnp.concatenate(outs, axis=0)
    normed = []
    for h in range(B_HEADS):
        oh = o[:, h * B_VSLOT:(h + 1) * B_VSLOT]
        mu = jnp.sum(oh, axis=-1, keepdims=True) * (1.0 / B_DV)
        d = jnp.where(lane_real, oh - mu, 0.0)
        var = jnp.sum(d * d, axis=-1, keepdims=True) * (1.0 / B_DV)
        normed.append(d * lax.rsqrt(var + NORM_EPS))
    y = jnp.concatenate(normed, axis=1)
    g = z_ref[:, 2 * B_QW + B_VW:].astype(F32)
    o_ref[...] = (y * nw_ref[...] * _silu(g)).astype(BF16)


def _retention(zb, cos, sin, consts, nw, B, S):
    nt = S // TS_B
    fixed = lambda b, t: (0, 0)
    dstack, qdec, kdec, cg, bd = consts
    return pl.pallas_call(
        _ret_kernel,
        grid=(B, nt),
        in_specs=[pl.BlockSpec((TS_B, ZB_W), lambda b, t: (b * nt + t, 0)),
                  pl.BlockSpec((TS_B, B_QW // 2), lambda b, t: (t, 0)),
                  pl.BlockSpec((TS_B, B_QW // 2), lambda b, t: (t, 0)),
                  pl.BlockSpec(dstack.shape, fixed),
                  pl.BlockSpec(qdec.shape, fixed),
                  pl.BlockSpec(kdec.shape, fixed),
                  pl.BlockSpec(cg.shape, fixed),
                  pl.BlockSpec(bd.shape, fixed),
                  pl.BlockSpec((1, B_VW), fixed)],
        out_specs=pl.BlockSpec((TS_B, B_VW), lambda b, t: (b * nt + t, 0)),
        out_shape=jax.ShapeDtypeStruct((B * S, B_VW), BF16),
        scratch_shapes=[pltpu.VMEM((B_VW, B_QW), F32)],
        compiler_params=_cparams("parallel", "arbitrary"),
        name="retention",
    )(zb, cos, sin, dstack, qdec, kdec, cg, bd, nw)


def _attn_kernel(q_ref, kp_ref, kc_ref, vp_ref, vc_ref, bias_ref, o_ref, kwin_ref, vwin_ref):
    t = pl.program_id(1)
    kwin_ref[0:TQ_C, :] = kp_ref[...]
    kwin_ref[TQ_C:, :] = kc_ref[...]
    vwin_ref[0:TQ_C, :] = vp_ref[...]
    vwin_ref[TQ_C:, :] = vc_ref[...]
    jcol = lax.broadcasted_iota(jnp.int32, (1, C_BWIN), 1)
    first_head = lax.broadcasted_iota(jnp.int32, (1, 2 * C_DH), 1) < C_DH
    zero = jnp.zeros((), BF16)

    def block(i, carry):
        r0 = pl.multiple_of(i * C_QB, C_QB)
        pen = jnp.where((jcol + (r0 + (t - 1) * TQ_C)) >= 0, 0.0, MASK_VALUE)
        for p in range(C_HEADS // 2):
            lanes = slice(2 * p * C_DH, 2 * (p + 1) * C_DH)
            q = q_ref[pl.ds(r0, C_QB), lanes] * (C_DH ** -0.5)
            lhs = jnp.concatenate([jnp.where(first_head, q, zero), jnp.where(first_head, zero, q)], axis=0)

            def scores(j):
                k = kwin_ref[pl.ds(r0 + j * C_KT, C_KT), lanes]
                return (_dot_nt(lhs, k) + bias_ref[p, :, j * C_KT:(j + 1) * C_KT]
                        + pen[:, j * C_KT:(j + 1) * C_KT])

            m = scores(0)
            for j in range(1, C_BWIN // C_KT):
                m = jnp.maximum(m, scores(j))
            m = jnp.max(m, axis=-1, keepdims=True)
            acc = l = None
            for j in range(C_BWIN // C_KT):
                e = jnp.exp(scores(j) - m)
                pvj = _dot(e.astype(BF16), vwin_ref[pl.ds(r0 + j * C_KT, C_KT), lanes])
                acc = pvj if acc is None else acc + pvj
                l = e if l is None else l + e
            pv = acc / jnp.sum(l, axis=-1, keepdims=True)
            o = jnp.where(first_head, pv[:C_QB], pv[C_QB:])
            o_ref[pl.ds(r0, C_QB), lanes] = o.astype(BF16)
        return carry

    lax.fori_loop(0, TQ_C // C_QB, block, 0, unroll=True)


def _attention(zc, bias, B, S):
    nt = S // TQ_C
    cur = lambda col: (lambda b, t: (b * nt + t, col))
    prev = lambda col: (lambda b, t: (b * nt + jnp.maximum(t - 1, 0), col))
    blk = (TQ_C, C_WIDTH)
    return pl.pallas_call(
        _attn_kernel,
        grid=(B, nt),
        in_specs=[pl.BlockSpec(blk, cur(0)),
                  pl.BlockSpec(blk, prev(1)), pl.BlockSpec(blk, cur(1)),
                  pl.BlockSpec(blk, prev(2)), pl.BlockSpec(blk, cur(2)),
                  pl.BlockSpec(bias.shape, lambda b, t: (0, 0, 0))],
        out_specs=pl.BlockSpec(blk, cur(0)),
        out_shape=jax.ShapeDtypeStruct((B * S, C_WIDTH), BF16),
        scratch_shapes=[pltpu.VMEM((2 * TQ_C, C_WIDTH), BF16),
                        pltpu.VMEM((2 * TQ_C, C_WIDTH), BF16)],
        compiler_params=_cparams("parallel", "arbitrary"),
        name="chunk_attention",
    )(zc, zc, zc, zc, zc, bias)


def _route(logits):
    lane = lax.broadcasted_iota(jnp.int32, logits.shape, 1).astype(F32)
    big = float(1 << 20)
    neg = -jnp.inf

    def first_argmax(vals):
        m = jnp.max(vals, axis=-1, keepdims=True)
        idx = jnp.min(jnp.where(vals == m, lane, big), axis=-1, keepdims=True)
        return m, idx

    gl = jnp.where(lane < N_GROUPS, logits, neg)
    gm, grp = first_argmax(gl)
    p_grp = 1.0 / jnp.sum(jnp.exp(gl - gm), axis=-1, keepdims=True)
    lo = N_GROUPS + grp * EXPERTS_PER_GROUP
    el = jnp.where((lane >= lo) & (lane < lo + EXPERTS_PER_GROUP), logits, neg)
    v1, i1 = first_argmax(el)
    v2, i2 = first_argmax(jnp.where(lane == i1, neg, el))
    e2 = jnp.exp(v2 - v1)
    g1 = p_grp / (1.0 + e2)
    g2 = p_grp * e2 / (1.0 + e2)
    out = jnp.where(lane == 0, i1 - N_GROUPS, 0.0)
    out = jnp.where(lane == 1, i2 - N_GROUPS, out)
    out = jnp.where(lane == 2, g1, out)
    out = jnp.where(lane == 3, g2, out)
    return out


def _outproj_kernel(x_ref, ya_ref, yb_ref, yc_ref, w_ref, nw_ref, wr_ref, xo_ref, r_ref):
    a0, a1 = A_WIDTH, A_WIDTH + B_VW
    x = x_ref[...]
    x = x + _dot(ya_ref[...], w_ref[0:a0, :])
    x = x + _dot(yb_ref[...], w_ref[a0:a1, :])
    x = x + _dot(yc_ref[...], w_ref[a1:, :])
    xo_ref[...] = x
    h = _rms(x, nw_ref[...])
    h_hi = h.astype(BF16)
    h_lo = (h - h_hi.astype(F32)).astype(BF16)
    d_hi = _dot(h_hi, wr_ref[...])
    logits = d_hi + pltpu.roll(d_hi, ROUTE_W // 2, 1) + _dot(h_lo, wr_ref[...])
    r_ref[...] = _route(logits)


def _outproj(x2d, ya, yb, yc, w, nw, wr3):
    T = x2d.shape[0]
    row = lambda i: (i, 0)
    fixed = lambda i: (0, 0)
    return pl.pallas_call(
        _outproj_kernel,
        grid=(T // TM,),
        in_specs=[pl.BlockSpec((TM, D_MODEL), row),
                  pl.BlockSpec((TM, A_WIDTH), row),
                  pl.BlockSpec((TM, B_VW), row),
                  pl.BlockSpec((TM, C_WIDTH), row),
                  pl.BlockSpec((Y_W, D_MODEL), fixed),
                  pl.BlockSpec((1, D_MODEL), fixed),
                  pl.BlockSpec((D_MODEL, ROUTE_W), fixed)],
        out_specs=[pl.BlockSpec((TM, D_MODEL), row),
                   pl.BlockSpec((TM, ROUTE_W), row)],
        out_shape=[jax.ShapeDtypeStruct((T, D_MODEL), F32),
                   jax.ShapeDtypeStruct((T, ROUTE_W), F32)],
        compiler_params=_cparams("parallel"),
        name="outproj_router",
    )(x2d, ya, yb, yc, w, nw, wr3)


PAD_PIECES = tuple(MOE_BM >> (k + 1) for k in range(MOE_BM.bit_length() - 1))
DMA_GROUP = 8


def _tile_rows(ref, first_tile_row, n_rows=1):
    return ref.at[pl.ds(pl.multiple_of(first_tile_row, ROW_TILE), n_rows * ROW_TILE)]


def _rows_done(src_hbm, dst, sem, n_rows):
    pltpu.make_async_copy(_tile_rows(src_hbm, 0, n_rows), _tile_rows(dst, 0, n_rows), sem).wait()


def _dispatch_kernel(cnt_ref, pstart_ref, pos_ref, x_ref, nw_ref, xs_hbm, hbuf, zbuf, sem, zsem):
    i = pl.program_id(0)
    n = pl.num_programs(0)
    tm = TM_DISPATCH
    cur = lax.rem(i, 2)
    stage = hbuf.at[cur]
    _store_row_tiles(stage, _rms(x_ref[...], nw_ref[...]))

    def pad_pieces(e):
        cnt = cnt_ref[e]
        first = pstart_ref[e] + cnt
        n_pad = (-cnt) & (MOE_BM - 1)
        for p in PAD_PIECES:
            yield (n_pad & p) != 0, first + (n_pad & (MOE_BM - 2 * p)), p

    def spare_blocks():
        last = N_EXPERTS - 1
        n_used = (pstart_ref[last] + cnt_ref[last] + (MOE_BM - 1)) // MOE_BM
        n_blocks = xs_hbm.shape[0] // (MOE_BM * ROW_TILE)
        for b in range(n_blocks - N_EXPERTS, n_blocks):
            for first in range(0, MOE_BM, PAD_PIECES[0]):
                yield b >= n_used, b * MOE_BM + first, PAD_PIECES[0]

    @pl.when(i == 0)
    def _():
        zbuf[...] = jnp.zeros_like(zbuf)
        for phase in ("start", "wait"):
            for e in range(N_EXPERTS + 1):
                for fire, slot, p in (pad_pieces(e) if e < N_EXPERTS else spare_blocks()):
                    @pl.when(fire)
                    def _():
                        cp = pltpu.make_async_copy(_tile_rows(zbuf, 0, p),
                                                   _tile_rows(xs_hbm, slot * ROW_TILE, p), zsem)
                        cp.start() if phase == "start" else cp.wait()

    def body(g, carry):
        rows = [g * DMA_GROUP + k for k in range(DMA_GROUP)]
        slots = [(pos_ref[0, 0, r], pos_ref[0, 0, tm + r]) for r in rows]
        for r, (s0, s1) in zip(rows, slots):
            src = _tile_rows(stage, r * ROW_TILE)
            pltpu.make_async_copy(src, _tile_rows(xs_hbm, s0), sem.at[cur]).start(priority=0)
            pltpu.make_async_copy(src, _tile_rows(xs_hbm, s1), sem.at[cur]).start(priority=1)
        return carry
    lax.fori_loop(0, tm // DMA_GROUP, body, 0)

    @pl.when(i > 0)
    def _():
        _rows_done(stage, xs_hbm, sem.at[1 - cur], 2 * tm)

    @pl.when(i == n - 1)
    def _():
        _rows_done(stage, xs_hbm, sem.at[cur], 2 * tm)


def _dispatch(x2d, nw, pos, counts, pad_start, n_slots):
    T = x2d.shape[0]
    tm = TM_DISPATCH
    grid_spec = pltpu.PrefetchScalarGridSpec(
        num_scalar_prefetch=2,
        grid=(T // tm,),
        in_specs=[pl.BlockSpec((1, 1, 2 * tm), lambda i, c, s: (i, 0, 0), memory_space=pltpu.SMEM),
                  pl.BlockSpec((tm, D_MODEL), lambda i, c, s: (i, 0)),
                  pl.BlockSpec((1, D_MODEL), lambda i, c, s: (0, 0))],
        out_specs=pl.BlockSpec(memory_space=pl.ANY),
        scratch_shapes=[pltpu.VMEM((2, tm * ROW_TILE, LANES), F32),
                        pltpu.VMEM((PAD_PIECES[0] * ROW_TILE, LANES), F32),
                        pltpu.SemaphoreType.DMA((2,)), pltpu.SemaphoreType.DMA(())],
    )
    return pl.pallas_call(
        _dispatch_kernel,
        grid_spec=grid_spec,
        out_shape=jax.ShapeDtypeStruct((n_slots * ROW_TILE, LANES), F32),
        compiler_params=_cparams("arbitrary"),
        name="dispatch",
    )(counts, pad_start, pos, x2d, nw)


def _expert_kernel(blk_exp_ref, next_exp_ref, nused_ref, x_ref, wg_hbm, wu_hbm, wd_hbm, ys_hbm,
                   wg_f, wu_f, wd_f, wg_bf, wu_bf, wd_bf, ybuf, wsem, ysem, *, layer):
    i = pl.program_id(0)
    n = pl.num_programs(0)
    used = i < nused_ref[0]
    slot = lax.rem(i, 2)
    block_rows = MOE_BM * ROW_TILE

    def weight_copies(e):
        return (pltpu.make_async_copy(wg_hbm.at[layer, e], wg_f, wsem.at[0]),
                pltpu.make_async_copy(wu_hbm.at[layer, e], wu_f, wsem.at[1]),
                pltpu.make_async_copy(wd_hbm.at[layer, e], wd_f, wsem.at[2]))

    def result_copy(s, block):
        dst = ys_hbm.at[pl.ds(pl.multiple_of(block * block_rows, block_rows), block_rows)]
        return pltpu.make_async_copy(ybuf.at[s], dst, ysem.at[s])

    @pl.when(i == 0)
    def _():
        for cp in weight_copies(blk_exp_ref[0]):
            cp.start(priority=1)

    @pl.when((i == 0) | (blk_exp_ref[i] != blk_exp_ref[jnp.maximum(i - 1, 0)]))
    def _():
        for cp in weight_copies(blk_exp_ref[i]):
            cp.wait()
        wg_bf[...] = wg_f[...].astype(BF16)
        wu_bf[...] = wu_f[...].astype(BF16)
        wd_bf[...] = wd_f[...].astype(BF16)

        @pl.when(next_exp_ref[i] >= 0)
        def _():
            for cp in weight_copies(next_exp_ref[i]):
                cp.start(priority=1)

    @pl.when(i >= 2)
    def _():
        result_copy(slot, i - 2).wait()

    @pl.when(used)
    def _():
        xb = _load_row_tiles(x_ref, 0, MOE_BM).astype(BF16)
        act = _silu(_dot(xb, wg_bf[...])) * _dot(xb, wu_bf[...])
        _store_row_tiles(ybuf.at[slot], _dot(act.astype(BF16), wd_bf[...]))

    @pl.when(jnp.logical_not(used))
    def _():
        ybuf[slot] = jnp.zeros((block_rows, LANES), F32)

    result_copy(slot, i).start(priority=1)

    @pl.when(i == n - 1)
    def _():
        result_copy(1 - slot, i - 1).wait()
        result_copy(slot, i).wait()


def _experts(xs, blk_exp, next_exp, nused, layer, w_gate, w_up, w_down):
    nblk = blk_exp.shape[0]
    assert nblk >= 2
    last_used = lambda i, nu: jnp.minimum(i, nu[0] - 1)
    grid_spec = pltpu.PrefetchScalarGridSpec(
        num_scalar_prefetch=3,
        grid=(nblk,),
        in_specs=[pl.BlockSpec((MOE_BM * ROW_TILE, LANES), lambda i, be, nx, nu: (last_used(i, nu), 0)),
                  pl.BlockSpec(memory_space=pl.ANY),
                  pl.BlockSpec(memory_space=pl.ANY),
                  pl.BlockSpec(memory_space=pl.ANY)],
        out_specs=pl.BlockSpec(memory_space=pl.ANY),
        scratch_shapes=[pltpu.VMEM((D_MODEL, D_EXPERT), F32), pltpu.VMEM((D_MODEL, D_EXPERT), F32),
                        pltpu.VMEM((D_EXPERT, D_MODEL), F32),
                        pltpu.VMEM((D_MODEL, D_EXPERT), BF16), pltpu.VMEM((D_MODEL, D_EXPERT), BF16),
                        pltpu.VMEM((D_EXPERT, D_MODEL), BF16),
                        pltpu.VMEM((2, MOE_BM * ROW_TILE, LANES), F32),
                        pltpu.SemaphoreType.DMA((3,)), pltpu.SemaphoreType.DMA((2,))],
    )
    return pl.pallas_call(
        functools.partial(_expert_kernel, layer=layer),
        grid_spec=grid_spec,
        out_shape=jax.ShapeDtypeStruct((nblk * MOE_BM * ROW_TILE, LANES), F32),
        compiler_params=_cparams("arbitrary"),
        name="experts",
    )(blk_exp, next_exp, nused, xs, w_gate, w_up, w_down)


def _combine_rows(pos_ref, pos_next_ref, x_ref, r_ref, y_hbm, ybuf, sem, request_next_first):
    i = pl.program_id(0)
    n_rows = 2 * TM_COMBINE
    slot = lax.rem(i, 2)

    @pl.when(i == 0)
    def _():
        _request_rows(pos_ref, y_hbm, ybuf, sem, 0, unrolled=False)

    if request_next_first:
        _request_next(pos_next_ref, y_hbm, ybuf, sem, unrolled=False)
    _rows_done(y_hbm, ybuf.at[slot], sem.at[slot], n_rows)
    r = r_ref[...]
    g0, g1 = r[:, 2:3], r[:, 3:4]
    y0 = _load_row_tiles(ybuf.at[slot], 0, TM_COMBINE)
    y1 = _load_row_tiles(ybuf.at[slot], TM_COMBINE, TM_COMBINE)
    return x_ref[...] + (g0 * y0 + g1 * y1)


def _request_rows(idx_ref, y_hbm, ybuf, sem, s, unrolled):
    n_rows = 2 * TM_COMBINE
    group = 2 * DMA_GROUP

    def body(g, carry):
        rows = [g * group + k for k in range(group)]
        slots = [idx_ref[0, 0, r] for r in rows]
        for k, src_row in enumerate(slots):
            pltpu.make_async_copy(_tile_rows(y_hbm, src_row),
                                  _tile_rows(ybuf.at[s], rows[k] * ROW_TILE), sem.at[s]).start(priority=k % 2)
        return carry

    if unrolled:
        for g in range(n_rows // group):
            body(g, 0)
    else:
        lax.fori_loop(0, n_rows // group, body, 0)


def _request_next(pos_next_ref, y_hbm, ybuf, sem, unrolled):
    _request_rows(pos_next_ref, y_hbm, ybuf, sem, 1 - lax.rem(pl.program_id(0), 2), unrolled)


def _drain_last(y_hbm, ybuf, sem):
    i = pl.program_id(0)

    @pl.when(i == pl.num_programs(0) - 1)
    def _():
        other = 1 - lax.rem(i, 2)
        _rows_done(y_hbm, ybuf.at[other], sem.at[other], 2 * TM_COMBINE)


def _combine_inproj_kernel(pos_ref, pos_next_ref, x_ref, r_ref, y_hbm, nw_ref, w_ref,
                           xo_ref, za_ref, zb_ref, zc_ref, ybuf, sem):
    x = _combine_rows(pos_ref, pos_next_ref, x_ref, r_ref, y_hbm, ybuf, sem, request_next_first=False)
    xo_ref[...] = x
    _request_next(pos_next_ref, y_hbm, ybuf, sem, unrolled=True)
    _project(_rms(x, nw_ref[...]), w_ref, za_ref, zb_ref, zc_ref)
    _drain_last(y_hbm, ybuf, sem)


def _combine_final_kernel(pos_ref, pos_next_ref, x_ref, r_ref, y_hbm, nw_ref, o_ref, ybuf, sem):
    x = _combine_rows(pos_ref, pos_next_ref, x_ref, r_ref, y_hbm, ybuf, sem, request_next_first=True)
    o_ref[...] = _rms(x, nw_ref[...])
    _drain_last(y_hbm, ybuf, sem)


def _combine(x2d, route, ys, pos, nw, w=None):
    T = x2d.shape[0]
    tm = TM_COMBINE
    row = lambda i: (i, 0)
    fixed = lambda i: (0, 0)
    n_tiles = T // tm
    in_specs = [pl.BlockSpec((1, 1, 2 * tm), lambda i: (i, 0, 0), memory_space=pltpu.SMEM),
                pl.BlockSpec((1, 1, 2 * tm), lambda i: (jnp.minimum(i + 1, n_tiles - 1), 0, 0),
                             memory_space=pltpu.SMEM),
                pl.BlockSpec((tm, D_MODEL), row),
                pl.BlockSpec((tm, ROUTE_W), row),
                pl.BlockSpec(memory_space=pl.ANY),
                pl.BlockSpec((1, D_MODEL), fixed)]
    scratch = [pltpu.VMEM((2, 2 * tm * ROW_TILE, LANES), F32), pltpu.SemaphoreType.DMA((2,))]
    if w is None:
        return pl.pallas_call(
            _combine_final_kernel,
            grid=(T // tm,),
            in_specs=in_specs,
            out_specs=pl.BlockSpec((tm, D_MODEL), row),
            out_shape=jax.ShapeDtypeStruct((T, D_MODEL), F32),
            scratch_shapes=scratch,
            compiler_params=_cparams("arbitrary"),
            name="combine_final",
        )(pos, pos, x2d, route, ys, nw)
    return pl.pallas_call(
        _combine_inproj_kernel,
        grid=(T // tm,),
        in_specs=in_specs + [pl.BlockSpec((D_MODEL, ZA_W + ZB_W + ZC_W), fixed)],
        out_specs=[pl.BlockSpec((tm, D_MODEL), row),
                   pl.BlockSpec((tm, ZA_W), row),
                   pl.BlockSpec((tm, ZB_W), row),
                   pl.BlockSpec((tm, ZC_W), row)],
        out_shape=[jax.ShapeDtypeStruct((T, D_MODEL), F32),
                   jax.ShapeDtypeStruct((T, ZA_W), BF16),
                   jax.ShapeDtypeStruct((T, ZB_W), BF16),
                   jax.ShapeDtypeStruct((T, ZC_W), BF16)],
        scratch_shapes=scratch,
        compiler_params=_cparams("arbitrary"),
        name="combine_inproj",
    )(pos, pos, x2d, route, ys, nw, w)


def _pad_value_heads(t, axis):
    shape = t.shape
    t = t.reshape(shape[:axis] + (B_HEADS, B_DV) + shape[axis + 1:])
    pad = [(0, 0)] * t.ndim
    pad[axis + 1] = (0, B_VSLOT - B_DV)
    return jnp.pad(t, pad).reshape(shape[:axis] + (B_VW,) + shape[axis + 1:])


def _layout_in_weights(w):
    D = w.shape[0]
    nqk, nv = B_HEADS * B_DK, B_HEADS * B_DV
    o = ZA_W

    def qk(block):
        t = block.reshape(D, B_HEADS, 2, B_HALF).transpose(0, 2, 1, 3)
        t = jnp.pad(t, ((0, 0), (0, 0), (0, 0), (0, B_QSLOT - B_HALF)))
        return t.reshape(D, B_QW)

    return jnp.concatenate([
        w[:, :o], qk(w[:, o:o + nqk]), qk(w[:, o + nqk:o + 2 * nqk]),
        _pad_value_heads(w[:, o + 2 * nqk:o + 2 * nqk + nv], 1),
        _pad_value_heads(w[:, o + 2 * nqk + nv:o + 2 * nqk + 2 * nv], 1),
        w[:, o + 2 * nqk + 2 * nv:]], axis=1)


def _layout_out_weights(w):
    nv = B_HEADS * B_DV
    return jnp.concatenate([w[:A_WIDTH], _pad_value_heads(w[A_WIDTH:A_WIDTH + nv], 0),
                            w[A_WIDTH + nv:]], axis=0)


def _retention_tables(S):
    log_gamma = jnp.log1p(-jnp.exp2(-5.0 - jnp.arange(B_HEADS, dtype=F32)))
    idx = jnp.arange(CHUNK, dtype=F32)
    rel = idx[:, None] - idx[None, :]
    decay = jnp.where(rel >= 0, jnp.exp(log_gamma[:, None, None] * jnp.maximum(rel, 0.0)), 0.0)
    dstack = decay.reshape(B_HEADS * CHUNK, CHUNK)
    k_decay = jnp.exp(log_gamma[:, None] * (CHUNK - 1.0 - idx)[None, :])
    q_decay = jnp.exp(log_gamma[:, None] * (idx + 1.0)[None, :])
    chunk_gamma = jnp.exp(log_gamma * CHUNK)

    def lanes(per_head):
        t = jnp.repeat(per_head[..., None], B_QSLOT, axis=-1)
        t = t.reshape(per_head.shape[:-1] + (B_HEADS * B_QSLOT,))
        return jnp.concatenate([t, t], axis=-1)

    qdec = lanes(q_decay.T)
    kdec = lanes(k_decay.T)
    cg = lanes(chunk_gamma[None, :])
    row_head = np.arange(B_VW) // B_VSLOT
    lane_head = (np.arange(B_QW) % (B_QW // 2)) // B_QSLOT
    bd = jnp.asarray((row_head[:, None] == lane_head[None, :]).astype(np.float32))

    half = B_HALF
    inv_freq = ROPE_BASE ** (-jnp.arange(half, dtype=F32) / half)
    ang = jnp.arange(S).astype(F32)[:, None] * inv_freq[None, :]
    pad = ((0, 0), (0, B_QSLOT - half))
    cos = jnp.tile(jnp.pad(jnp.cos(ang), pad), (1, B_HEADS))
    sin = jnp.tile(jnp.pad(jnp.sin(ang), pad), (1, B_HEADS))
    return cos, sin, (dstack, qdec, kdec, cg, bd)


def _attention_bias(rel_bias):
    qi = np.arange(C_QB)[:, None]
    km = np.arange(C_BWIN)[None, :]
    lag = (qi // CHUNK + N_PREV_CHUNKS) - km // CHUNK
    in_band = jnp.asarray((lag >= 0) & (lag <= N_PREV_CHUNKS))
    P = C_QB + C_BWIN
    t = np.arange(P)
    t = np.where(t < C_BWIN, t, t - P)
    dist = N_PREV_CHUNKS * CHUNK - t
    row = rel_bias.astype(F32)[:, np.clip(dist, -MAX_REL, MAX_REL) + MAX_REL]
    b = jnp.tile(row, (1, C_QB))[:, :C_QB * (P - 1)].reshape(C_HEADS, C_QB, P - 1)[:, :, :C_BWIN]
    b = jnp.where(in_band[None], b, MASK_VALUE)
    return b.reshape(C_HEADS // 2, 2 * C_QB, C_BWIN)


def _hgrn_tables():
    tri = jnp.asarray(np.tril(np.ones((CHUNK, CHUNK), np.float32))).astype(BF16)
    head = np.arange(A_WIDTH) // A_DK
    same = (head[:, None] == head[None, :]).astype(np.float32)
    return tri, jnp.asarray(same).astype(BF16), jnp.asarray(same)


def _moe_plan(route, T):
    nblk = 2 * T // MOE_BM + N_EXPERTS
    experts = jnp.arange(N_EXPERTS, dtype=jnp.int32)[None, :]
    oh0 = (route[:, 0:1].astype(jnp.int32) == experts).astype(jnp.int32)
    oh1 = (route[:, 1:2].astype(jnp.int32) == experts).astype(jnp.int32)
    c0, c1 = jnp.cumsum(oh0, axis=0), jnp.cumsum(oh1, axis=0)
    first_choice = c0[-1]
    counts = first_choice + c1[-1]
    padded = (counts + MOE_BM - 1) // MOE_BM * MOE_BM
    pad_end = jnp.cumsum(padded)
    pad_start = pad_end - padded
    dest0 = jnp.sum(oh0 * (pad_start[None, :] + c0 - oh0), axis=1)
    dest1 = jnp.sum(oh1 * ((pad_start + first_choice)[None, :] + c1 - oh1), axis=1)
    nused = (pad_end[-1] // MOE_BM).astype(jnp.int32)
    blk_start = jnp.arange(nblk, dtype=jnp.int32) * MOE_BM
    blk_exp = jnp.sum((pad_end[None, :] <= blk_start[:, None]).astype(jnp.int32), axis=1)
    last = jnp.take(blk_exp, jnp.maximum(nused - 1, 0))
    blk_used = jnp.arange(nblk) < nused
    blk_exp = jnp.where(blk_used, jnp.minimum(blk_exp, N_EXPERTS - 1), last)
    later = blk_used[None, :] & (blk_exp[None, :] > blk_exp[:, None])
    next_exp = jnp.min(jnp.where(later, blk_exp[None, :], N_EXPERTS), axis=1)
    next_exp = jnp.where(next_exp < N_EXPERTS, next_exp, -1).astype(jnp.int32)
    tm = TM_COMBINE
    pos = jnp.concatenate([dest0.reshape(T // tm, 1, tm), dest1.reshape(T // tm, 1, tm)], axis=2) * ROW_TILE
    return pos, counts.astype(jnp.int32), pad_start.astype(jnp.int32), blk_exp, next_exp, nused.reshape(1)


def kernel(x, w_in, w_out, norm_mix, norm_ffn, norm_final, hgrn_lb, hgrn_norm, ret_norm, rel_bias,
           router_group, router_expert, expert_w_gate, expert_w_up, expert_w_down):
    B, S, D = x.shape
    T = B * S
    depth = w_in.shape[0]
    assert D == D_MODEL and w_in.shape[1:] == (D_MODEL, 4 * A_WIDTH + 2 * B_HEADS * (B_DK + B_DV) + ZC_W)
    assert w_out.shape[1:] == (A_WIDTH + B_HEADS * B_DV + C_WIDTH, D_MODEL)
    assert expert_w_gate.shape[1:] == (N_EXPERTS, D_MODEL, D_EXPERT)
    assert router_group.shape[2] == N_GROUPS and router_expert.shape[2] == N_EXPERTS
    assert rel_bias.shape == (C_HEADS, 2 * MAX_REL + 1)
    assert S % max(TS_A, TS_B, TQ_C) == 0 and TQ_C == N_PREV_CHUNKS * CHUNK
    assert T % max(TM, TM_COMBINE) == 0 and (2 * T) % MOE_BM == 0

    p = jax.nn.softmax(hgrn_lb.astype(F32), axis=0)
    lower_bounds = jnp.clip(jnp.cumsum(p, axis=0) - p[0], 0.0, 1.0)

    cos, sin, ret_consts = _retention_tables(S)
    bias = _attention_bias(rel_bias)
    tri, eones, bd_a = _hgrn_tables()

    x2d = x.reshape(T, D).astype(F32)
    route = ys = pos = None
    out = None
    for l in range(depth):
        w_in_l = _layout_in_weights(w_in[l]).astype(BF16)
        w_out_l = _layout_out_weights(w_out[l]).astype(BF16)
        ret_nw = _pad_value_heads(ret_norm[l].astype(F32), 0).reshape(1, B_VW)
        nw_mix = norm_mix[l].astype(F32).reshape(1, D)
        if l == 0:
            za, zb, zc = _inproj(x2d, nw_mix, w_in_l)
        else:
            x2d, za, zb, zc = _combine(x2d, route, ys, pos, nw_mix, w_in_l)

        ya = _hgrn(za, lower_bounds[l].reshape(1, A_WIDTH), hgrn_norm[l].astype(F32).reshape(1, A_WIDTH),
                   tri, eones, bd_a, B, S)
        yb = _retention(zb, cos, sin, ret_consts, ret_nw, B, S)
        yc = _attention(zc, bias, B, S)

        wr = jnp.concatenate([router_group[l], router_expert[l]], axis=1).astype(F32)
        wr_hi = wr.astype(BF16)
        wr_lo = (wr - wr_hi.astype(F32)).astype(BF16)
        gap = jnp.zeros((D, ROUTE_W // 2 - wr.shape[1]), BF16)
        wr2 = jnp.concatenate([wr_hi, gap, wr_lo, gap], axis=1)
        nw_ffn = norm_ffn[l].astype(F32).reshape(1, D)
        x2d, route = _outproj(x2d, ya, yb, yc, w_out_l, nw_ffn, wr2)

        pos, counts, pad_start, blk_exp, next_exp, nused = _moe_plan(route, T)
        xs = _dispatch(x2d, nw_ffn, pos, counts, pad_start, blk_exp.shape[0] * MOE_BM)
        ys = _experts(xs, blk_exp, next_exp, nused, l, expert_w_gate, expert_w_up, expert_w_down)

    out = _combine(x2d, route, ys, pos, norm_final.astype(F32).reshape(1, D))
    return out.reshape(B, S, D).astype(x.dtype)
```

```python
import functools

import numpy as np
import jax
import jax.numpy as jnp
from jax import lax
from jax.experimental import pallas as pl
from jax.experimental.pallas import tpu as pltpu

F32 = jnp.float32
BF16 = jnp.bfloat16

D_MODEL = 1024
CHUNK = 64
NORM_EPS = 1e-6
MASK_VALUE = -1e30
MIN_FORGET = 1e-30
LOG2_E = 1.4426950408889634

A_HEADS, A_DK, A_DV = 4, 64, 64
A_WIDTH = A_HEADS * A_DV
B_HEADS, B_DK, B_DV = 4, 48, 96
B_HALF = B_DK // 2
B_QSLOT = 32
B_VSLOT = 128
B_QW = 2 * B_HEADS * B_QSLOT
B_VW = B_HEADS * B_VSLOT
ROPE_BASE = 10000.0
C_HEADS, C_DH = 6, 64
C_WIDTH = C_HEADS * C_DH
N_PREV_CHUNKS = 8
MAX_REL = 128
C_QB = 2 * CHUNK
C_BWIN = C_QB + N_PREV_CHUNKS * CHUNK
C_KT = 128

ZA_W = 4 * A_WIDTH
ZB_W = 2 * B_QW + 2 * B_VW
ZC_W = 3 * C_WIDTH
Y_W = A_WIDTH + B_VW + C_WIDTH

N_GROUPS, EXPERTS_PER_GROUP = 4, 8
N_EXPERTS = N_GROUPS * EXPERTS_PER_GROUP
D_EXPERT = 512
ROUTE_W = 128

SUB = 8
V7X_VMEM_BYTES = 64 * 1024 * 1024
VMEM_LIMIT = V7X_VMEM_BYTES * 7 // 8

TM = 1024
TS_A = 1024
TS_B = 1024
TQ_C = 512
MOE_BM = 512
TM_COMBINE = 512
TM_DISPATCH = TM_COMBINE


def _cparams(*sem):
    return pltpu.CompilerParams(dimension_semantics=sem, vmem_limit_bytes=VMEM_LIMIT)


def _sigmoid(x):
    return 1.0 / (1.0 + jnp.exp(-x))


def _silu(x):
    return x * _sigmoid(x)


def _split3(x):
    hi = x.astype(BF16)
    r1 = x - hi.astype(F32)
    mid = r1.astype(BF16)
    lo = (r1 - mid.astype(F32)).astype(BF16)
    return hi, mid, lo


def _dot(a, b):
    return jnp.dot(a, b, preferred_element_type=F32)


def _dot_nt(a, b):
    return lax.dot_general(a, b, (((1,), (1,)), ((), ())), preferred_element_type=F32)


def _dot_tn(a, b):
    return lax.dot_general(a, b, (((0,), (0,)), ((), ())), preferred_element_type=F32)


LANES = 128
ROW_TILE = D_MODEL // LANES


def _store_row_tiles(ref, val):
    n = val.shape[0]
    for c in range(ROW_TILE):
        ref[pl.ds(c, n, stride=ROW_TILE), :] = val[:, c * LANES:(c + 1) * LANES]


def _load_row_tiles(ref, first_row, n):
    return jnp.concatenate(
        [ref[pl.ds(first_row * ROW_TILE + c, n, stride=ROW_TILE), :] for c in range(ROW_TILE)], axis=1)


def _rms(x, w):
    ms = jnp.mean(x * x, axis=-1, keepdims=True)
    return x * lax.rsqrt(ms + NORM_EPS) * w


def _project(h, w_ref, za_ref, zb_ref, zc_ref):
    hb = h.astype(BF16)
    za_ref[...] = _dot(hb, w_ref[:, 0:ZA_W]).astype(BF16)
    zb_ref[...] = _dot(hb, w_ref[:, ZA_W:ZA_W + ZB_W]).astype(BF16)
    zc_ref[...] = _dot(hb, w_ref[:, ZA_W + ZB_W:]).astype(BF16)


def _inproj_kernel(x_ref, nw_ref, w_ref, za_ref, zb_ref, zc_ref):
    _project(_rms(x_ref[...], nw_ref[...]), w_ref, za_ref, zb_ref, zc_ref)


def _inproj(x2d, nw, w):
    T = x2d.shape[0]
    row = lambda i: (i, 0)
    fixed = lambda i: (0, 0)
    return pl.pallas_call(
        _inproj_kernel,
        grid=(T // TM,),
        in_specs=[pl.BlockSpec((TM, D_MODEL), row),
                  pl.BlockSpec((1, D_MODEL), fixed),
                  pl.BlockSpec((D_MODEL, ZA_W + ZB_W + ZC_W), fixed)],
        out_specs=[pl.BlockSpec((TM, ZA_W), row),
                   pl.BlockSpec((TM, ZB_W), row),
                   pl.BlockSpec((TM, ZC_W), row)],
        out_shape=[jax.ShapeDtypeStruct((T, ZA_W), BF16),
                   jax.ShapeDtypeStruct((T, ZB_W), BF16),
                   jax.ShapeDtypeStruct((T, ZC_W), BF16)],
        compiler_params=_cparams("parallel"),
        name="inproj",
    )(x2d, nw, w)


def _hgrn_kernel(z_ref, lb_ref, nw_ref, tri_ref, eones_ref, bd_ref, o_ref, st_ref):
    @pl.when(pl.program_id(1) == 0)
    def _():
        st_ref[...] = jnp.zeros_like(st_ref)

    W = A_WIDTH
    lb = lb_ref[...]
    lane_head = lax.broadcasted_iota(jnp.int32, (1, W), 1) // A_DK
    head_masks = [lane_head == h for h in range(A_HEADS)]
    tri = tri_ref[...]
    eones = eones_ref[...]
    bd = bd_ref[...]
    jj = lax.broadcasted_iota(jnp.int32, (SUB, SUB, 1), 0)
    ii = lax.broadcasted_iota(jnp.int32, (SUB, SUB, 1), 1)
    causal = jj <= ii

    n_chunks = z_ref.shape[0] // CHUNK
    n_sub = CHUNK // SUB
    outs = []
    for c in range(n_chunks):
        r0 = c * CHUNK
        zq = z_ref[r0:r0 + CHUNK, 0:W].astype(F32)
        zf = z_ref[r0:r0 + CHUNK, W:2 * W].astype(F32)
        vv = z_ref[r0:r0 + CHUNK, 2 * W:3 * W]
        vf = vv.astype(F32)
        f = lb + (1.0 - lb) * _sigmoid(zf)
        lf = jnp.log(jnp.maximum(f, MIN_FORGET))
        kk = (1.0 - lb) * _sigmoid(-zf)
        qf = _silu(zq) * (A_DK ** -0.5)
        hi, mid, lo = _split3(lf)
        G = (_dot(tri, hi) + _dot(tri, mid) + _dot(tri, lo)) * LOG2_E
        g_last = G[CHUNK - 1:CHUNK, :]

        st = st_ref[...]
        inter = _dot_nt((qf * jnp.exp2(G)).astype(BF16), st.astype(BF16))
        khat = kk * jnp.exp2(g_last - G)
        ut = _dot_tn(vv, khat.astype(BF16))
        st_ref[...] = jnp.exp2(g_last) * st + ut * bd

        p_rows = []
        for s in range(1, n_sub):
            b = G[s * SUB - 1:s * SUB, :]
            qt = qf[s * SUB:(s + 1) * SUB, :] * jnp.exp2(G[s * SUB:(s + 1) * SUB, :] - b)
            kt = (kk[:s * SUB] * jnp.exp2(jnp.minimum(b - G[:s * SUB], 0.0))).astype(BF16)
            kt = jnp.concatenate([kt, jnp.zeros((CHUNK - s * SUB, W), BF16)], axis=0)
            lhs = jnp.concatenate([jnp.where(m, qt, 0.0) for m in head_masks], axis=0)
            p_rows.append(_dot_nt(lhs.astype(BF16), kt))
        pall = jnp.concatenate(p_rows, axis=0)
        pv = _dot(pall.astype(BF16), vv)

        pieces = []
        for s in range(n_sub):
            sl = slice(s * SUB, (s + 1) * SUB)
            gb, qb, kb, vb = G[sl], qf[sl], kk[sl], vf[sl]
            dec = jnp.exp2(jnp.minimum(gb[None, :, :] - gb[:, None, :], 0.0))
            a = (qb[None, :, :] * kb[:, None, :]) * dec
            r = _dot(a.reshape(SUB * SUB, W).astype(BF16), eones).reshape(SUB, SUB, W)
            diag = jnp.sum(jnp.where(causal, r, 0.0) * vb[:, None, :], axis=0)
            piece = inter[sl] + diag
            if s > 0:
                base = (s - 1) * A_HEADS * SUB
                for h in range(A_HEADS):
                    blk = pv[base + h * SUB:base + (h + 1) * SUB, :]
                    piece = piece + jnp.where(head_masks[h], blk, 0.0)
            pieces.append(piece)
        outs.append(jnp.concatenate(pieces, axis=0))

    o = jnp.concatenate(outs, axis=0)
    g = z_ref[:, 3 * W:4 * W].astype(F32)
    o_ref[...] = (_rms(o, nw_ref[...]) * _silu(g)).astype(BF16)


def _hgrn(za, lb, nw, tri, eones, bd, B, S):
    nt = S // TS_A
    fixed = lambda b, t: (0, 0)
    return pl.pallas_call(
        _hgrn_kernel,
        grid=(B, nt),
        in_specs=[pl.BlockSpec((TS_A, ZA_W), lambda b, t: (b * nt + t, 0)),
                  pl.BlockSpec((1, A_WIDTH), fixed),
                  pl.BlockSpec((1, A_WIDTH), fixed),
                  pl.BlockSpec((CHUNK, CHUNK), fixed),
                  pl.BlockSpec((A_WIDTH, A_WIDTH), fixed),
                  pl.BlockSpec((A_WIDTH, A_WIDTH), fixed)],
        out_specs=pl.BlockSpec((TS_A, A_WIDTH), lambda b, t: (b * nt + t, 0)),
        out_shape=jax.ShapeDtypeStruct((B * S, A_WIDTH), BF16),
        scratch_shapes=[pltpu.VMEM((A_WIDTH, A_WIDTH), F32)],
        compiler_params=_cparams("parallel", "arbitrary"),
        name="hgrn2",
    )(za, lb, nw, tri, eones, bd)


def _ret_kernel(z_ref, cos_ref, sin_ref, dstack_ref, qdec_ref, kdec_ref, cg_ref, bd_ref, nw_ref,
                o_ref, st_ref):
    @pl.when(pl.program_id(1) == 0)
    def _():
        st_ref[...] = jnp.zeros_like(st_ref)

    H = B_QW // 2
    lane_head = (lax.broadcasted_iota(jnp.int32, (1, B_QW), 1) % H) // B_QSLOT
    head_masks = [lane_head == h for h in range(B_HEADS)]
    lane_real = lax.broadcasted_iota(jnp.int32, (1, B_VSLOT), 1) < B_DV
    dstack = dstack_ref[...]
    qdec, kdec = qdec_ref[...], kdec_ref[...]
    cg, bd = cg_ref[...], bd_ref[...]

    n_chunks = z_ref.shape[0] // CHUNK
    outs = []
    for c in range(n_chunks):
        rows = slice(c * CHUNK, (c + 1) * CHUNK)
        cos, sin = cos_ref[rows, :], sin_ref[rows, :]

        def rot(off):
            t1 = z_ref[rows, off:off + H].astype(F32)
            t2 = z_ref[rows, off + H:off + 2 * H].astype(F32)
            return jnp.concatenate([t1 * cos - t2 * sin, t1 * sin + t2 * cos], axis=1)

        qr = rot(0)
        kr = rot(B_QW) * (B_DK ** -0.5)
        vv = z_ref[rows, 2 * B_QW:2 * B_QW + B_VW]

        lhs = jnp.concatenate([jnp.where(m, qr, 0.0) for m in head_masks], axis=0)
        sc = _dot_nt(lhs.astype(BF16), kr.astype(BF16)) * dstack
        pv = _dot(sc.astype(BF16), vv)
        intra = jnp.concatenate(
            [pv[h * CHUNK:(h + 1) * CHUNK, h * B_VSLOT:(h + 1) * B_VSLOT] for h in range(B_HEADS)],
            axis=1)
        st = st_ref[...]
        inter = _dot_nt((qr * qdec).astype(BF16), st.astype(BF16))
        ut = _dot_tn(vv, (kr * kdec).astype(BF16))
        st_ref[...] = cg * st + ut * bd
        outs.append(intra + inter)

    o = jnp.concatenate(outs, axis=0)
    normed = []
    for h in range(B_HEADS):
        oh = o[:, h * B_VSLOT:(h + 1) * B_VSLOT]
        mu = jnp.sum(oh, axis=-1, keepdims=True) * (1.0 / B_DV)
        d = jnp.where(lane_real, oh - mu, 0.0)
        var = jnp.sum(d * d, axis=-1, keepdims=True) * (1.0 / B_DV)
        normed.append(d * lax.rsqrt(var + NORM_EPS))
    y = jnp.concatenate(normed, axis=1)
    g = z_ref[:, 2 * B_QW + B_VW:].astype(F32)
    o_ref[...] = (y * nw_ref[...] * _silu(g)).astype(BF16)


def _retention(zb, cos, sin, consts, nw, B, S):
    nt = S // TS_B
    fixed = lambda b, t: (0, 0)
    dstack, qdec, kdec, cg, bd = consts
    return pl.pallas_call(
        _ret_kernel,
        grid=(B, nt),
        in_specs=[pl.BlockSpec((TS_B, ZB_W), lambda b, t: (b * nt + t, 0)),
                  pl.BlockSpec((TS_B, B_QW // 2), lambda b, t: (t, 0)),
                  pl.BlockSpec((TS_B, B_QW // 2), lambda b, t: (t, 0)),
                  pl.BlockSpec(dstack.shape, fixed),
                  pl.BlockSpec(qdec.shape, fixed),
                  pl.BlockSpec(kdec.shape, fixed),
                  pl.BlockSpec(cg.shape, fixed),
                  pl.BlockSpec(bd.shape, fixed),
                  pl.BlockSpec((1, B_VW), fixed)],
        out_specs=pl.BlockSpec((TS_B, B_VW), lambda b, t: (b * nt + t, 0)),
        out_shape=jax.ShapeDtypeStruct((B * S, B_VW), BF16),
        scratch_shapes=[pltpu.VMEM((B_VW, B_QW), F32)],
        compiler_params=_cparams("parallel", "arbitrary"),
        name="retention",
    )(zb, cos, sin, dstack, qdec, kdec, cg, bd, nw)


def _attn_kernel(q_ref, kp_ref, kc_ref, vp_ref, vc_ref, bias_ref, o_ref, kwin_ref, vwin_ref):
    t = pl.program_id(1)
    kwin_ref[0:TQ_C, :] = kp_ref[...]
    kwin_ref[TQ_C:, :] = kc_ref[...]
    vwin_ref[0:TQ_C, :] = vp_ref[...]
    vwin_ref[TQ_C:, :] = vc_ref[...]
    jcol = lax.broadcasted_iota(jnp.int32, (1, C_BWIN), 1)
    first_head = lax.broadcasted_iota(jnp.int32, (1, 2 * C_DH), 1) < C_DH
    zero = jnp.zeros((), BF16)

    def block(i, carry):
        r0 = pl.multiple_of(i * C_QB, C_QB)
        pen = jnp.where((jcol + (r0 + (t - 1) * TQ_C)) >= 0, 0.0, MASK_VALUE)
        for p in range(C_HEADS // 2):
            lanes = slice(2 * p * C_DH, 2 * (p + 1) * C_DH)
            q = q_ref[pl.ds(r0, C_QB), lanes] * (C_DH ** -0.5)
            lhs = jnp.concatenate([jnp.where(first_head, q, zero), jnp.where(first_head, zero, q)], axis=0)

            def scores(j):
                k = kwin_ref[pl.ds(r0 + j * C_KT, C_KT), lanes]
                return (_dot_nt(lhs, k) + bias_ref[p, :, j * C_KT:(j + 1) * C_KT]
                        + pen[:, j * C_KT:(j + 1) * C_KT])

            m = scores(0)
            for j in range(1, C_BWIN // C_KT):
                m = jnp.maximum(m, scores(j))
            m = jnp.max(m, axis=-1, keepdims=True)
            acc = l = None
            for j in range(C_BWIN // C_KT):
                e = jnp.exp(scores(j) - m)
                pvj = _dot(e.astype(BF16), vwin_ref[pl.ds(r0 + j * C_KT, C_KT), lanes])
                acc = pvj if acc is None else acc + pvj
                l = e if l is None else l + e
            pv = acc / jnp.sum(l, axis=-1, keepdims=True)
            o = jnp.where(first_head, pv[:C_QB], pv[C_QB:])
            o_ref[pl.ds(r0, C_QB), lanes] = o.astype(BF16)
        return carry

    lax.fori_loop(0, TQ_C // C_QB, block, 0, unroll=True)


def _attention(zc, bias, B, S):
    nt = S // TQ_C
    cur = lambda col: (lambda b, t: (b * nt + t, col))
    prev = lambda col: (lambda b, t: (b * nt + jnp.maximum(t - 1, 0), col))
    blk = (TQ_C, C_WIDTH)
    return pl.pallas_call(
        _attn_kernel,
        grid=(B, nt),
        in_specs=[pl.BlockSpec(blk, cur(0)),
                  pl.BlockSpec(blk, prev(1)), pl.BlockSpec(blk, cur(1)),
                  pl.BlockSpec(blk, prev(2)), pl.BlockSpec(blk, cur(2)),
                  pl.BlockSpec(bias.shape, lambda b, t: (0, 0, 0))],
        out_specs=pl.BlockSpec(blk, cur(0)),
        out_shape=jax.ShapeDtypeStruct((B * S, C_WIDTH), BF16),
        scratch_shapes=[pltpu.VMEM((2 * TQ_C, C_WIDTH), BF16),
                        pltpu.VMEM((2 * TQ_C, C_WIDTH), BF16)],
        compiler_params=_cparams("parallel", "arbitrary"),
        name="chunk_attention",
    )(zc, zc, zc, zc, zc, bias)


def _route(logits):
    lane = lax.broadcasted_iota(jnp.int32, logits.shape, 1).astype(F32)
    big = float(1 << 20)
    neg = -jnp.inf

    def first_argmax(vals):
        m = jnp.max(vals, axis=-1, keepdims=True)
        idx = jnp.min(jnp.where(vals == m, lane, big), axis=-1, keepdims=True)
        return m, idx

    gl = jnp.where(lane < N_GROUPS, logits, neg)
    gm, grp = first_argmax(gl)
    p_grp = 1.0 / jnp.sum(jnp.exp(gl - gm), axis=-1, keepdims=True)
    lo = N_GROUPS + grp * EXPERTS_PER_GROUP
    el = jnp.where((lane >= lo) & (lane < lo + EXPERTS_PER_GROUP), logits, neg)
    v1, i1 = first_argmax(el)
    v2, i2 = first_argmax(jnp.where(lane == i1, neg, el))
    e2 = jnp.exp(v2 - v1)
    g1 = p_grp / (1.0 + e2)
    g2 = p_grp * e2 / (1.0 + e2)
    out = jnp.where(lane == 0, i1 - N_GROUPS, 0.0)
    out = jnp.where(lane == 1, i2 - N_GROUPS, out)
    out = jnp.where(lane == 2, g1, out)
    out = jnp.where(lane == 3, g2, out)
    return out


def _outproj_kernel(x_ref, ya_ref, yb_ref, yc_ref, w_ref, nw_ref, wr_ref, xo_ref, r_ref):
    a0, a1 = A_WIDTH, A_WIDTH + B_VW
    x = x_ref[...]
    x = x + _dot(ya_ref[...], w_ref[0:a0, :])
    x = x + _dot(yb_ref[...], w_ref[a0:a1, :])
    x = x + _dot(yc_ref[...], w_ref[a1:, :])
    xo_ref[...] = x
    h = _rms(x, nw_ref[...])
    h_hi = h.astype(BF16)
    h_lo = (h - h_hi.astype(F32)).astype(BF16)
    d_hi = _dot(h_hi, wr_ref[...])
    logits = d_hi + pltpu.roll(d_hi, ROUTE_W // 2, 1) + _dot(h_lo, wr_ref[...])
    r_ref[...] = _route(logits)


def _outproj(x2d, ya, yb, yc, w, nw, wr3):
    T = x2d.shape[0]
    row = lambda i: (i, 0)
    fixed = lambda i: (0, 0)
    return pl.pallas_call(
        _outproj_kernel,
        grid=(T // TM,),
        in_specs=[pl.BlockSpec((TM, D_MODEL), row),
                  pl.BlockSpec((TM, A_WIDTH), row),
                  pl.BlockSpec((TM, B_VW), row),
                  pl.BlockSpec((TM, C_WIDTH), row),
                  pl.BlockSpec((Y_W, D_MODEL), fixed),
                  pl.BlockSpec((1, D_MODEL), fixed),
                  pl.BlockSpec((D_MODEL, ROUTE_W), fixed)],
        out_specs=[pl.BlockSpec((TM, D_MODEL), row),
                   pl.BlockSpec((TM, ROUTE_W), row)],
        out_shape=[jax.ShapeDtypeStruct((T, D_MODEL), F32),
                   jax.ShapeDtypeStruct((T, ROUTE_W), F32)],
        compiler_params=_cparams("parallel"),
        name="outproj_router",
    )(x2d, ya, yb, yc, w, nw, wr3)


PAD_PIECES = tuple(MOE_BM >> (k + 1) for k in range(MOE_BM.bit_length() - 1))
DMA_GROUP = 8


def _tile_rows(ref, first_tile_row, n_rows=1):
    return ref.at[pl.ds(pl.multiple_of(first_tile_row, ROW_TILE), n_rows * ROW_TILE)]


def _rows_done(src_hbm, dst, sem, n_rows):
    pltpu.make_async_copy(_tile_rows(src_hbm, 0, n_rows), _tile_rows(dst, 0, n_rows), sem).wait()


def _dispatch_kernel(cnt_ref, pstart_ref, pos_ref, x_ref, nw_ref, xs_hbm, hbuf, zbuf, sem, zsem):
    i = pl.program_id(0)
    n = pl.num_programs(0)
    tm = TM_DISPATCH
    cur = lax.rem(i, 2)
    stage = hbuf.at[cur]
    _store_row_tiles(stage, _rms(x_ref[...], nw_ref[...]))

    def pad_pieces(e):
        cnt = cnt_ref[e]
        first = pstart_ref[e] + cnt
        n_pad = (-cnt) & (MOE_BM - 1)
        for p in PAD_PIECES:
            yield (n_pad & p) != 0, first + (n_pad & (MOE_BM - 2 * p)), p

    def spare_blocks():
        last = N_EXPERTS - 1
        n_used = (pstart_ref[last] + cnt_ref[last] + (MOE_BM - 1)) // MOE_BM
        n_blocks = xs_hbm.shape[0] // (MOE_BM * ROW_TILE)
        for b in range(n_blocks - N_EXPERTS, n_blocks):
            for first in range(0, MOE_BM, PAD_PIECES[0]):
                yield b >= n_used, b * MOE_BM + first, PAD_PIECES[0]

    @pl.when(i == 0)
    def _():
        zbuf[...] = jnp.zeros_like(zbuf)
        for phase in ("start", "wait"):
            for e in range(N_EXPERTS + 1):
                for fire, slot, p in (pad_pieces(e) if e < N_EXPERTS else spare_blocks()):
                    @pl.when(fire)
                    def _():
                        cp = pltpu.make_async_copy(_tile_rows(zbuf, 0, p),
                                                   _tile_rows(xs_hbm, slot * ROW_TILE, p), zsem)
                        cp.start() if phase == "start" else cp.wait()

    def body(g, carry):
        rows = [g * DMA_GROUP + k for k in range(DMA_GROUP)]
        slots = [(pos_ref[0, 0, r], pos_ref[0, 0, tm + r]) for r in rows]
        for r, (s0, s1) in zip(rows, slots):
            src = _tile_rows(stage, r * ROW_TILE)
            pltpu.make_async_copy(src, _tile_rows(xs_hbm, s0), sem.at[cur]).start(priority=0)
            pltpu.make_async_copy(src, _tile_rows(xs_hbm, s1), sem.at[cur]).start(priority=1)
        return carry
    lax.fori_loop(0, tm // DMA_GROUP, body, 0)

    @pl.when(i > 0)
    def _():
        _rows_done(stage, xs_hbm, sem.at[1 - cur], 2 * tm)

    @pl.when(i == n - 1)
    def _():
        _rows_done(stage, xs_hbm, sem.at[cur], 2 * tm)


def _dispatch(x2d, nw, pos, counts, pad_start, n_slots):
    T = x2d.shape[0]
    tm = TM_DISPATCH
    grid_spec = pltpu.PrefetchScalarGridSpec(
        num_scalar_prefetch=2,
        grid=(T // tm,),
        in_specs=[pl.BlockSpec((1, 1, 2 * tm), lambda i, c, s: (i, 0, 0), memory_space=pltpu.SMEM),
                  pl.BlockSpec((tm, D_MODEL), lambda i, c, s: (i, 0)),
                  pl.BlockSpec((1, D_MODEL), lambda i, c, s: (0, 0))],
        out_specs=pl.BlockSpec(memory_space=pl.ANY),
        scratch_shapes=[pltpu.VMEM((2, tm * ROW_TILE, LANES), F32),
                        pltpu.VMEM((PAD_PIECES[0] * ROW_TILE, LANES), F32),
                        pltpu.SemaphoreType.DMA((2,)), pltpu.SemaphoreType.DMA(())],
    )
    return pl.pallas_call(
        _dispatch_kernel,
        grid_spec=grid_spec,
        out_shape=jax.ShapeDtypeStruct((n_slots * ROW_TILE, LANES), F32),
        compiler_params=_cparams("arbitrary"),
        name="dispatch",
    )(counts, pad_start, pos, x2d, nw)


def _expert_kernel(blk_exp_ref, next_exp_ref, nused_ref, x_ref, wg_hbm, wu_hbm, wd_hbm, ys_hbm,
                   wg_f, wu_f, wd_f, wg_bf, wu_bf, wd_bf, ybuf, wsem, ysem, *, layer):
    i = pl.program_id(0)
    n = pl.num_programs(0)
    used = i < nused_ref[0]
    slot = lax.rem(i, 2)
    block_rows = MOE_BM * ROW_TILE

    def weight_copies(e):
        return (pltpu.make_async_copy(wg_hbm.at[layer, e], wg_f, wsem.at[0]),
                pltpu.make_async_copy(wu_hbm.at[layer, e], wu_f, wsem.at[1]),
                pltpu.make_async_copy(wd_hbm.at[layer, e], wd_f, wsem.at[2]))

    def result_copy(s, block):
        dst = ys_hbm.at[pl.ds(pl.multiple_of(block * block_rows, block_rows), block_rows)]
        return pltpu.make_async_copy(ybuf.at[s], dst, ysem.at[s])

    @pl.when(i == 0)
    def _():
        for cp in weight_copies(blk_exp_ref[0]):
            cp.start(priority=1)

    @pl.when((i == 0) | (blk_exp_ref[i] != blk_exp_ref[jnp.maximum(i - 1, 0)]))
    def _():
        for cp in weight_copies(blk_exp_ref[i]):
            cp.wait()
        wg_bf[...] = wg_f[...].astype(BF16)
        wu_bf[...] = wu_f[...].astype(BF16)
        wd_bf[...] = wd_f[...].astype(BF16)

        @pl.when(next_exp_ref[i] >= 0)
        def _():
            for cp in weight_copies(next_exp_ref[i]):
                cp.start(priority=1)

    @pl.when(i >= 2)
    def _():
        result_copy(slot, i - 2).wait()

    @pl.when(used)
    def _():
        xb = _load_row_tiles(x_ref, 0, MOE_BM).astype(BF16)
        act = _silu(_dot(xb, wg_bf[...])) * _dot(xb, wu_bf[...])
        _store_row_tiles(ybuf.at[slot], _dot(act.astype(BF16), wd_bf[...]))

    @pl.when(jnp.logical_not(used))
    def _():
        ybuf[slot] = jnp.zeros((block_rows, LANES), F32)

    result_copy(slot, i).start(priority=1)

    @pl.when(i == n - 1)
    def _():
        result_copy(1 - slot, i - 1).wait()
        result_copy(slot, i).wait()


def _experts(xs, blk_exp, next_exp, nused, layer, w_gate, w_up, w_down):
    nblk = blk_exp.shape[0]
    assert nblk >= 2
    last_used = lambda i, nu: jnp.minimum(i, nu[0] - 1)
    grid_spec = pltpu.PrefetchScalarGridSpec(
        num_scalar_prefetch=3,
        grid=(nblk,),
        in_specs=[pl.BlockSpec((MOE_BM * ROW_TILE, LANES), lambda i, be, nx, nu: (last_used(i, nu), 0)),
                  pl.BlockSpec(memory_space=pl.ANY),
                  pl.BlockSpec(memory_space=pl.ANY),
                  pl.BlockSpec(memory_space=pl.ANY)],
        out_specs=pl.BlockSpec(memory_space=pl.ANY),
        scratch_shapes=[pltpu.VMEM((D_MODEL, D_EXPERT), F32), pltpu.VMEM((D_MODEL, D_EXPERT), F32),
                        pltpu.VMEM((D_EXPERT, D_MODEL), F32),
                        pltpu.VMEM((D_MODEL, D_EXPERT), BF16), pltpu.VMEM((D_MODEL, D_EXPERT), BF16),
                        pltpu.VMEM((D_EXPERT, D_MODEL), BF16),
                        pltpu.VMEM((2, MOE_BM * ROW_TILE, LANES), F32),
                        pltpu.SemaphoreType.DMA((3,)), pltpu.SemaphoreType.DMA((2,))],
    )
    return pl.pallas_call(
        functools.partial(_expert_kernel, layer=layer),
        grid_spec=grid_spec,
        out_shape=jax.ShapeDtypeStruct((nblk * MOE_BM * ROW_TILE, LANES), F32),
        compiler_params=_cparams("arbitrary"),
        name="experts",
    )(blk_exp, next_exp, nused, xs, w_gate, w_up, w_down)


def _combine_rows(pos_ref, pos_next_ref, x_ref, r_ref, y_hbm, ybuf, sem, request_next_first):
    i = pl.program_id(0)
    n_rows = 2 * TM_COMBINE
    slot = lax.rem(i, 2)

    @pl.when(i == 0)
    def _():
        _request_rows(pos_ref, y_hbm, ybuf, sem, 0, unrolled=False)

    if request_next_first:
        _request_next(pos_next_ref, y_hbm, ybuf, sem, unrolled=False)
    _rows_done(y_hbm, ybuf.at[slot], sem.at[slot], n_rows)
    r = r_ref[...]
    g0, g1 = r[:, 2:3], r[:, 3:4]
    y0 = _load_row_tiles(ybuf.at[slot], 0, TM_COMBINE)
    y1 = _load_row_tiles(ybuf.at[slot], TM_COMBINE, TM_COMBINE)
    return x_ref[...] + (g0 * y0 + g1 * y1)


def _request_rows(idx_ref, y_hbm, ybuf, sem, s, unrolled):
    n_rows = 2 * TM_COMBINE
    group = 2 * DMA_GROUP

    def body(g, carry):
        rows = [g * group + k for k in range(group)]
        slots = [idx_ref[0, 0, r] for r in rows]
        for k, src_row in enumerate(slots):
            pltpu.make_async_copy(_tile_rows(y_hbm, src_row),
                                  _tile_rows(ybuf.at[s], rows[k] * ROW_TILE), sem.at[s]).start(priority=k % 2)
        return carry

    if unrolled:
        for g in range(n_rows // group):
            body(g, 0)
    else:
        lax.fori_loop(0, n_rows // group, body, 0)


def _request_next(pos_next_ref, y_hbm, ybuf, sem, unrolled):
    _request_rows(pos_next_ref, y_hbm, ybuf, sem, 1 - lax.rem(pl.program_id(0), 2), unrolled)


def _drain_last(y_hbm, ybuf, sem):
    i = pl.program_id(0)

    @pl.when(i == pl.num_programs(0) - 1)
    def _():
        other = 1 - lax.rem(i, 2)
        _rows_done(y_hbm, ybuf.at[other], sem.at[other], 2 * TM_COMBINE)


def _combine_inproj_kernel(pos_ref, pos_next_ref, x_ref, r_ref, y_hbm, nw_ref, w_ref,
                           xo_ref, za_ref, zb_ref, zc_ref, ybuf, sem):
    x = _combine_rows(pos_ref, pos_next_ref, x_ref, r_ref, y_hbm, ybuf, sem, request_next_first=False)
    xo_ref[...] = x
    _request_next(pos_next_ref, y_hbm, ybuf, sem, unrolled=True)
    _project(_rms(x, nw_ref[...]), w_ref, za_ref, zb_ref, zc_ref)
    _drain_last(y_hbm, ybuf, sem)


def _combine_final_kernel(pos_ref, pos_next_ref, x_ref, r_ref, y_hbm, nw_ref, o_ref, ybuf, sem):
    x = _combine_rows(pos_ref, pos_next_ref, x_ref, r_ref, y_hbm, ybuf, sem, request_next_first=True)
    o_ref[...] = _rms(x, nw_ref[...])
    _drain_last(y_hbm, ybuf, sem)


def _combine(x2d, route, ys, pos, nw, w=None):
    T = x2d.shape[0]
    tm = TM_COMBINE
    row = lambda i: (i, 0)
    fixed = lambda i: (0, 0)
    n_tiles = T // tm
    in_specs = [pl.BlockSpec((1, 1, 2 * tm), lambda i: (i, 0, 0), memory_space=pltpu.SMEM),
                pl.BlockSpec((1, 1, 2 * tm), lambda i: (jnp.minimum(i + 1, n_tiles - 1), 0, 0),
                             memory_space=pltpu.SMEM),
                pl.BlockSpec((tm, D_MODEL), row),
                pl.BlockSpec((tm, ROUTE_W), row),
                pl.BlockSpec(memory_space=pl.ANY),
                pl.BlockSpec((1, D_MODEL), fixed)]
    scratch = [pltpu.VMEM((2, 2 * tm * ROW_TILE, LANES), F32), pltpu.SemaphoreType.DMA((2,))]
    if w is None:
        return pl.pallas_call(
            _combine_final_kernel,
            grid=(T // tm,),
            in_specs=in_specs,
            out_specs=pl.BlockSpec((tm, D_MODEL), row),
            out_shape=jax.ShapeDtypeStruct((T, D_MODEL), F32),
            scratch_shapes=scratch,
            compiler_params=_cparams("arbitrary"),
            name="combine_final",
        )(pos, pos, x2d, route, ys, nw)
    return pl.pallas_call(
        _combine_inproj_kernel,
        grid=(T // tm,),
        in_specs=in_specs + [pl.BlockSpec((D_MODEL, ZA_W + ZB_W + ZC_W), fixed)],
        out_specs=[pl.BlockSpec((tm, D_MODEL), row),
                   pl.BlockSpec((tm, ZA_W), row),
                   pl.BlockSpec((tm, ZB_W), row),
                   pl.BlockSpec((tm, ZC_W), row)],
        out_shape=[jax.ShapeDtypeStruct((T, D_MODEL), F32),
                   jax.ShapeDtypeStruct((T, ZA_W), BF16),
                   jax.ShapeDtypeStruct((T, ZB_W), BF16),
                   jax.ShapeDtypeStruct((T, ZC_W), BF16)],
        scratch_shapes=scratch,
        compiler_params=_cparams("arbitrary"),
        name="combine_inproj",
    )(pos, pos, x2d, route, ys, nw, w)


def _pad_value_heads(t, axis):
    shape = t.shape
    t = t.reshape(shape[:axis] + (B_HEADS, B_DV) + shape[axis + 1:])
    pad = [(0, 0)] * t.ndim
    pad[axis + 1] = (0, B_VSLOT - B_DV)
    return jnp.pad(t, pad).reshape(shape[:axis] + (B_VW,) + shape[axis + 1:])


def _layout_in_weights(w):
    D = w.shape[0]
    nqk, nv = B_HEADS * B_DK, B_HEADS * B_DV
    o = ZA_W

    def qk(block):
        t = block.reshape(D, B_HEADS, 2, B_HALF).transpose(0, 2, 1, 3)
        t = jnp.pad(t, ((0, 0), (0, 0), (0, 0), (0, B_QSLOT - B_HALF)))
        return t.reshape(D, B_QW)

    return jnp.concatenate([
        w[:, :o], qk(w[:, o:o + nqk]), qk(w[:, o + nqk:o + 2 * nqk]),
        _pad_value_heads(w[:, o + 2 * nqk:o + 2 * nqk + nv], 1),
        _pad_value_heads(w[:, o + 2 * nqk + nv:o + 2 * nqk + 2 * nv], 1),
        w[:, o + 2 * nqk + 2 * nv:]], axis=1)


def _layout_out_weights(w):
    nv = B_HEADS * B_DV
    return jnp.concatenate([w[:A_WIDTH], _pad_value_heads(w[A_WIDTH:A_WIDTH + nv], 0),
                            w[A_WIDTH + nv:]], axis=0)


def _retention_tables(S):
    log_gamma = jnp.log1p(-jnp.exp2(-5.0 - jnp.arange(B_HEADS, dtype=F32)))
    idx = jnp.arange(CHUNK, dtype=F32)
    rel = idx[:, None] - idx[None, :]
    decay = jnp.where(rel >= 0, jnp.exp(log_gamma[:, None, None] * jnp.maximum(rel, 0.0)), 0.0)
    dstack = decay.reshape(B_HEADS * CHUNK, CHUNK)
    k_decay = jnp.exp(log_gamma[:, None] * (CHUNK - 1.0 - idx)[None, :])
    q_decay = jnp.exp(log_gamma[:, None] * (idx + 1.0)[None, :])
    chunk_gamma = jnp.exp(log_gamma * CHUNK)

    def lanes(per_head):
        t = jnp.repeat(per_head[..., None], B_QSLOT, axis=-1)
        t = t.reshape(per_head.shape[:-1] + (B_HEADS * B_QSLOT,))
        return jnp.concatenate([t, t], axis=-1)

    qdec = lanes(q_decay.T)
    kdec = lanes(k_decay.T)
    cg = lanes(chunk_gamma[None, :])
    row_head = np.arange(B_VW) // B_VSLOT
    lane_head = (np.arange(B_QW) % (B_QW // 2)) // B_QSLOT
    bd = jnp.asarray((row_head[:, None] == lane_head[None, :]).astype(np.float32))

    half = B_HALF
    inv_freq = ROPE_BASE ** (-jnp.arange(half, dtype=F32) / half)
    ang = jnp.arange(S).astype(F32)[:, None] * inv_freq[None, :]
    pad = ((0, 0), (0, B_QSLOT - half))
    cos = jnp.tile(jnp.pad(jnp.cos(ang), pad), (1, B_HEADS))
    sin = jnp.tile(jnp.pad(jnp.sin(ang), pad), (1, B_HEADS))
    return cos, sin, (dstack, qdec, kdec, cg, bd)


def _attention_bias(rel_bias):
    qi = np.arange(C_QB)[:, None]
    km = np.arange(C_BWIN)[None, :]
    lag = (qi // CHUNK + N_PREV_CHUNKS) - km // CHUNK
    in_band = jnp.asarray((lag >= 0) & (lag <= N_PREV_CHUNKS))
    P = C_QB + C_BWIN
    t = np.arange(P)
    t = np.where(t < C_BWIN, t, t - P)
    dist = N_PREV_CHUNKS * CHUNK - t
    row = rel_bias.astype(F32)[:, np.clip(dist, -MAX_REL, MAX_REL) + MAX_REL]
    b = jnp.tile(row, (1, C_QB))[:, :C_QB * (P - 1)].reshape(C_HEADS, C_QB, P - 1)[:, :, :C_BWIN]
    b = jnp.where(in_band[None], b, MASK_VALUE)
    return b.reshape(C_HEADS // 2, 2 * C_QB, C_BWIN)


def _hgrn_tables():
    tri = jnp.asarray(np.tril(np.ones((CHUNK, CHUNK), np.float32))).astype(BF16)
    head = np.arange(A_WIDTH) // A_DK
    same = (head[:, None] == head[None, :]).astype(np.float32)
    return tri, jnp.asarray(same).astype(BF16), jnp.asarray(same)


def _moe_plan(route, T):
    nblk = 2 * T // MOE_BM + N_EXPERTS
    experts = jnp.arange(N_EXPERTS, dtype=jnp.int32)[None, :]
    oh0 = (route[:, 0:1].astype(jnp.int32) == experts).astype(jnp.int32)
    oh1 = (route[:, 1:2].astype(jnp.int32) == experts).astype(jnp.int32)
    c0, c1 = jnp.cumsum(oh0, axis=0), jnp.cumsum(oh1, axis=0)
    first_choice = c0[-1]
    counts = first_choice + c1[-1]
    padded = (counts + MOE_BM - 1) // MOE_BM * MOE_BM
    pad_end = jnp.cumsum(padded)
    pad_start = pad_end - padded
    dest0 = jnp.sum(oh0 * (pad_start[None, :] + c0 - oh0), axis=1)
    dest1 = jnp.sum(oh1 * ((pad_start + first_choice)[None, :] + c1 - oh1), axis=1)
    nused = (pad_end[-1] // MOE_BM).astype(jnp.int32)
    blk_start = jnp.arange(nblk, dtype=jnp.int32) * MOE_BM
    blk_exp = jnp.sum((pad_end[None, :] <= blk_start[:, None]).astype(jnp.int32), axis=1)
    last = jnp.take(blk_exp, jnp.maximum(nused - 1, 0))
    blk_used = jnp.arange(nblk) < nused
    blk_exp = jnp.where(blk_used, jnp.minimum(blk_exp, N_EXPERTS - 1), last)
    later = blk_used[None, :] & (blk_exp[None, :] > blk_exp[:, None])
    next_exp = jnp.min(jnp.where(later, blk_exp[None, :], N_EXPERTS), axis=1)
    next_exp = jnp.where(next_exp < N_EXPERTS, next_exp, -1).astype(jnp.int32)
    tm = TM_COMBINE
    pos = jnp.concatenate([dest0.reshape(T // tm, 1, tm), dest1.reshape(T // tm, 1, tm)], axis=2) * ROW_TILE
    return pos, counts.astype(jnp.int32), pad_start.astype(jnp.int32), blk_exp, next_exp, nused.reshape(1)


def kernel(x, w_in, w_out, norm_mix, norm_ffn, norm_final, hgrn_lb, hgrn_norm, ret_norm, rel_bias,
           router_group, router_expert, expert_w_gate, expert_w_up, expert_w_down):
    B, S, D = x.shape
    T = B * S
    depth = w_in.shape[0]
    assert D == D_MODEL and w_in.shape[1:] == (D_MODEL, 4 * A_WIDTH + 2 * B_HEADS * (B_DK + B_DV) + ZC_W)
    assert w_out.shape[1:] == (A_WIDTH + B_HEADS * B_DV + C_WIDTH, D_MODEL)
    assert expert_w_gate.shape[1:] == (N_EXPERTS, D_MODEL, D_EXPERT)
    assert router_group.shape[2] == N_GROUPS and router_expert.shape[2] == N_EXPERTS
    assert rel_bias.shape == (C_HEADS, 2 * MAX_REL + 1)
    assert S % max(TS_A, TS_B, TQ_C) == 0 and TQ_C == N_PREV_CHUNKS * CHUNK
    assert T % max(TM, TM_COMBINE) == 0 and (2 * T) % MOE_BM == 0

    p = jax.nn.softmax(hgrn_lb.astype(F32), axis=0)
    lower_bounds = jnp.clip(jnp.cumsum(p, axis=0) - p[0], 0.0, 1.0)

    cos, sin, ret_consts = _retention_tables(S)
    bias = _attention_bias(rel_bias)
    tri, eones, bd_a = _hgrn_tables()

    x2d = x.reshape(T, D).astype(F32)
    route = ys = pos = None
    out = None
    for l in range(depth):
        w_in_l = _layout_in_weights(w_in[l]).astype(BF16)
        w_out_l = _layout_out_weights(w_out[l]).astype(BF16)
        ret_nw = _pad_value_heads(ret_norm[l].astype(F32), 0).reshape(1, B_VW)
        nw_mix = norm_mix[l].astype(F32).reshape(1, D)
        if l == 0:
            za, zb, zc = _inproj(x2d, nw_mix, w_in_l)
        else:
            x2d, za, zb, zc = _combine(x2d, route, ys, pos, nw_mix, w_in_l)

        ya = _hgrn(za, lower_bounds[l].reshape(1, A_WIDTH), hgrn_norm[l].astype(F32).reshape(1, A_WIDTH),
                   tri, eones, bd_a, B, S)
        yb = _retention(zb, cos, sin, ret_consts, ret_nw, B, S)
        yc = _attention(zc, bias, B, S)

        wr = jnp.concatenate([router_group[l], router_expert[l]], axis=1).astype(F32)
        wr_hi = wr.astype(BF16)
        wr_lo = (wr - wr_hi.astype(F32)).astype(BF16)
        gap = jnp.zeros((D, ROUTE_W // 2 - wr.shape[1]), BF16)
        wr2 = jnp.concatenate([wr_hi, gap, wr_lo, gap], axis=1)
        nw_ffn = norm_ffn[l].astype(F32).reshape(1, D)
        x2d, route = _outproj(x2d, ya, yb, yc, w_out_l, nw_ffn, wr2)

        pos, counts, pad_start, blk_exp, next_exp, nused = _moe_plan(route, T)
        xs = _dispatch(x2d, nw_ffn, pos, counts, pad_start, blk_exp.shape[0] * MOE_BM)
        ys = _experts(xs, blk_exp, next_exp, nused, l, expert_w_gate, expert_w_up, expert_w_down)

    out = _combine(x2d, route, ys, pos, norm_final.astype(F32).reshape(1, D))
    return out.reshape(B, S, D).astype(x.dtype)
```

```python
import functools

import numpy as np
import jax
import jax.numpy as jnp
from jax import lax
from jax.experimental import pallas as pl
from jax.experimental.pallas import tpu as pltpu

F32 = jnp.float32
BF16 = jnp.bfloat16

D_MODEL = 1024
CHUNK = 64
NORM_EPS = 1e-6
MASK_VALUE = -1e30
MIN_FORGET = 1e-30
LOG2_E = 1.4426950408889634

A_HEADS, A_DK, A_DV = 4, 64, 64
A_WIDTH = A_HEADS * A_DV
B_HEADS, B_DK, B_DV = 4, 48, 96
B_HALF = B_DK // 2
B_QSLOT = 32
B_VSLOT = 128
B_QW = 2 * B_HEADS * B_QSLOT
B_VW = B_HEADS * B_VSLOT
ROPE_BASE = 10000.0
C_HEADS, C_DH = 6, 64
C_WIDTH = C_HEADS * C_DH
N_PREV_CHUNKS = 8
MAX_REL = 128
C_QB = 2 * CHUNK
C_BWIN = C_QB + N_PREV_CHUNKS * CHUNK
C_KT = 128

ZA_W = 4 * A_WIDTH
ZB_W = 2 * B_QW + 2 * B_VW
ZC_W = 3 * C_WIDTH
Y_W = A_WIDTH + B_VW + C_WIDTH

N_GROUPS, EXPERTS_PER_GROUP = 4, 8
N_EXPERTS = N_GROUPS * EXPERTS_PER_GROUP
D_EXPERT = 512
ROUTE_W = 128

SUB = 8
V7X_VMEM_BYTES = 64 * 1024 * 1024
VMEM_LIMIT = V7X_VMEM_BYTES * 7 // 8

TM = 1024
TS_A = 1024
TQ_C = 512
MOE_BM = 512
TM_COMBINE = 512
TM_DISPATCH = TM_COMBINE


def _cparams(*sem):
    return pltpu.CompilerParams(dimension_semantics=sem, vmem_limit_bytes=VMEM_LIMIT)


def _sigmoid(x):
    return 1.0 / (1.0 + jnp.exp(-x))


def _silu(x):
    return x * _sigmoid(x)


def _split3(x):
    hi = x.astype(BF16)
    r1 = x - hi.astype(F32)
    mid = r1.astype(BF16)
    lo = (r1 - mid.astype(F32)).astype(BF16)
    return hi, mid, lo


def _dot(a, b):
    return jnp.dot(a, b, preferred_element_type=F32)


def _dot_nt(a, b):
    return lax.dot_general(a, b, (((1,), (1,)), ((), ())), preferred_element_type=F32)


def _dot_tn(a, b):
    return lax.dot_general(a, b, (((0,), (0,)), ((), ())), preferred_element_type=F32)


LANES = 128
ROW_TILE = D_MODEL // LANES


def _store_row_tiles(ref, val):
    n = val.shape[0]
    for c in range(ROW_TILE):
        ref[pl.ds(c, n, stride=ROW_TILE), :] = val[:, c * LANES:(c + 1) * LANES]


def _load_row_tiles(ref, first_row, n):
    return jnp.concatenate(
        [ref[pl.ds(first_row * ROW_TILE + c, n, stride=ROW_TILE), :] for c in range(ROW_TILE)], axis=1)


def _rms(x, w):
    ms = jnp.mean(x * x, axis=-1, keepdims=True)
    return x * lax.rsqrt(ms + NORM_EPS) * w


def _project(h, w_ref, za_ref, zb_ref, zc_ref):
    hb = h.astype(BF16)
    za_ref[...] = _dot(hb, w_ref[:, 0:ZA_W]).astype(BF16)
    zb_ref[...] = _dot(hb, w_ref[:, ZA_W:ZA_W + ZB_W]).astype(BF16)
    zc_ref[...] = _dot(hb, w_ref[:, ZA_W + ZB_W:]).astype(BF16)


def _inproj_kernel(x_ref, nw_ref, w_ref, za_ref, zb_ref, zc_ref):
    _project(_rms(x_ref[...], nw_ref[...]), w_ref, za_ref, zb_ref, zc_ref)


def _inproj(x2d, nw, w):
    T = x2d.shape[0]
    row = lambda i: (i, 0)
    fixed = lambda i: (0, 0)
    return pl.pallas_call(
        _inproj_kernel,
        grid=(T // TM,),
        in_specs=[pl.BlockSpec((TM, D_MODEL), row),
                  pl.BlockSpec((1, D_MODEL), fixed),
                  pl.BlockSpec((D_MODEL, ZA_W + ZB_W + ZC_W), fixed)],
        out_specs=[pl.BlockSpec((TM, ZA_W), row),
                   pl.BlockSpec((TM, ZB_W), row),
                   pl.BlockSpec((TM, ZC_W), row)],
        out_shape=[jax.ShapeDtypeStruct((T, ZA_W), BF16),
                   jax.ShapeDtypeStruct((T, ZB_W), BF16),
                   jax.ShapeDtypeStruct((T, ZC_W), BF16)],
        compiler_params=_cparams("parallel"),
        name="inproj",
    )(x2d, nw, w)


def _hgrn_body(z_ref, lb_ref, nw_ref, tri_ref, eones_ref, bd_ref, o_ref, st_ref):
    W = A_WIDTH
    lb = lb_ref[...]
    lane_head = lax.broadcasted_iota(jnp.int32, (1, W), 1) // A_DK
    head_masks = [lane_head == h for h in range(A_HEADS)]
    tri = tri_ref[...]
    eones = eones_ref[...]
    bd = bd_ref[...]
    jj = lax.broadcasted_iota(jnp.int32, (SUB, SUB, 1), 0)
    ii = lax.broadcasted_iota(jnp.int32, (SUB, SUB, 1), 1)
    causal = jj <= ii

    n_chunks = z_ref.shape[0] // CHUNK
    n_sub = CHUNK // SUB
    outs = []
    for c in range(n_chunks):
        r0 = c * CHUNK
        zq = z_ref[r0:r0 + CHUNK, 0:W].astype(F32)
        zf = z_ref[r0:r0 + CHUNK, W:2 * W].astype(F32)
        vv = z_ref[r0:r0 + CHUNK, 2 * W:3 * W]
        vf = vv.astype(F32)
        f = lb + (1.0 - lb) * _sigmoid(zf)
        lf = jnp.log(jnp.maximum(f, MIN_FORGET))
        kk = (1.0 - lb) * _sigmoid(-zf)
        qf = _silu(zq) * (A_DK ** -0.5)
        hi, mid, lo = _split3(lf)
        G = (_dot(tri, hi) + _dot(tri, mid) + _dot(tri, lo)) * LOG2_E
        g_last = G[CHUNK - 1:CHUNK, :]

        st = st_ref[...]
        inter = _dot_nt((qf * jnp.exp2(G)).astype(BF16), st.astype(BF16))
        khat = kk * jnp.exp2(g_last - G)
        ut = _dot_tn(vv, khat.astype(BF16))
        st_ref[...] = jnp.exp2(g_last) * st + ut * bd

        p_rows = []
        for s in range(1, n_sub):
            b = G[s * SUB - 1:s * SUB, :]
            qt = qf[s * SUB:(s + 1) * SUB, :] * jnp.exp2(G[s * SUB:(s + 1) * SUB, :] - b)
            kt = (kk[:s * SUB] * jnp.exp2(jnp.minimum(b - G[:s * SUB], 0.0))).astype(BF16)
            kt = jnp.concatenate([kt, jnp.zeros((CHUNK - s * SUB, W), BF16)], axis=0)
            lhs = jnp.concatenate([jnp.where(m, qt, 0.0) for m in head_masks], axis=0)
            p_rows.append(_dot_nt(lhs.astype(BF16), kt))
        pall = jnp.concatenate(p_rows, axis=0)
        pv = _dot(pall.astype(BF16), vv)

        pieces = []
        for s in range(n_sub):
            sl = slice(s * SUB, (s + 1) * SUB)
            gb, qb, kb, vb = G[sl], qf[sl], kk[sl], vf[sl]
            dec = jnp.exp2(jnp.minimum(gb[None, :, :] - gb[:, None, :], 0.0))
            a = (qb[None, :, :] * kb[:, None, :]) * dec
            r = _dot(a.reshape(SUB * SUB, W).astype(BF16), eones).reshape(SUB, SUB, W)
            diag = jnp.sum(jnp.where(causal, r, 0.0) * vb[:, None, :], axis=0)
            piece = inter[sl] + diag
            if s > 0:
                base = (s - 1) * A_HEADS * SUB
                for h in range(A_HEADS):
                    blk = pv[base + h * SUB:base + (h + 1) * SUB, :]
                    piece = piece + jnp.where(head_masks[h], blk, 0.0)
            pieces.append(piece)
        outs.append(jnp.concatenate(pieces, axis=0))

    o = jnp.concatenate(outs, axis=0)
    g = z_ref[:, 3 * W:4 * W].astype(F32)
    o_ref[...] = (_rms(o, nw_ref[...]) * _silu(g)).astype(BF16)


N_HGRN_IN = 6


def _recurrent_mixers_kernel(*refs):
    n_in = N_HGRN_IN + 9
    ins, (ya_ref, yb_ref, st_a, st_b) = refs[:n_in], refs[n_in:]

    @pl.when(pl.program_id(1) == 0)
    def _():
        st_a[...] = jnp.zeros_like(st_a)
        st_b[...] = jnp.zeros_like(st_b)

    _hgrn_body(*ins[:N_HGRN_IN], ya_ref, st_a)
    _ret_body(*ins[N_HGRN_IN:], yb_ref, st_b)


def _recurrent_mixers(za, lb, nw_a, tri, eones, bd_a, zb, cos, sin, consts, nw_b, B, S):
    nt = S // TS_A
    fixed = lambda b, t: (0, 0)
    tile = lambda b, t: (b * nt + t, 0)
    dstack, qdec, kdec, cg, bd_b = consts
    return pl.pallas_call(
        _recurrent_mixers_kernel,
        grid=(B, nt),
        in_specs=[pl.BlockSpec((TS_A, ZA_W), tile),
                  pl.BlockSpec((1, A_WIDTH), fixed),
                  pl.BlockSpec((1, A_WIDTH), fixed),
                  pl.BlockSpec((CHUNK, CHUNK), fixed),
                  pl.BlockSpec((A_WIDTH, A_WIDTH), fixed),
                  pl.BlockSpec((A_WIDTH, A_WIDTH), fixed),
                  pl.BlockSpec((TS_A, ZB_W), tile),
                  pl.BlockSpec((TS_A, B_QW // 2), lambda b, t: (t, 0)),
                  pl.BlockSpec((TS_A, B_QW // 2), lambda b, t: (t, 0)),
                  pl.BlockSpec(dstack.shape, fixed),
                  pl.BlockSpec(qdec.shape, fixed),
                  pl.BlockSpec(kdec.shape, fixed),
                  pl.BlockSpec(cg.shape, fixed),
                  pl.BlockSpec(bd_b.shape, fixed),
                  pl.BlockSpec((1, B_VW), fixed)],
        out_specs=[pl.BlockSpec((TS_A, A_WIDTH), tile), pl.BlockSpec((TS_A, B_VW), tile)],
        out_shape=[jax.ShapeDtypeStruct((B * S, A_WIDTH), BF16),
                   jax.ShapeDtypeStruct((B * S, B_VW), BF16)],
        scratch_shapes=[pltpu.VMEM((A_WIDTH, A_WIDTH), F32), pltpu.VMEM((B_VW, B_QW), F32)],
        compiler_params=_cparams("parallel", "arbitrary"),
        name="recurrent_mixers",
    )(za, lb, nw_a, tri, eones, bd_a, zb, cos, sin, dstack, qdec, kdec, cg, bd_b, nw_b)


def _ret_body(z_ref, cos_ref, sin_ref, dstack_ref, qdec_ref, kdec_ref, cg_ref, bd_ref, nw_ref,
              o_ref, st_ref):
    H = B_QW // 2
    lane_head = (lax.broadcasted_iota(jnp.int32, (1, B_QW), 1) % H) // B_QSLOT
    head_masks = [lane_head == h for h in range(B_HEADS)]
    lane_real = lax.broadcasted_iota(jnp.int32, (1, B_VSLOT), 1) < B_DV
    dstack = dstack_ref[...]
    qdec, kdec = qdec_ref[...], kdec_ref[...]
    cg, bd = cg_ref[...], bd_ref[...]

    n_chunks = z_ref.shape[0] // CHUNK
    outs = []
    for c in range(n_chunks):
        rows = slice(c * CHUNK, (c + 1) * CHUNK)
        cos, sin = cos_ref[rows, :], sin_ref[rows, :]

        def rot(off):
            t1 = z_ref[rows, off:off + H].astype(F32)
            t2 = z_ref[rows, off + H:off + 2 * H].astype(F32)
            return jnp.concatenate([t1 * cos - t2 * sin, t1 * sin + t2 * cos], axis=1)

        qr = rot(0)
        kr = rot(B_QW) * (B_DK ** -0.5)
        vv = z_ref[rows, 2 * B_QW:2 * B_QW + B_VW]

        lhs = jnp.concatenate([jnp.where(m, qr, 0.0) for m in head_masks], axis=0)
        sc = _dot_nt(lhs.astype(BF16), kr.astype(BF16)) * dstack
        pv = _dot(sc.astype(BF16), vv)
        intra = jnp.concatenate(
            [pv[h * CHUNK:(h + 1) * CHUNK, h * B_VSLOT:(h + 1) * B_VSLOT] for h in range(B_HEADS)],
            axis=1)
        st = st_ref[...]
        inter = _dot_nt((qr * qdec).astype(BF16), st.astype(BF16))
        ut = _dot_tn(vv, (kr * kdec).astype(BF16))
        st_ref[...] = cg * st + ut * bd
        outs.append(intra + inter)

    o = jnp.concatenate(outs, axis=0)
    normed = []
    for h in range(B_HEADS):
        oh = o[:, h * B_VSLOT:(h + 1) * B_VSLOT]
        mu = jnp.sum(oh, axis=-1, keepdims=True) * (1.0 / B_DV)
        d = jnp.where(lane_real, oh - mu, 0.0)
        var = jnp.sum(d * d, axis=-1, keepdims=True) * (1.0 / B_DV)
        normed.append(d * lax.rsqrt(var + NORM_EPS))
    y = jnp.concatenate(normed, axis=1)
    g = z_ref[:, 2 * B_QW + B_VW:].astype(F32)
    o_ref[...] = (y * nw_ref[...] * _silu(g)).astype(BF16)


def _attn_kernel(q_ref, kp_ref, kc_ref, vp_ref, vc_ref, bias_ref, o_ref, kwin_ref, vwin_ref):
    t = pl.program_id(1)
    kwin_ref[0:TQ_C, :] = kp_ref[...]
    kwin_ref[TQ_C:, :] = kc_ref[...]
    vwin_ref[0:TQ_C, :] = vp_ref[...]
    vwin_ref[TQ_C:, :] = vc_ref[...]
    jcol = lax.broadcasted_iota(jnp.int32, (1, C_BWIN), 1)
    first_head = lax.broadcasted_iota(jnp.int32, (1, 2 * C_DH), 1) < C_DH
    zero = jnp.zeros((), BF16)

    def block(i, carry):
        r0 = pl.multiple_of(i * C_QB, C_QB)
        pen = jnp.where((jcol + (r0 + (t - 1) * TQ_C)) >= 0, 0.0, MASK_VALUE)
        for p in range(C_HEADS // 2):
            lanes = slice(2 * p * C_DH, 2 * (p + 1) * C_DH)
            q = q_ref[pl.ds(r0, C_QB), lanes] * (C_DH ** -0.5)
            lhs = jnp.concatenate([jnp.where(first_head, q, zero), jnp.where(first_head, zero, q)], axis=0)

            def scores(j):
                k = kwin_ref[pl.ds(r0 + j * C_KT, C_KT), lanes]
                return (_dot_nt(lhs, k) + bias_ref[p, :, j * C_KT:(j + 1) * C_KT]
                        + pen[:, j * C_KT:(j + 1) * C_KT])

            m = scores(0)
            for j in range(1, C_BWIN // C_KT):
                m = jnp.maximum(m, scores(j))
            m = jnp.max(m, axis=-1, keepdims=True)
            acc = l = None
            for j in range(C_BWIN // C_KT):
                e = jnp.exp(scores(j) - m)
                pvj = _dot(e.astype(BF16), vwin_ref[pl.ds(r0 + j * C_KT, C_KT), lanes])
                acc = pvj if acc is None else acc + pvj
                l = e if l is None else l + e
            pv = acc / jnp.sum(l, axis=-1, keepdims=True)
            o = jnp.where(first_head, pv[:C_QB], pv[C_QB:])
            o_ref[pl.ds(r0, C_QB), lanes] = o.astype(BF16)
        return carry

    lax.fori_loop(0, TQ_C // C_QB, block, 0, unroll=True)


def _attention(zc, bias, B, S):
    nt = S // TQ_C
    cur = lambda col: (lambda b, t: (b * nt + t, col))
    prev = lambda col: (lambda b, t: (b * nt + jnp.maximum(t - 1, 0), col))
    blk = (TQ_C, C_WIDTH)
    return pl.pallas_call(
        _attn_kernel,
        grid=(B, nt),
        in_specs=[pl.BlockSpec(blk, cur(0)),
                  pl.BlockSpec(blk, prev(1)), pl.BlockSpec(blk, cur(1)),
                  pl.BlockSpec(blk, prev(2)), pl.BlockSpec(blk, cur(2)),
                  pl.BlockSpec(bias.shape, lambda b, t: (0, 0, 0))],
        out_specs=pl.BlockSpec(blk, cur(0)),
        out_shape=jax.ShapeDtypeStruct((B * S, C_WIDTH), BF16),
        scratch_shapes=[pltpu.VMEM((2 * TQ_C, C_WIDTH), BF16),
                        pltpu.VMEM((2 * TQ_C, C_WIDTH), BF16)],
        compiler_params=_cparams("parallel", "arbitrary"),
        name="chunk_attention",
    )(zc, zc, zc, zc, zc, bias)


def _route(logits):
    lane = lax.broadcasted_iota(jnp.int32, logits.shape, 1).astype(F32)
    big = float(1 << 20)
    neg = -jnp.inf

    def first_argmax(vals):
        m = jnp.max(vals, axis=-1, keepdims=True)
        idx = jnp.min(jnp.where(vals == m, lane, big), axis=-1, keepdims=True)
        return m, idx

    gl = jnp.where(lane < N_GROUPS, logits, neg)
    gm, grp = first_argmax(gl)
    p_grp = 1.0 / jnp.sum(jnp.exp(gl - gm), axis=-1, keepdims=True)
    lo = N_GROUPS + grp * EXPERTS_PER_GROUP
    el = jnp.where((lane >= lo) & (lane < lo + EXPERTS_PER_GROUP), logits, neg)
    v1, i1 = first_argmax(el)
    v2, i2 = first_argmax(jnp.where(lane == i1, neg, el))
    e2 = jnp.exp(v2 - v1)
    g1 = p_grp / (1.0 + e2)
    g2 = p_grp * e2 / (1.0 + e2)
    out = jnp.where(lane == 0, i1 - N_GROUPS, 0.0)
    out = jnp.where(lane == 1, i2 - N_GROUPS, out)
    out = jnp.where(lane == 2, g1, out)
    out = jnp.where(lane == 3, g2, out)
    return out


def _outproj_kernel(x_ref, ya_ref, yb_ref, yc_ref, w_ref, nw_ref, wr_ref, xo_ref, r_ref):
    a0, a1 = A_WIDTH, A_WIDTH + B_VW
    x = x_ref[...]
    x = x + _dot(ya_ref[...], w_ref[0:a0, :])
    x = x + _dot(yb_ref[...], w_ref[a0:a1, :])
    x = x + _dot(yc_ref[...], w_ref[a1:, :])
    xo_ref[...] = x
    h = _rms(x, nw_ref[...])
    h_hi = h.astype(BF16)
    h_lo = (h - h_hi.astype(F32)).astype(BF16)
    d_hi = _dot(h_hi, wr_ref[...])
    logits = d_hi + pltpu.roll(d_hi, ROUTE_W // 2, 1) + _dot(h_lo, wr_ref[...])
    r_ref[...] = _route(logits)


def _outproj(x2d, ya, yb, yc, w, nw, wr3):
    T = x2d.shape[0]
    row = lambda i: (i, 0)
    fixed = lambda i: (0, 0)
    return pl.pallas_call(
        _outproj_kernel,
        grid=(T // TM,),
        in_specs=[pl.BlockSpec((TM, D_MODEL), row),
                  pl.BlockSpec((TM, A_WIDTH), row),
                  pl.BlockSpec((TM, B_VW), row),
                  pl.BlockSpec((TM, C_WIDTH), row),
                  pl.BlockSpec((Y_W, D_MODEL), fixed),
                  pl.BlockSpec((1, D_MODEL), fixed),
                  pl.BlockSpec((D_MODEL, ROUTE_W), fixed)],
        out_specs=[pl.BlockSpec((TM, D_MODEL), row),
                   pl.BlockSpec((TM, ROUTE_W), row)],
        out_shape=[jax.ShapeDtypeStruct((T, D_MODEL), F32),
                   jax.ShapeDtypeStruct((T, ROUTE_W), F32)],
        compiler_params=_cparams("parallel"),
        name="outproj_router",
    )(x2d, ya, yb, yc, w, nw, wr3)


PAD_PIECES = tuple(MOE_BM >> (k + 1) for k in range(MOE_BM.bit_length() - 1))
DMA_GROUP = 8


def _tile_rows(ref, first_tile_row, n_rows=1):
    return ref.at[pl.ds(pl.multiple_of(first_tile_row, ROW_TILE), n_rows * ROW_TILE)]


def _rows_done(src_hbm, dst, sem, n_rows):
    pltpu.make_async_copy(_tile_rows(src_hbm, 0, n_rows), _tile_rows(dst, 0, n_rows), sem).wait()


def _dispatch_kernel(cnt_ref, pstart_ref, pos_ref, x_ref, nw_ref, xs_hbm, hbuf, zbuf, sem, zsem):
    i = pl.program_id(0)
    n = pl.num_programs(0)
    tm = TM_DISPATCH
    cur = lax.rem(i, 2)
    stage = hbuf.at[cur]
    _store_row_tiles(stage, _rms(x_ref[...], nw_ref[...]))

    def pad_pieces(e):
        cnt = cnt_ref[e]
        first = pstart_ref[e] + cnt
        n_pad = (-cnt) & (MOE_BM - 1)
        for p in PAD_PIECES:
            yield (n_pad & p) != 0, first + (n_pad & (MOE_BM - 2 * p)), p

    def spare_blocks():
        last = N_EXPERTS - 1
        n_used = (pstart_ref[last] + cnt_ref[last] + (MOE_BM - 1)) // MOE_BM
        n_blocks = xs_hbm.shape[0] // (MOE_BM * ROW_TILE)
        for b in range(n_blocks - N_EXPERTS, n_blocks):
            for first in range(0, MOE_BM, PAD_PIECES[0]):
                yield b >= n_used, b * MOE_BM + first, PAD_PIECES[0]

    @pl.when(i == 0)
    def _():
        zbuf[...] = jnp.zeros_like(zbuf)
        for phase in ("start", "wait"):
            for e in range(N_EXPERTS + 1):
                for fire, slot, p in (pad_pieces(e) if e < N_EXPERTS else spare_blocks()):
                    @pl.when(fire)
                    def _():
                        cp = pltpu.make_async_copy(_tile_rows(zbuf, 0, p),
                                                   _tile_rows(xs_hbm, slot * ROW_TILE, p), zsem)
                        cp.start() if phase == "start" else cp.wait()

    def body(g, carry):
        rows = [g * DMA_GROUP + k for k in range(DMA_GROUP)]
        slots = [(pos_ref[0, 0, r], pos_ref[0, 0, tm + r]) for r in rows]
        for r, (s0, s1) in zip(rows, slots):
            src = _tile_rows(stage, r * ROW_TILE)
            pltpu.make_async_copy(src, _tile_rows(xs_hbm, s0), sem.at[cur]).start(priority=0)
            pltpu.make_async_copy(src, _tile_rows(xs_hbm, s1), sem.at[cur]).start(priority=1)
        return carry
    lax.fori_loop(0, tm // DMA_GROUP, body, 0)

    @pl.when(i > 0)
    def _():
        _rows_done(stage, xs_hbm, sem.at[1 - cur], 2 * tm)

    @pl.when(i == n - 1)
    def _():
        _rows_done(stage, xs_hbm, sem.at[cur], 2 * tm)


def _dispatch(x2d, nw, pos, counts, pad_start, n_slots):
    T = x2d.shape[0]
    tm = TM_DISPATCH
    grid_spec = pltpu.PrefetchScalarGridSpec(
        num_scalar_prefetch=2,
        grid=(T // tm,),
        in_specs=[pl.BlockSpec((1, 1, 2 * tm), lambda i, c, s: (i, 0, 0), memory_space=pltpu.SMEM),
                  pl.BlockSpec((tm, D_MODEL), lambda i, c, s: (i, 0)),
                  pl.BlockSpec((1, D_MODEL), lambda i, c, s: (0, 0))],
        out_specs=pl.BlockSpec(memory_space=pl.ANY),
        scratch_shapes=[pltpu.VMEM((2, tm * ROW_TILE, LANES), F32),
                        pltpu.VMEM((PAD_PIECES[0] * ROW_TILE, LANES), F32),
                        pltpu.SemaphoreType.DMA((2,)), pltpu.SemaphoreType.DMA(())],
    )
    return pl.pallas_call(
        _dispatch_kernel,
        grid_spec=grid_spec,
        out_shape=jax.ShapeDtypeStruct((n_slots * ROW_TILE, LANES), F32),
        compiler_params=_cparams("arbitrary"),
        name="dispatch",
    )(counts, pad_start, pos, x2d, nw)


def _expert_kernel(blk_exp_ref, next_exp_ref, nused_ref, x_ref, wg_hbm, wu_hbm, wd_hbm, ys_hbm,
                   wg_f, wu_f, wd_f, wg_bf, wu_bf, wd_bf, ybuf, wsem, ysem, *, layer):
    i = pl.program_id(0)
    n = pl.num_programs(0)
    used = i < nused_ref[0]
    slot = lax.rem(i, 2)
    block_rows = MOE_BM * ROW_TILE

    def weight_copies(e):
        return (pltpu.make_async_copy(wg_hbm.at[layer, e], wg_f, wsem.at[0]),
                pltpu.make_async_copy(wu_hbm.at[layer, e], wu_f, wsem.at[1]),
                pltpu.make_async_copy(wd_hbm.at[layer, e], wd_f, wsem.at[2]))

    def result_copy(s, block):
        dst = ys_hbm.at[pl.ds(pl.multiple_of(block * block_rows, block_rows), block_rows)]
        return pltpu.make_async_copy(ybuf.at[s], dst, ysem.at[s])

    @pl.when(i == 0)
    def _():
        for cp in weight_copies(blk_exp_ref[0]):
            cp.start(priority=1)

    @pl.when((i == 0) | (blk_exp_ref[i] != blk_exp_ref[jnp.maximum(i - 1, 0)]))
    def _():
        for cp in weight_copies(blk_exp_ref[i]):
            cp.wait()
        wg_bf[...] = wg_f[...].astype(BF16)
        wu_bf[...] = wu_f[...].astype(BF16)
        wd_bf[...] = wd_f[...].astype(BF16)

        @pl.when(next_exp_ref[i] >= 0)
        def _():
            for cp in weight_copies(next_exp_ref[i]):
                cp.start(priority=1)

    @pl.when(i >= 2)
    def _():
        result_copy(slot, i - 2).wait()

    @pl.when(used)
    def _():
        xb = _load_row_tiles(x_ref, 0, MOE_BM).astype(BF16)
        act = _silu(_dot(xb, wg_bf[...])) * _dot(xb, wu_bf[...])
        _store_row_tiles(ybuf.at[slot], _dot(act.astype(BF16), wd_bf[...]))

    @pl.when(jnp.logical_not(used))
    def _():
        ybuf[slot] = jnp.zeros((block_rows, LANES), F32)

    result_copy(slot, i).start(priority=1)

    @pl.when(i == n - 1)
    def _():
        result_copy(1 - slot, i - 1).wait()
        result_copy(slot, i).wait()


def _experts(xs, blk_exp, next_exp, nused, layer, w_gate, w_up, w_down):
    nblk = blk_exp.shape[0]
    assert nblk >= 2
    last_used = lambda i, nu: jnp.minimum(i, nu[0] - 1)
    grid_spec = pltpu.PrefetchScalarGridSpec(
        num_scalar_prefetch=3,
        grid=(nblk,),
        in_specs=[pl.BlockSpec((MOE_BM * ROW_TILE, LANES), lambda i, be, nx, nu: (last_used(i, nu), 0)),
                  pl.BlockSpec(memory_space=pl.ANY),
                  pl.BlockSpec(memory_space=pl.ANY),
                  pl.BlockSpec(memory_space=pl.ANY)],
        out_specs=pl.BlockSpec(memory_space=pl.ANY),
        scratch_shapes=[pltpu.VMEM((D_MODEL, D_EXPERT), F32), pltpu.VMEM((D_MODEL, D_EXPERT), F32),
                        pltpu.VMEM((D_EXPERT, D_MODEL), F32),
                        pltpu.VMEM((D_MODEL, D_EXPERT), BF16), pltpu.VMEM((D_MODEL, D_EXPERT), BF16),
                        pltpu.VMEM((D_EXPERT, D_MODEL), BF16),
                        pltpu.VMEM((2, MOE_BM * ROW_TILE, LANES), F32),
                        pltpu.SemaphoreType.DMA((3,)), pltpu.SemaphoreType.DMA((2,))],
    )
    return pl.pallas_call(
        functools.partial(_expert_kernel, layer=layer),
        grid_spec=grid_spec,
        out_shape=jax.ShapeDtypeStruct((nblk * MOE_BM * ROW_TILE, LANES), F32),
        compiler_params=_cparams("arbitrary"),
        name="experts",
    )(blk_exp, next_exp, nused, xs, w_gate, w_up, w_down)


def _combine_rows(pos_ref, pos_next_ref, x_ref, r_ref, y_hbm, ybuf, sem, request_next_first):
    i = pl.program_id(0)
    n_rows = 2 * TM_COMBINE
    slot = lax.rem(i, 2)

    @pl.when(i == 0)
    def _():
        _request_rows(pos_ref, y_hbm, ybuf, sem, 0, unrolled=False)

    if request_next_first:
        _request_next(pos_next_ref, y_hbm, ybuf, sem, unrolled=False)
    _rows_done(y_hbm, ybuf.at[slot], sem.at[slot], n_rows)
    r = r_ref[...]
    g0, g1 = r[:, 2:3], r[:, 3:4]
    y0 = _load_row_tiles(ybuf.at[slot], 0, TM_COMBINE)
    y1 = _load_row_tiles(ybuf.at[slot], TM_COMBINE, TM_COMBINE)
    return x_ref[...] + (g0 * y0 + g1 * y1)


def _request_rows(idx_ref, y_hbm, ybuf, sem, s, unrolled):
    n_rows = 2 * TM_COMBINE
    group = 2 * DMA_GROUP

    def body(g, carry):
        rows = [g * group + k for k in range(group)]
        slots = [idx_ref[0, 0, r] for r in rows]
        for k, src_row in enumerate(slots):
            pltpu.make_async_copy(_tile_rows(y_hbm, src_row),
                                  _tile_rows(ybuf.at[s], rows[k] * ROW_TILE), sem.at[s]).start(priority=k % 2)
        return carry

    if unrolled:
        for g in range(n_rows // group):
            body(g, 0)
    else:
        lax.fori_loop(0, n_rows // group, body, 0)


def _request_next(pos_next_ref, y_hbm, ybuf, sem, unrolled):
    _request_rows(pos_next_ref, y_hbm, ybuf, sem, 1 - lax.rem(pl.program_id(0), 2), unrolled)


def _drain_last(y_hbm, ybuf, sem):
    i = pl.program_id(0)

    @pl.when(i == pl.num_programs(0) - 1)
    def _():
        other = 1 - lax.rem(i, 2)
        _rows_done(y_hbm, ybuf.at[other], sem.at[other], 2 * TM_COMBINE)


def _combine_inproj_kernel(pos_ref, pos_next_ref, x_ref, r_ref, y_hbm, nw_ref, w_ref,
                           xo_ref, za_ref, zb_ref, zc_ref, ybuf, sem):
    x = _combine_rows(pos_ref, pos_next_ref, x_ref, r_ref, y_hbm, ybuf, sem, request_next_first=False)
    xo_ref[...] = x
    _request_next(pos_next_ref, y_hbm, ybuf, sem, unrolled=True)
    _project(_rms(x, nw_ref[...]), w_ref, za_ref, zb_ref, zc_ref)
    _drain_last(y_hbm, ybuf, sem)


def _combine_final_kernel(pos_ref, pos_next_ref, x_ref, r_ref, y_hbm, nw_ref, o_ref, ybuf, sem):
    x = _combine_rows(pos_ref, pos_next_ref, x_ref, r_ref, y_hbm, ybuf, sem, request_next_first=True)
    o_ref[...] = _rms(x, nw_ref[...])
    _drain_last(y_hbm, ybuf, sem)


def _combine(x2d, route, ys, pos, nw, w=None):
    T = x2d.shape[0]
    tm = TM_COMBINE
    row = lambda i: (i, 0)
    fixed = lambda i: (0, 0)
    n_tiles = T // tm
    in_specs = [pl.BlockSpec((1, 1, 2 * tm), lambda i: (i, 0, 0), memory_space=pltpu.SMEM),
                pl.BlockSpec((1, 1, 2 * tm), lambda i: (jnp.minimum(i + 1, n_tiles - 1), 0, 0),
                             memory_space=pltpu.SMEM),
                pl.BlockSpec((tm, D_MODEL), row),
                pl.BlockSpec((tm, ROUTE_W), row),
                pl.BlockSpec(memory_space=pl.ANY),
                pl.BlockSpec((1, D_MODEL), fixed)]
    scratch = [pltpu.VMEM((2, 2 * tm * ROW_TILE, LANES), F32), pltpu.SemaphoreType.DMA((2,))]
    if w is None:
        return pl.pallas_call(
            _combine_final_kernel,
            grid=(T // tm,),
            in_specs=in_specs,
            out_specs=pl.BlockSpec((tm, D_MODEL), row),
            out_shape=jax.ShapeDtypeStruct((T, D_MODEL), F32),
            scratch_shapes=scratch,
            compiler_params=_cparams("arbitrary"),
            name="combine_final",
        )(pos, pos, x2d, route, ys, nw)
    return pl.pallas_call(
        _combine_inproj_kernel,
        grid=(T // tm,),
        in_specs=in_specs + [pl.BlockSpec((D_MODEL, ZA_W + ZB_W + ZC_W), fixed)],
        out_specs=[pl.BlockSpec((tm, D_MODEL), row),
                   pl.BlockSpec((tm, ZA_W), row),
                   pl.BlockSpec((tm, ZB_W), row),
                   pl.BlockSpec((tm, ZC_W), row)],
        out_shape=[jax.ShapeDtypeStruct((T, D_MODEL), F32),
                   jax.ShapeDtypeStruct((T, ZA_W), BF16),
                   jax.ShapeDtypeStruct((T, ZB_W), BF16),
                   jax.ShapeDtypeStruct((T, ZC_W), BF16)],
        scratch_shapes=scratch,
        compiler_params=_cparams("arbitrary"),
        name="combine_inproj",
    )(pos, pos, x2d, route, ys, nw, w)


def _pad_value_heads(t, axis):
    shape = t.shape
    t = t.reshape(shape[:axis] + (B_HEADS, B_DV) + shape[axis + 1:])
    pad = [(0, 0)] * t.ndim
    pad[axis + 1] = (0, B_VSLOT - B_DV)
    return jnp.pad(t, pad).reshape(shape[:axis] + (B_VW,) + shape[axis + 1:])


def _layout_in_weights(w):
    D = w.shape[0]
    nqk, nv = B_HEADS * B_DK, B_HEADS * B_DV
    o = ZA_W

    def qk(block):
        t = block.reshape(D, B_HEADS, 2, B_HALF).transpose(0, 2, 1, 3)
        t = jnp.pad(t, ((0, 0), (0, 0), (0, 0), (0, B_QSLOT - B_HALF)))
        return t.reshape(D, B_QW)

    return jnp.concatenate([
        w[:, :o], qk(w[:, o:o + nqk]), qk(w[:, o + nqk:o + 2 * nqk]),
        _pad_value_heads(w[:, o + 2 * nqk:o + 2 * nqk + nv], 1),
        _pad_value_heads(w[:, o + 2 * nqk + nv:o + 2 * nqk + 2 * nv], 1),
        w[:, o + 2 * nqk + 2 * nv:]], axis=1)


def _layout_out_weights(w):
    nv = B_HEADS * B_DV
    return jnp.concatenate([w[:A_WIDTH], _pad_value_heads(w[A_WIDTH:A_WIDTH + nv], 0),
                            w[A_WIDTH + nv:]], axis=0)


def _retention_tables(S):
    log_gamma = jnp.log1p(-jnp.exp2(-5.0 - jnp.arange(B_HEADS, dtype=F32)))
    idx = jnp.arange(CHUNK, dtype=F32)
    rel = idx[:, None] - idx[None, :]
    decay = jnp.where(rel >= 0, jnp.exp(log_gamma[:, None, None] * jnp.maximum(rel, 0.0)), 0.0)
    dstack = decay.reshape(B_HEADS * CHUNK, CHUNK)
    k_decay = jnp.exp(log_gamma[:, None] * (CHUNK - 1.0 - idx)[None, :])
    q_decay = jnp.exp(log_gamma[:, None] * (idx + 1.0)[None, :])
    chunk_gamma = jnp.exp(log_gamma * CHUNK)

    def lanes(per_head):
        t = jnp.repeat(per_head[..., None], B_QSLOT, axis=-1)
        t = t.reshape(per_head.shape[:-1] + (B_HEADS * B_QSLOT,))
        return jnp.concatenate([t, t], axis=-1)

    qdec = lanes(q_decay.T)
    kdec = lanes(k_decay.T)
    cg = lanes(chunk_gamma[None, :])
    row_head = np.arange(B_VW) // B_VSLOT
    lane_head = (np.arange(B_QW) % (B_QW // 2)) // B_QSLOT
    bd = jnp.asarray((row_head[:, None] == lane_head[None, :]).astype(np.float32))

    half = B_HALF
    inv_freq = ROPE_BASE ** (-jnp.arange(half, dtype=F32) / half)
    ang = jnp.arange(S).astype(F32)[:, None] * inv_freq[None, :]
    pad = ((0, 0), (0, B_QSLOT - half))
    cos = jnp.tile(jnp.pad(jnp.cos(ang), pad), (1, B_HEADS))
    sin = jnp.tile(jnp.pad(jnp.sin(ang), pad), (1, B_HEADS))
    return cos, sin, (dstack, qdec, kdec, cg, bd)


def _attention_bias(rel_bias):
    qi = np.arange(C_QB)[:, None]
    km = np.arange(C_BWIN)[None, :]
    lag = (qi // CHUNK + N_PREV_CHUNKS) - km // CHUNK
    in_band = jnp.asarray((lag >= 0) & (lag <= N_PREV_CHUNKS))
    P = C_QB + C_BWIN
    t = np.arange(P)
    t = np.where(t < C_BWIN, t, t - P)
    dist = N_PREV_CHUNKS * CHUNK - t
    row = rel_bias.astype(F32)[:, np.clip(dist, -MAX_REL, MAX_REL) + MAX_REL]
    b = jnp.tile(row, (1, C_QB))[:, :C_QB * (P - 1)].reshape(C_HEADS, C_QB, P - 1)[:, :, :C_BWIN]
    b = jnp.where(in_band[None], b, MASK_VALUE)
    return b.reshape(C_HEADS // 2, 2 * C_QB, C_BWIN)


def _hgrn_tables():
    tri = jnp.asarray(np.tril(np.ones((CHUNK, CHUNK), np.float32))).astype(BF16)
    head = np.arange(A_WIDTH) // A_DK
    same = (head[:, None] == head[None, :]).astype(np.float32)
    return tri, jnp.asarray(same).astype(BF16), jnp.asarray(same)


def _moe_plan(route, T):
    nblk = 2 * T // MOE_BM + N_EXPERTS
    experts = jnp.arange(N_EXPERTS, dtype=jnp.int32)[None, :]
    oh0 = (route[:, 0:1].astype(jnp.int32) == experts).astype(jnp.int32)
    oh1 = (route[:, 1:2].astype(jnp.int32) == experts).astype(jnp.int32)
    c0, c1 = jnp.cumsum(oh0, axis=0), jnp.cumsum(oh1, axis=0)
    first_choice = c0[-1]
    counts = first_choice + c1[-1]
    padded = (counts + MOE_BM - 1) // MOE_BM * MOE_BM
    pad_end = jnp.cumsum(padded)
    pad_start = pad_end - padded
    dest0 = jnp.sum(oh0 * (pad_start[None, :] + c0 - oh0), axis=1)
    dest1 = jnp.sum(oh1 * ((pad_start + first_choice)[None, :] + c1 - oh1), axis=1)
    nused = (pad_end[-1] // MOE_BM).astype(jnp.int32)
    blk_start = jnp.arange(nblk, dtype=jnp.int32) * MOE_BM
    blk_exp = jnp.sum((pad_end[None, :] <= blk_start[:, None]).astype(jnp.int32), axis=1)
    last = jnp.take(blk_exp, jnp.maximum(nused - 1, 0))
    blk_used = jnp.arange(nblk) < nused
    blk_exp = jnp.where(blk_used, jnp.minimum(blk_exp, N_EXPERTS - 1), last)
    later = blk_used[None, :] & (blk_exp[None, :] > blk_exp[:, None])
    next_exp = jnp.min(jnp.where(later, blk_exp[None, :], N_EXPERTS), axis=1)
    next_exp = jnp.where(next_exp < N_EXPERTS, next_exp, -1).astype(jnp.int32)
    tm = TM_COMBINE
    pos = jnp.concatenate([dest0.reshape(T // tm, 1, tm), dest1.reshape(T // tm, 1, tm)], axis=2) * ROW_TILE
    return pos, counts.astype(jnp.int32), pad_start.astype(jnp.int32), blk_exp, next_exp, nused.reshape(1)


def kernel(x, w_in, w_out, norm_mix, norm_ffn, norm_final, hgrn_lb, hgrn_norm, ret_norm, rel_bias,
           router_group, router_expert, expert_w_gate, expert_w_up, expert_w_down):
    B, S, D = x.shape
    T = B * S
    depth = w_in.shape[0]
    assert D == D_MODEL and w_in.shape[1:] == (D_MODEL, 4 * A_WIDTH + 2 * B_HEADS * (B_DK + B_DV) + ZC_W)
    assert w_out.shape[1:] == (A_WIDTH + B_HEADS * B_DV + C_WIDTH, D_MODEL)
    assert expert_w_gate.shape[1:] == (N_EXPERTS, D_MODEL, D_EXPERT)
    assert router_group.shape[2] == N_GROUPS and router_expert.shape[2] == N_EXPERTS
    assert rel_bias.shape == (C_HEADS, 2 * MAX_REL + 1)
    assert S % max(TS_A, TQ_C) == 0 and TQ_C == N_PREV_CHUNKS * CHUNK
    assert T % max(TM, TM_COMBINE) == 0 and (2 * T) % MOE_BM == 0

    p = jax.nn.softmax(hgrn_lb.astype(F32), axis=0)
    lower_bounds = jnp.clip(jnp.cumsum(p, axis=0) - p[0], 0.0, 1.0)

    cos, sin, ret_consts = _retention_tables(S)
    bias = _attention_bias(rel_bias)
    tri, eones, bd_a = _hgrn_tables()

    x2d = x.reshape(T, D).astype(F32)
    route = ys = pos = None
    out = None
    for l in range(depth):
        w_in_l = _layout_in_weights(w_in[l]).astype(BF16)
        w_out_l = _layout_out_weights(w_out[l]).astype(BF16)
        ret_nw = _pad_value_heads(ret_norm[l].astype(F32), 0).reshape(1, B_VW)
        nw_mix = norm_mix[l].astype(F32).reshape(1, D)
        if l == 0:
            za, zb, zc = _inproj(x2d, nw_mix, w_in_l)
        else:
            x2d, za, zb, zc = _combine(x2d, route, ys, pos, nw_mix, w_in_l)

        ya, yb = _recurrent_mixers(za, lower_bounds[l].reshape(1, A_WIDTH),
                                   hgrn_norm[l].astype(F32).reshape(1, A_WIDTH), tri, eones, bd_a,
                                   zb, cos, sin, ret_consts, ret_nw, B, S)
        yc = _attention(zc, bias, B, S)

        wr = jnp.concatenate([router_group[l], router_expert[l]], axis=1).astype(F32)
        wr_hi = wr.astype(BF16)
        wr_lo = (wr - wr_hi.astype(F32)).astype(BF16)
        gap = jnp.zeros((D, ROUTE_W // 2 - wr.shape[1]), BF16)
        wr2 = jnp.concatenate([wr_hi, gap, wr_lo, gap], axis=1)
        nw_ffn = norm_ffn[l].astype(F32).reshape(1, D)
        x2d, route = _outproj(x2d, ya, yb, yc, w_out_l, nw_ffn, wr2)

        pos, counts, pad_start, blk_exp, next_exp, nused = _moe_plan(route, T)
        xs = _dispatch(x2d, nw_ffn, pos, counts, pad_start, blk_exp.shape[0] * MOE_BM)
        ys = _experts(xs, blk_exp, next_exp, nused, l, expert_w_gate, expert_w_up, expert_w_down)

    out = _combine(x2d, route, ys, pos, norm_final.astype(F32).reshape(1, D))
    return out.reshape(B, S, D).astype(x.dtype)
```
